```python
import math
import jax
import jax.numpy as jnp
from jax import lax
import numpy as np

D_MODEL = 2048
BATCH = 2
SEQ = 4096
DEPTH = 1
DEC_BATCH = 128
DEC_SEQ = 8
PAST_LEN = 2048
PAGE_SIZE = 128

HEAD_DIM = 64
NSA_HEADS = 16
NSA_KV_HEADS = 4
NSA_GROUP = NSA_HEADS // NSA_KV_HEADS
NSA_WIDTH = NSA_HEADS * HEAD_DIM
KV_WIDTH = NSA_KV_HEADS * HEAD_DIM
CMP_BLOCK = 32
SEL_BLOCK = 64
SEL_TOPK = 16
WINDOW = 512
N_BRANCH = 3
Q_BLOCK = 128
RWKV_HEADS = 16
RWKV_WIDTH = RWKV_HEADS * HEAD_DIM
DECAY_LORA = 64
ICLR_LORA = 64
GATE_LORA = 160
SHIFT_WIDTH = 3 * RWKV_WIDTH + DECAY_LORA + ICLR_LORA + GATE_LORA
N_EXPERTS = 32
MOE_TOPK = 4
D_FF = 2048
SWIGLU_LIMIT = 7.0
SWIGLU_ALPHA = 1.702
MOE_BLOCK = 128

NORM_EPS = 1e-5
GN_EPS = 64e-5
W_IN_COLS = NSA_WIDTH + 6 * KV_WIDTH + N_BRANCH * NSA_HEADS + SHIFT_WIDTH
IN_SPLITS = (NSA_WIDTH, NSA_WIDTH + 6 * KV_WIDTH, NSA_WIDTH + 6 * KV_WIDTH + N_BRANCH * NSA_HEADS)
RWKV_SPLITS = (RWKV_WIDTH, 2 * RWKV_WIDTH, 3 * RWKV_WIDTH, 3 * RWKV_WIDTH + DECAY_LORA,
               3 * RWKV_WIDTH + DECAY_LORA + ICLR_LORA)

kernel_name = "hymba_nsa_rwkv7_moe_step"


def rmsnorm(x, g):
    xf = x.astype(jnp.float32)
    y = xf * lax.rsqrt(jnp.mean(xf * xf, axis=-1, keepdims=True) + NORM_EPS)
    return (y * g.astype(jnp.float32)).astype(x.dtype)


def alibi_slopes():
    return 2.0 ** (-8.0 * jnp.arange(1, NSA_HEADS + 1, dtype=jnp.float32) / NSA_HEADS)


def masked_softmax(s, mask, axes):
    s = jnp.where(mask, s, -jnp.inf)
    m = jnp.max(s, axis=axes, keepdims=True)
    m = jnp.where(jnp.isfinite(m), m, 0.0)
    e = jnp.exp(s - m)
    return e / jnp.maximum(jnp.sum(e, axis=axes, keepdims=True), 1e-30)


def nsa_attend(q, gate_logits, kv_full, kv_win, q_pos0, w_pos0, w_cmp_k, w_cmp_v, out_gain):
    n, tq = q.shape[0], q.shape[1]
    length = kv_full.shape[1]
    l_pad = -(-length // SEL_BLOCK) * SEL_BLOCK
    kv_full = jnp.pad(kv_full, ((0, 0), (0, l_pad - length), (0, 0), (0, 0), (0, 0)))
    n_cmp = l_pad // CMP_BLOCK
    n_sel = l_pad // SEL_BLOCK
    cmp_rows = kv_full[:, :, 0:2].reshape(n, n_cmp, CMP_BLOCK, 2, NSA_KV_HEADS, HEAD_DIM)
    k_cmp = jnp.einsum('ncbhd,bde->nche', cmp_rows[:, :, :, 0], w_cmp_k)
    v_cmp = jnp.einsum('ncbhd,bde->nche', cmp_rows[:, :, :, 1], w_cmp_v)
    cmp_end = (jnp.arange(n_cmp, dtype=jnp.int32) + 1) * CMP_BLOCK - 1
    sel_rows = kv_full[:, :, 2:4].reshape(n, n_sel, SEL_BLOCK, 2, NSA_KV_HEADS, HEAD_DIM)
    k_sel = sel_rows[:, :, :, 0].transpose(0, 3, 1, 2, 4)
    v_sel = sel_rows[:, :, :, 1].transpose(0, 3, 1, 2, 4)
    top_n = min(SEL_TOPK, n_sel)
    win_pad = jnp.pad(kv_win, ((0, 0), (WINDOW, 0), (0, 0), (0, 0), (0, 0)))
    slopes = alibi_slopes().reshape(NSA_KV_HEADS, NSA_GROUP)
    scale = HEAD_DIM ** -0.5
    qb = math.gcd(tq, Q_BLOCK)
    n_qb = tq // qb
    q_blocks = q.reshape(n, n_qb, qb, NSA_KV_HEADS, NSA_GROUP, HEAD_DIM).swapaxes(0, 1)
    q_starts = q_pos0 + jnp.arange(n_qb, dtype=jnp.int32) * qb
    b_idx = jnp.arange(n)[:, None, None, None]
    h_idx = jnp.arange(NSA_KV_HEADS)[None, None, :, None]
    sel_ids = jnp.arange(n_sel, dtype=jnp.int32)

    def attend_block(args):
        qi, p0 = args
        t = p0 + jnp.arange(qb, dtype=jnp.int32)
        s = jnp.einsum('nqhgd,nchd->nqhgc', qi, k_cmp, preferred_element_type=jnp.float32) * scale
        dist_c = (t[:, None] - cmp_end[None, :]).astype(jnp.float32)
        s = s - slopes[None, None, :, :, None] * dist_c[None, :, None, None, :]
        p_cmp = masked_softmax(s, (dist_c >= 0)[None, :, None, None, :], (-1,))
        o_cmp = jnp.einsum('nqhgc,nchd->nqhgd', p_cmp.astype(v_cmp.dtype), v_cmp)
        imp = p_cmp.sum(axis=3).reshape(n, qb, NSA_KV_HEADS, n_sel, SEL_BLOCK // CMP_BLOCK).sum(-1)
        cur = t // SEL_BLOCK
        forced = (sel_ids[None, :] == cur[:, None]) | (sel_ids[None, :] == 0)
        past = sel_ids[None, :] < cur[:, None]
        imp = jnp.where(forced[None, :, None, :], jnp.inf,
                        jnp.where(past[None, :, None, :], imp, -jnp.inf))
        _, idx = lax.top_k(imp, top_n)
        k_g = k_sel[b_idx, h_idx, idx]
        v_g = v_sel[b_idx, h_idx, idx]
        pos = idx[..., None] * SEL_BLOCK + jnp.arange(SEL_BLOCK, dtype=jnp.int32)
        dist_s = (t[None, :, None, None, None] - pos).astype(jnp.float32)
        s = jnp.einsum('nqhgd,nqhksd->nqhgks', qi, k_g, preferred_element_type=jnp.float32) * scale
        s = s - slopes[None, None, :, :, None, None] * dist_s[:, :, :, None]
        p = masked_softmax(s, (dist_s >= 0)[:, :, :, None], (-2, -1))
        o_sel = jnp.einsum('nqhgks,nqhksd->nqhgd', p.astype(v_g.dtype), v_g)
        win = lax.dynamic_slice_in_dim(win_pad, p0 - w_pos0, qb + WINDOW, axis=1)
        kpos = p0 - WINDOW + jnp.arange(qb + WINDOW, dtype=jnp.int32)
        dist_w = t[:, None] - kpos[None, :]
        ok = (dist_w >= 0) & (dist_w <= WINDOW) & (kpos >= w_pos0)[None, :]
        s = jnp.einsum('nqhgd,nshd->nqhgs', qi, win[:, :, 0], preferred_element_type=jnp.float32) * scale
        s = s - slopes[None, None, :, :, None] * dist_w.astype(jnp.float32)[None, :, None, None, :]
        p = masked_softmax(s, ok[None, :, None, None, :], (-1,))
        o_win = jnp.einsum('nqhgs,nshd->nqhgd', p.astype(win.dtype), win[:, :, 1])
        return jnp.stack([o_cmp, o_sel, o_win], axis=4)

    o = lax.map(attend_block, (q_blocks, q_starts))
    o = o.swapaxes(0, 1).reshape(n, tq, NSA_HEADS, N_BRANCH, HEAD_DIM)
    gates = jax.nn.sigmoid(gate_logits.astype(jnp.float32))
    o = jnp.einsum('nthb,nthbd->nthd', gates, o.astype(jnp.float32))
    o = o * lax.rsqrt(jnp.mean(o * o, axis=-1, keepdims=True) + NORM_EPS)
    o = o * out_gain.astype(jnp.float32).reshape(NSA_HEADS, HEAD_DIM)
    return o.reshape(n, tq, NSA_WIDTH).astype(q.dtype)


def rwkv_scan(s0, r, w, k, v, kk, a):
    xs = tuple(jnp.moveaxis(u, 1, 0) for u in (r, w, k, v, kk, a))

    def step(s, inp):
        r_t, w_t, k_t, v_t, kk_t, a_t = inp
        sa = jnp.einsum('nhvk,nhk->nhv', s, kk_t)
        s = (s * w_t[:, :, None, :] - sa[..., None] * (kk_t * a_t)[:, :, None, :]
             + v_t[..., None] * k_t[:, :, None, :])
        return s, jnp.einsum('nhvk,nhk->nhv', s, r_t)

    s_fin, ys = lax.scan(step, s0, xs)
    return jnp.moveaxis(ys, 0, 1), s_fin


def rwkv_mix(z, shift_prev, s0, mu, w0, w2, a0, a2, g2, k_k, k_a, r_k, ln_w, ln_b):
    n, t_len, _ = z.shape
    z_prev = jnp.concatenate([shift_prev[:, None, :].astype(z.dtype), z[:, :-1]], axis=1)
    zm = (z + (z_prev - z) * mu).astype(jnp.float32)
    r, k, v, wd, ad, gd = jnp.split(zm, RWKV_SPLITS, axis=-1)
    w_log = -jax.nn.softplus(-(w0 + jnp.tanh(wd) @ w2)) - 0.5
    decay = jnp.exp(-jnp.exp(w_log))
    a = jax.nn.sigmoid(a0 + ad @ a2)
    g = jax.nn.sigmoid(gd) @ g2

    def heads(u):
        return u.reshape(n, t_len, RWKV_HEADS, HEAD_DIM)

    kk = heads(k * k_k)
    kk = kk / jnp.maximum(jnp.sqrt(jnp.sum(kk * kk, axis=-1, keepdims=True)), 1e-12)
    k = k * (1.0 + (a - 1.0) * k_a)
    r_h, k_h, v_h, w_h, a_h = heads(r), heads(k), heads(v), heads(decay), heads(a)
    y, s_new = rwkv_scan(s0.astype(jnp.float32), r_h, w_h, k_h, v_h, kk, a_h)
    mean = jnp.mean(y, axis=-1, keepdims=True)
    var = jnp.mean(jnp.square(y - mean), axis=-1, keepdims=True)
    y = ((y - mean) * lax.rsqrt(var + GN_EPS)).reshape(n, t_len, RWKV_WIDTH) * ln_w + ln_b
    bonus = jnp.sum(r_h * k_h * r_k.reshape(RWKV_HEADS, HEAD_DIM), axis=-1, keepdims=True) * v_h
    y = (y + bonus.reshape(n, t_len, RWKV_WIDTH)) * g
    return y, s_new, z[:, -1]


def moe_ffn(x, router_w, router_b, w1, b1, w2, b2):
    n, t_len, dm = x.shape
    n_tok = n * t_len
    xt = x.reshape(n_tok, dm)
    logits = jnp.dot(xt, router_w, preferred_element_type=jnp.float32) + router_b.astype(jnp.float32)
    top_v, top_e = lax.top_k(logits, MOE_TOPK)
    gate = jax.nn.softmax(top_v, axis=-1)
    n_assign = n_tok * MOE_TOPK
    flat_e = top_e.reshape(-1)
    flat_tok = jnp.repeat(jnp.arange(n_tok, dtype=jnp.int32), MOE_TOPK)
    flat_gate = gate.reshape(-1)
    order = jnp.argsort(flat_e)
    e_sorted = flat_e[order]
    counts = jnp.zeros((N_EXPERTS,), jnp.int32).at[flat_e].add(1)
    padded = (counts + MOE_BLOCK - 1) // MOE_BLOCK * MOE_BLOCK
    pad_end = jnp.cumsum(padded)
    pad_start = pad_end - padded
    start = jnp.cumsum(counts) - counts
    dest = pad_start[e_sorted] + jnp.arange(n_assign, dtype=jnp.int32) - start[e_sorted]
    n_blocks = -(-n_assign // MOE_BLOCK) + N_EXPERTS
    n_slots = n_blocks * MOE_BLOCK
    slot_tok = jnp.full((n_slots,), n_tok, jnp.int32).at[dest].set(flat_tok[order])
    slot_gate = jnp.zeros((n_slots,), jnp.float32).at[dest].set(flat_gate[order])
    blk_start = jnp.arange(n_blocks, dtype=jnp.int32) * MOE_BLOCK
    blk_e = jnp.minimum(jnp.searchsorted(pad_end, blk_start, side='right'), N_EXPERTS - 1)
    x_pad = jnp.concatenate([xt, jnp.zeros((1, dm), xt.dtype)], axis=0)
    xb = x_pad[slot_tok].reshape(n_blocks, MOE_BLOCK, dm)

    def expert_block(args):
        xi, e = args
        h = xi @ w1[e] + b1[e]
        glu, lin = jnp.split(h, 2, axis=-1)
        glu = jnp.minimum(glu, SWIGLU_LIMIT)
        lin = jnp.clip(lin, -SWIGLU_LIMIT, SWIGLU_LIMIT)
        act = glu * jax.nn.sigmoid(SWIGLU_ALPHA * glu) * (lin + 1.0)
        return act @ w2[e] + b2[e]

    yb = lax.map(expert_block, (xb, blk_e)).reshape(n_slots, dm)
    y = jax.ops.segment_sum(yb * slot_gate[:, None], slot_tok, num_segments=n_tok + 1)[:n_tok]
    return y.reshape(n, t_len, dm).astype(x.dtype)


def layer_forward(x, past_kv, past_win, s0, shift0, win_keep, lp):
    (norm_attn, w_in, w_cmp_k, w_cmp_v, nsa_gain, mu, w0, w2, a0, a2, g2, k_k, k_a, r_k,
     ln_w, ln_b, w_out, norm_ffn, router_w, router_b, moe_w1, moe_b1, moe_w2, moe_b2) = lp
    n, t_len, _ = x.shape
    p_len = past_kv.shape[1]
    w_len = past_win.shape[1]
    h = rmsnorm(x, norm_attn)
    proj = jnp.dot(h, w_in)
    q, kvw, gl, z = jnp.split(proj, IN_SPLITS, axis=-1)
    kvw = kvw.reshape(n, t_len, 6, NSA_KV_HEADS, HEAD_DIM)
    kv_new = kvw[:, :, :4]
    win_new = kvw[:, :, 4:]
    kv_full = jnp.concatenate([past_kv.astype(kv_new.dtype), kv_new], axis=1)
    kv_win = jnp.concatenate([past_win.astype(win_new.dtype), win_new], axis=1)
    o_nsa = nsa_attend(q.reshape(n, t_len, NSA_HEADS, HEAD_DIM),
                       gl.reshape(n, t_len, NSA_HEADS, N_BRANCH),
                       kv_full, kv_win, p_len, p_len - w_len, w_cmp_k, w_cmp_v, nsa_gain)
    o_rwkv, s_new, shift_new = rwkv_mix(z, shift0, s0, mu, w0, w2, a0, a2, g2, k_k, k_a, r_k, ln_w, ln_b)
    mixed = jnp.concatenate([o_nsa, o_rwkv.astype(o_nsa.dtype)], axis=-1)
    x = x + jnp.dot(mixed, w_out).astype(x.dtype)
    x = x + moe_ffn(rmsnorm(x, norm_ffn), router_w, router_b, moe_w1, moe_b1, moe_w2, moe_b2)
    win_out = kv_win[:, kv_win.shape[1] - win_keep:]
    return x, kv_new, win_out, s_new.astype(s0.dtype), shift_new.astype(shift0.dtype)


def setup_inputs(seed: int = 0) -> dict:
    key = jax.random.key(seed)
    ks = iter(jax.random.split(key, 40))

    def nrm(shape, scale):
        return scale * jax.random.normal(next(ks), shape, jnp.float32)

    def unif(shape, lo, hi):
        return jax.random.uniform(next(ks), shape, jnp.float32, lo, hi)

    n_pages = PAST_LEN // PAGE_SIZE
    n_used = DEC_BATCH * n_pages
    n_phys = n_used + n_used // 4
    win_buf = min(WINDOW, PAST_LEN)
    page_table = jax.random.permutation(next(ks), n_phys)[:n_used].reshape(DEC_BATCH, n_pages).astype(jnp.int32)
    L = DEPTH
    return {
        "x_prompt": nrm((BATCH, SEQ, D_MODEL), 1.0),
        "x_sample": nrm((DEC_BATCH, DEC_SEQ, D_MODEL), 1.0),
        "cache_nsa_kv": nrm((L, n_phys, PAGE_SIZE, 4, NSA_KV_HEADS, HEAD_DIM), 1.0),
        "state_win_kv": nrm((L, DEC_BATCH, win_buf, 2, NSA_KV_HEADS, HEAD_DIM), 1.0),
        "state_rwkv": nrm((L, DEC_BATCH, RWKV_HEADS, HEAD_DIM, HEAD_DIM), 0.3),
        "state_shift": nrm((L, DEC_BATCH, SHIFT_WIDTH), 1.0),
        "page_table": page_table,
        "norm_attn": 1.0 + nrm((L, D_MODEL), 0.05),
        "w_in": nrm((L, D_MODEL, W_IN_COLS), D_MODEL ** -0.5),
        "nsa_w_cmp_k": nrm((L, CMP_BLOCK, HEAD_DIM, HEAD_DIM), (CMP_BLOCK * HEAD_DIM) ** -0.5),
        "nsa_w_cmp_v": nrm((L, CMP_BLOCK, HEAD_DIM, HEAD_DIM), (CMP_BLOCK * HEAD_DIM) ** -0.5),
        "nsa_out_gain": 1.0 + nrm((L, NSA_WIDTH), 0.05),
        "rwkv_mu": unif((L, SHIFT_WIDTH), 0.0, 1.0),
        "rwkv_w0": unif((L, RWKV_WIDTH), -4.0, 1.0),
        "rwkv_w2": nrm((L, DECAY_LORA, RWKV_WIDTH), DECAY_LORA ** -0.5),
        "rwkv_a0": nrm((L, RWKV_WIDTH), 0.3),
        "rwkv_a2": nrm((L, ICLR_LORA, RWKV_WIDTH), ICLR_LORA ** -0.5),
        "rwkv_g2": nrm((L, GATE_LORA, RWKV_WIDTH), GATE_LORA ** -0.5),
        "rwkv_k_k": 0.85 + nrm((L, RWKV_WIDTH), 0.05),
        "rwkv_k_a": 1.0 + nrm((L, RWKV_WIDTH), 0.05),
        "rwkv_r_k": nrm((L, RWKV_WIDTH), 0.1),
        "rwkv_ln_w": 1.0 + nrm((L, RWKV_WIDTH), 0.05),
        "rwkv_ln_b": nrm((L, RWKV_WIDTH), 0.01),
        "w_out": nrm((L, D_MODEL, D_MODEL), D_MODEL ** -0.5),
        "norm_ffn": 1.0 + nrm((L, D_MODEL), 0.05),
        "router_w": nrm((L, D_MODEL, N_EXPERTS), D_MODEL ** -0.5),
        "router_b": nrm((L, N_EXPERTS), 0.01),
        "moe_w1": nrm((L, N_EXPERTS, D_MODEL, 2 * D_FF), D_MODEL ** -0.5),
        "moe_b1": nrm((L, N_EXPERTS, 2 * D_FF), 0.01),
        "moe_w2": nrm((L, N_EXPERTS, D_FF, D_MODEL), D_FF ** -0.5),
        "moe_b2": nrm((L, N_EXPERTS, D_MODEL), 0.01),
        "norm_final": 1.0 + nrm((D_MODEL,), 0.05),
    }


def reference(x_prompt, x_sample, cache_nsa_kv, state_win_kv, state_rwkv, state_shift, page_table,
              norm_attn, w_in, nsa_w_cmp_k, nsa_w_cmp_v, nsa_out_gain, rwkv_mu, rwkv_w0, rwkv_w2,
              rwkv_a0, rwkv_a2, rwkv_g2, rwkv_k_k, rwkv_k_a, rwkv_r_k, rwkv_ln_w, rwkv_ln_b, w_out,
              norm_ffn, router_w, router_b, moe_w1, moe_b1, moe_w2, moe_b2, norm_final):
    nb_p, seq_p = x_prompt.shape[0], x_prompt.shape[1]
    nb_s = x_sample.shape[0]
    past_len = page_table.shape[1] * PAGE_SIZE
    win_buf = state_win_kv.shape[2]
    xp, xs = x_prompt, x_sample
    kv_p, kv_s, win_p, win_s, rw_p, rw_s, sh_p, sh_s = [], [], [], [], [], [], [], []
    for l in range(DEPTH):
        lp = (norm_attn[l], w_in[l], nsa_w_cmp_k[l], nsa_w_cmp_v[l], nsa_out_gain[l], rwkv_mu[l],
              rwkv_w0[l], rwkv_w2[l], rwkv_a0[l], rwkv_a2[l], rwkv_g2[l], rwkv_k_k[l], rwkv_k_a[l],
              rwkv_r_k[l], rwkv_ln_w[l], rwkv_ln_b[l], w_out[l], norm_ffn[l], router_w[l], router_b[l],
              moe_w1[l], moe_b1[l], moe_w2[l], moe_b2[l])
        xp, a1, a2, a3, a4 = layer_forward(
            xp,
            jnp.zeros((nb_p, 0, 4, NSA_KV_HEADS, HEAD_DIM), xp.dtype),
            jnp.zeros((nb_p, 0, 2, NSA_KV_HEADS, HEAD_DIM), xp.dtype),
            jnp.zeros((nb_p, RWKV_HEADS, HEAD_DIM, HEAD_DIM), xp.dtype),
            jnp.zeros((nb_p, SHIFT_WIDTH), xp.dtype),
            min(WINDOW, seq_p), lp)
        past_kv = cache_nsa_kv[l, page_table].reshape(nb_s, past_len, 4, NSA_KV_HEADS, HEAD_DIM)
        xs, b1, b2, b3, b4 = layer_forward(
            xs, past_kv, state_win_kv[l], state_rwkv[l], state_shift[l], win_buf, lp)
        kv_p.append(a1); win_p.append(a2); rw_p.append(a3); sh_p.append(a4)
        kv_s.append(b1); win_s.append(b2); rw_s.append(b3); sh_s.append(b4)
    y_prompt = rmsnorm(xp, norm_final)
    y_sample = rmsnorm(xs, norm_final)
    return (y_prompt, y_sample, jnp.stack(kv_p), jnp.stack(kv_s), jnp.stack(win_p), jnp.stack(win_s),
            jnp.stack(rw_p), jnp.stack(rw_s), jnp.stack(sh_p), jnp.stack(sh_s))
```

```python
import functools

import jax
import jax.numpy as jnp
from jax import lax
from jax.experimental import pallas as pl
from jax.experimental.pallas import tpu as pltpu

F32 = jnp.float32
BF16 = jnp.bfloat16

D_MODEL = 2048
HEAD_DIM = 64
NSA_HEADS = 16
NSA_KV_HEADS = 4
NSA_GROUP = 4
NSA_WIDTH = 1024
KV_WIDTH = 256
CMP_BLOCK = 32
SEL_BLOCK = 64
SEL_TOPK = 16
WINDOW = 512
Q_BLOCK = 128
PAGE_SIZE = 128
RWKV_HEADS = 16
RWKV_WIDTH = 1024
DECAY_LORA = 64
ICLR_LORA = 64
GATE_LORA = 160
SHIFT_WIDTH = 3 * RWKV_WIDTH + DECAY_LORA + ICLR_LORA + GATE_LORA
Z_PAD = 3456
N_EXPERTS = 32
MOE_TOPK = 4
D_FF = 2048
SWIGLU_LIMIT = 7.0
SWIGLU_ALPHA = 1.702
NORM_EPS = 1e-5
GN_EPS = 64e-5
LANES = 128
VMEM_LIMIT = 56 * 1024 * 1024


def _cparams(*sem):
    return pltpu.CompilerParams(dimension_semantics=sem, vmem_limit_bytes=VMEM_LIMIT)


def _dot(a, b):
    return jnp.dot(a, b, preferred_element_type=F32)


def _dot_nt(a, b):
    return lax.dot_general(a, b, (((1,), (1,)), ((), ())), preferred_element_type=F32)


def _split3(x):
    hi = x.astype(BF16)
    r1 = x - hi.astype(F32)
    mid = r1.astype(BF16)
    lo = (r1 - mid.astype(F32)).astype(BF16)
    return hi, mid, lo


def _dot_exact01(x, m01):
    hi, mid, lo = _split3(x)
    return _dot(hi, m01) + _dot(mid, m01) + _dot(lo, m01)


def _rmsnorm_kernel(x_ref, g_ref, o_ref):
    x = x_ref[...]
    ms = jnp.mean(x * x, axis=-1, keepdims=True)
    o_ref[...] = (x * lax.rsqrt(ms + NORM_EPS) * g_ref[...]).astype(o_ref.dtype)


def _rmsnorm(x, g, out_dtype, tm=512):
    m, d = x.shape
    return pl.pallas_call(
        _rmsnorm_kernel,
        grid=(m // tm,),
        in_specs=[pl.BlockSpec((tm, d), lambda i: (i, 0)), pl.BlockSpec((1, d), lambda i: (0, 0))],
        out_specs=pl.BlockSpec((tm, d), lambda i: (i, 0)),
        out_shape=jax.ShapeDtypeStruct((m, d), out_dtype),
        compiler_params=_cparams("parallel"),
        name="rmsnorm",
    )(x, g.reshape(1, d))


def _mm_kernel(a_ref, b_ref, o_ref):
    o_ref[...] = _dot(a_ref[...], b_ref[...]).astype(o_ref.dtype)


def _matmul(a, b, out_dtype, tm, tn, name):
    m, k = a.shape
    n = b.shape[1]
    return pl.pallas_call(
        _mm_kernel,
        grid=(m // tm, n // tn),
        in_specs=[pl.BlockSpec((tm, k), lambda i, j: (i, 0)), pl.BlockSpec((k, tn), lambda i, j: (0, j))],
        out_specs=pl.BlockSpec((tm, tn), lambda i, j: (i, j)),
        out_shape=jax.ShapeDtypeStruct((m, n), out_dtype),
        compiler_params=_cparams("parallel", "arbitrary"),
        name=name,
    )(a, b)


def _compress_kernel(a_ref, wk_ref, wv_ref, o_ref, acc_ref):
    b = pl.program_id(1)

    @pl.when(b == 0)
    def _():
        acc_ref[...] = jnp.zeros_like(acc_ref)

    a = a_ref[...].astype(BF16)
    acc_ref[:, 0:KV_WIDTH] += _dot(a[:, 0:KV_WIDTH], wk_ref[0])
    acc_ref[:, KV_WIDTH:] += _dot(a[:, KV_WIDTH:], wv_ref[0])

    @pl.when(b == CMP_BLOCK - 1)
    def _():
        o_ref[...] = acc_ref[...]


def _compress(rows, wk_bd, wv_bd, nb):
    n_blocks = rows.shape[0]
    return pl.pallas_call(
        _compress_kernel,
        grid=(n_blocks // nb, CMP_BLOCK),
        in_specs=[
            pl.BlockSpec((nb, 2 * KV_WIDTH), lambda i, b: (i, 2 * b)),
            pl.BlockSpec((1, KV_WIDTH, KV_WIDTH), lambda i, b: (b, 0, 0)),
            pl.BlockSpec((1, KV_WIDTH, KV_WIDTH), lambda i, b: (b, 0, 0)),
        ],
        out_specs=pl.BlockSpec((nb, 2 * KV_WIDTH), lambda i, b: (i, 0)),
        out_shape=jax.ShapeDtypeStruct((n_blocks, 2 * KV_WIDTH), F32),
        scratch_shapes=[pltpu.VMEM((nb, 2 * KV_WIDTH), F32)],
        compiler_params=_cparams("parallel", "arbitrary"),
        name="nsa_compress",
    )(rows, wk_bd, wv_bd)


def _lane_masks(ne):
    lane = lax.broadcasted_iota(jnp.int32, (1, ne * HEAD_DIM), 1)
    return [(lane >= e * HEAD_DIM) & (lane < (e + 1) * HEAD_DIM) for e in range(ne)]


def _build_qbd(slabs, masks):
    parts = [jnp.where(m, s, 0.0) for s in slabs for m in masks]
    return jnp.concatenate(parts, axis=0).astype(BF16)


def _masked_softmax(s, mask):
    s = jnp.where(mask, s, -jnp.inf)
    m = jnp.max(s, axis=-1, keepdims=True)
    m = jnp.where(jnp.isfinite(m), m, 0.0)
    e = jnp.exp(s - m)
    return e / jnp.maximum(jnp.sum(e, axis=-1, keepdims=True), 1e-30)


def _cmp_branch(qbd, kc, vc, slope, tpos, n_cmp):
    s = _dot_nt(qbd, kc.astype(BF16))
    cmp_end = (lax.broadcasted_iota(jnp.int32, (1, n_cmp), 1) + 1) * CMP_BLOCK - 1
    dist = (tpos - cmp_end).astype(F32)
    p = _masked_softmax(s - slope * dist, dist >= 0)
    return _dot(p.astype(BF16), vc.astype(BF16)), p


def _select_blocks(p_cmp, pair01, tpos_et, n_rank, rows_et):
    psum = p_cmp[0:rows_et]
    for g in range(1, NSA_GROUP):
        psum = psum + p_cmp[g * rows_et:(g + 1) * rows_et]
    imp = _dot_exact01(psum, pair01)
    j = lax.broadcasted_iota(jnp.int32, (1, LANES), 1)
    cur = jnp.right_shift(tpos_et, 6)
    forced = (j == cur) | (j == 0)
    key = jnp.where(forced, jnp.inf, jnp.where(j < cur, imp, -jnp.inf))
    rank = jnp.zeros(key.shape, F32)
    for i in range(n_rank):
        ci = key[:, i:i + 1]
        ahead = (ci > key) | ((ci == key) & (j > i))
        rank = rank + jnp.where(ahead, 1.0, 0.0)
    return jnp.where(rank < SEL_TOPK, 1.0, 0.0)


def _sel_scores(qbd, kt, sel_bf, k0, tk, tpos, slope):
    s = _dot_nt(qbd, kt)
    kpos = k0 + lax.broadcasted_iota(jnp.int32, (1, tk), 1)
    jrow = lax.broadcasted_iota(jnp.int32, (LANES, 1), 0)
    expand = jnp.where(jnp.right_shift(kpos, 6) == jrow, 1.0, 0.0).astype(BF16)
    chosen = _dot(sel_bf, expand)
    chosen = jnp.concatenate([chosen] * NSA_GROUP, axis=0)
    dist = (tpos - kpos).astype(F32)
    vis = (chosen > 0.5) & (dist >= 0)
    return jnp.where(vis, s - slope * dist, -jnp.inf)


def _online_update(s, vt, m_old, l_old, acc_old):
    m_new = jnp.maximum(m_old, jnp.max(s, axis=-1, keepdims=True))
    alpha = jnp.exp(m_old - m_new)
    p = jnp.exp(s - m_new)
    l_new = alpha * l_old + jnp.sum(p, axis=-1, keepdims=True)
    acc_new = alpha * acc_old + _dot(p.astype(BF16), vt)
    return m_new, l_new, acc_new


M_INIT = -1e30


def _win_branch(qbd, kw, vw, start, tpos, slope):
    wk = kw.shape[0]
    s = _dot_nt(qbd, kw.astype(BF16))
    kpos = start + lax.broadcasted_iota(jnp.int32, (1, wk), 1)
    dist = tpos - kpos
    ok = (dist >= 0) & (dist <= WINDOW)
    p = _masked_softmax(s - slope * dist.astype(F32), ok)
    return _dot(p.astype(BF16), vw.astype(BF16))


def _merge_heads(o, g, masks, tq):
    ne = len(masks)
    out = None
    for e in range(ne):
        r0 = (g * ne + e) * tq
        part = jnp.where(masks[e], o[r0:r0 + tq], 0.0)
        out = part if out is None else out + part
    return out


def _gate_norm(o_c, o_s, o_w, gates, gain, bo01, masks, tq, width):
    outs = []
    for g in range(NSA_GROUP):
        x = None
        for br, o in enumerate((o_c, o_s, o_w)):
            gt = gates[:, (br * NSA_GROUP + g) * width:(br * NSA_GROUP + g + 1) * width]
            term = gt * _merge_heads(o, g, masks, tq)
            x = term if x is None else x + term
        ms = _dot_exact01(x * x, bo01) * (1.0 / HEAD_DIM)
        outs.append(x * lax.rsqrt(ms + NORM_EPS) * gain[:, g * width:(g + 1) * width])
    return outs


def _nsa_prompt_kernel(q_ref, gl_ref, ks_ref, vs_ref, kw_ref, vw_ref, kc_ref, vc_ref, slope_ref, eg_ref,
                       gain_ref, pair_ref, bo_ref, o_ref, m_ref, l_ref, acc_ref, *, seq):
    ne, tq = 2, Q_BLOCK
    width = ne * HEAD_DIM
    rows_et = ne * tq
    n_cmp = seq // CMP_BLOCK
    tk = 2 * Q_BLOCK
    i = pl.program_id(1)
    p0 = i * tq
    masks = _lane_masks(ne)
    qf = q_ref[...]
    qbd = _build_qbd([qf[:, g * width:(g + 1) * width] for g in range(NSA_GROUP)], masks)
    r = lax.broadcasted_iota(jnp.int32, (NSA_GROUP * rows_et, 1), 0)
    tpos = p0 + (r & (tq - 1))
    slope = slope_ref[0]

    o_c, p_c = _cmp_branch(qbd, kc_ref[...], vc_ref[...], slope, tpos, n_cmp)
    sel = _select_blocks(p_c, pair_ref[...], tpos[0:rows_et], seq // SEL_BLOCK, rows_et).astype(BF16)

    m_ref[...] = jnp.full(m_ref.shape, M_INIT, F32)
    l_ref[...] = jnp.zeros(l_ref.shape, F32)
    acc_ref[...] = jnp.zeros(acc_ref.shape, F32)

    def body(t, carry):
        k0 = pl.multiple_of(t * tk, tk)
        kt = ks_ref[pl.ds(k0, tk), :].astype(BF16)
        vt = vs_ref[pl.ds(k0, tk), :].astype(BF16)
        s = _sel_scores(qbd, kt, sel, k0, tk, tpos, slope)
        m_new, l_new, acc_new = _online_update(s, vt, m_ref[...], l_ref[...], acc_ref[...])
        m_ref[...] = m_new
        l_ref[...] = l_new
        acc_ref[...] = acc_new
        return carry

    lax.fori_loop(0, lax.shift_right_logical(i + 2, 1), body, 0)
    o_s = acc_ref[...] / jnp.maximum(l_ref[...], 1e-30)

    start = pl.multiple_of(jnp.maximum(p0 - WINDOW, 0), Q_BLOCK)
    wk = WINDOW + tq
    o_w = _win_branch(qbd, kw_ref[pl.ds(start, wk), :], vw_ref[pl.ds(start, wk), :], start, tpos, slope)

    gates = _dot_exact01(jax.nn.sigmoid(gl_ref[...]), eg_ref[0])
    outs = _gate_norm(o_c, o_s, o_w, gates, gain_ref[0], bo_ref[...], masks, tq, width)
    for g in range(NSA_GROUP):
        o_ref[:, g * width:(g + 1) * width] = outs[g].astype(o_ref.dtype)


def _nsa_prompt(q, gl, kvn, winn, kvc, slopes, eg, gain, pair01, bo01, n, seq):
    nq = seq // Q_BLOCK
    n_cmp = seq // CMP_BLOCK
    rr = NSA_GROUP * 2 * Q_BLOCK
    kern = functools.partial(_nsa_prompt_kernel, seq=seq)
    return pl.pallas_call(
        kern,
        grid=(n, nq, 2),
        in_specs=[
            pl.BlockSpec((Q_BLOCK, 512), lambda b, i, p: (b * nq + i, p)),
            pl.BlockSpec((Q_BLOCK, LANES), lambda b, i, p: (b * nq + i, 0)),
            pl.BlockSpec((seq, LANES), lambda b, i, p: (b, 4 + p)),
            pl.BlockSpec((seq, LANES), lambda b, i, p: (b, 6 + p)),
            pl.BlockSpec((seq, LANES), lambda b, i, p: (b, p)),
            pl.BlockSpec((seq, LANES), lambda b, i, p: (b, 2 + p)),
            pl.BlockSpec((n_cmp, LANES), lambda b, i, p: (b, p)),
            pl.BlockSpec((n_cmp, LANES), lambda b, i, p: (b, 2 + p)),
            pl.BlockSpec((1, rr, 1), lambda b, i, p: (p, 0, 0)),
            pl.BlockSpec((1, LANES, 12 * LANES), lambda b, i, p: (p, 0, 0)),
            pl.BlockSpec((1, 1, 512), lambda b, i, p: (p, 0, 0)),
            pl.BlockSpec((n_cmp, LANES), lambda b, i, p: (0, 0)),
            pl.BlockSpec((LANES, LANES), lambda b, i, p: (0, 0)),
        ],
        out_specs=pl.BlockSpec((Q_BLOCK, 512), lambda b, i, p: (b * nq + i, p)),
        out_shape=jax.ShapeDtypeStruct((n * seq, NSA_WIDTH), BF16),
        scratch_shapes=[pltpu.VMEM((rr, 1), F32), pltpu.VMEM((rr, 1), F32), pltpu.VMEM((rr, LANES), F32)],
        compiler_params=_cparams("parallel", "parallel", "parallel"),
        name="nsa_prompt",
    )(q, gl, kvn, kvn, winn, winn, kvc, kvc, slopes, eg, gain, pair01, bo01)


def _pad_rows(x, rows):
    return jnp.concatenate([x, jnp.zeros((rows - x.shape[0], x.shape[1]), x.dtype)], axis=0)


def _nsa_sample_kernel(pt_ref, q_ref, gl_ref, kvn_ref, winn_ref, wst_ref, *rest, past_len):
    del pt_ref
    n_pages = past_len // PAGE_SIZE
    page_refs = rest[:n_pages]
    kc_refs = rest[n_pages:2 * n_pages]
    slope_ref, eg_ref, gain_ref, pair_ref, bo_ref, o_ref, nw_ref, kc_s, vc_s = rest[2 * n_pages:]
    ne, tq = NSA_KV_HEADS, 8
    width = ne * HEAD_DIM
    rows_et = ne * tq
    n_cmp = past_len // CMP_BLOCK
    blocks_per_page = PAGE_SIZE // CMP_BLOCK
    masks = _lane_masks(ne)

    def pair_slabs(x, g, stride):
        return jnp.concatenate([x[:, g * LANES:(g + 1) * LANES],
                                x[:, stride + g * LANES:stride + (g + 1) * LANES]], axis=1)

    qf = q_ref[...]
    qbd = _build_qbd([pair_slabs(qf, g, 4 * LANES) for g in range(NSA_GROUP)], masks)
    r = lax.broadcasted_iota(jnp.int32, (NSA_GROUP * rows_et, 1), 0)
    tpos = past_len + (r & (tq - 1))
    slope = slope_ref[...]

    for j in range(n_pages):
        for b in range(blocks_per_page):
            c = j * blocks_per_page + b
            kc_s[c:c + 1, :] = kc_refs[j][0, :, b * 2 * KV_WIDTH:b * 2 * KV_WIDTH + KV_WIDTH]
            vc_s[c:c + 1, :] = kc_refs[j][0, :, b * 2 * KV_WIDTH + KV_WIDTH:(b + 1) * 2 * KV_WIDTH]
    o_c, p_c = _cmp_branch(qbd, kc_s[...], vc_s[...], slope, tpos, n_cmp)
    n_sel_past = past_len // SEL_BLOCK
    sel = _select_blocks(p_c, pair_ref[...], tpos[0:rows_et], n_sel_past + 1, rows_et).astype(BF16)

    m = jnp.full((NSA_GROUP * rows_et, 1), M_INIT, F32)
    l = jnp.zeros((NSA_GROUP * rows_et, 1), F32)
    acc = jnp.zeros((NSA_GROUP * rows_et, width), F32)
    for j in range(n_pages):
        kt = page_refs[j][0, :, 0:KV_WIDTH].astype(BF16)
        vt = page_refs[j][0, :, KV_WIDTH:2 * KV_WIDTH].astype(BF16)
        s = _sel_scores(qbd, kt, sel, j * PAGE_SIZE, PAGE_SIZE, tpos, slope)
        m, l, acc = _online_update(s, vt, m, l, acc)
    kt = _pad_rows(kvn_ref[:, 2 * KV_WIDTH:3 * KV_WIDTH], PAGE_SIZE).astype(BF16)
    vt = _pad_rows(kvn_ref[:, 3 * KV_WIDTH:4 * KV_WIDTH], PAGE_SIZE).astype(BF16)
    s = _sel_scores(qbd, kt, sel, past_len, PAGE_SIZE, tpos, slope)
    m, l, acc = _online_update(s, vt, m, l, acc)
    o_s = acc / jnp.maximum(l, 1e-30)

    kw = jnp.concatenate([wst_ref[0, :, 0:KV_WIDTH], _pad_rows(winn_ref[:, 0:KV_WIDTH], PAGE_SIZE)], axis=0)
    vw = jnp.concatenate([wst_ref[0, :, KV_WIDTH:], _pad_rows(winn_ref[:, KV_WIDTH:], PAGE_SIZE)], axis=0)
    o_w = _win_branch(qbd, kw, vw, past_len - WINDOW, tpos, slope)

    gx = _dot_exact01(jax.nn.sigmoid(gl_ref[...]), eg_ref[...])
    gates = jnp.concatenate([pair_slabs(gx, br * NSA_GROUP + g, 12 * LANES)
                             for br in range(3) for g in range(NSA_GROUP)], axis=1)
    gain = jnp.concatenate([pair_slabs(gain_ref[...], g, 4 * LANES) for g in range(NSA_GROUP)], axis=1)
    outs = _gate_norm(o_c, o_s, o_w, gates, gain, bo_ref[...], masks, tq, width)
    for g in range(NSA_GROUP):
        o_ref[:, g * LANES:(g + 1) * LANES] = outs[g][:, 0:LANES]
        o_ref[:, (NSA_GROUP + g) * LANES:(NSA_GROUP + g + 1) * LANES] = outs[g][:, LANES:]

    w_len = wst_ref.shape[1]
    nw_ref[0, 0:w_len - tq, :] = wst_ref[0, tq:w_len, :]
    nw_ref[0, w_len - tq:w_len, :] = winn_ref[...]


def _nsa_sample(page_table, q, gl, kvn, winn, win_state, cache_pages, kvc_pages, slopes, eg, gain, pair01, bo01,
                row0):
    n_seq, n_pages = page_table.shape
    past_len = n_pages * PAGE_SIZE
    w_len = win_state.shape[1]
    blk0 = row0 // 8
    rr = NSA_GROUP * NSA_KV_HEADS * 8
    tok = lambda s, pt: (blk0 + s, 0)
    const2 = lambda s, pt: (0, 0)
    in_specs = [
        pl.BlockSpec((8, NSA_WIDTH), tok),
        pl.BlockSpec((8, LANES), tok),
        pl.BlockSpec((8, 4 * KV_WIDTH), tok),
        pl.BlockSpec((8, 2 * KV_WIDTH), tok),
        pl.BlockSpec((1, w_len, 2 * KV_WIDTH), lambda s, pt: (s, 0, 0)),
    ]
    in_specs += [pl.BlockSpec((1, PAGE_SIZE, 2 * KV_WIDTH), functools.partial(lambda s, pt, j: (pt[s, j], 0, 1), j=j))
                 for j in range(n_pages)]
    in_specs += [pl.BlockSpec((1, 1, 8 * KV_WIDTH), functools.partial(lambda s, pt, j: (pt[s, j], 0, 0), j=j))
                 for j in range(n_pages)]
    in_specs += [
        pl.BlockSpec((rr, 1), const2),
        pl.BlockSpec((LANES, 24 * LANES), const2),
        pl.BlockSpec((1, NSA_WIDTH), const2),
        pl.BlockSpec((past_len // CMP_BLOCK, LANES), const2),
        pl.BlockSpec((2 * LANES, 2 * LANES), const2),
    ]
    grid_spec = pltpu.PrefetchScalarGridSpec(
        num_scalar_prefetch=1,
        grid=(n_seq,),
        in_specs=in_specs,
        out_specs=[pl.BlockSpec((8, NSA_WIDTH), lambda s, pt: (s, 0)),
                   pl.BlockSpec((1, w_len, 2 * KV_WIDTH), lambda s, pt: (s, 0, 0))],
        scratch_shapes=[pltpu.VMEM((past_len // CMP_BLOCK, KV_WIDTH), F32),
                        pltpu.VMEM((past_len // CMP_BLOCK, KV_WIDTH), F32)],
    )
    return pl.pallas_call(
        functools.partial(_nsa_sample_kernel, past_len=past_len),
        grid_spec=grid_spec,
        out_shape=[jax.ShapeDtypeStruct((n_seq * 8, NSA_WIDTH), F32),
                   jax.ShapeDtypeStruct((n_seq, w_len, 2 * KV_WIDTH), F32)],
        compiler_params=_cparams("parallel"),
        name="nsa_sample",
    )(page_table, q, gl, kvn, winn, win_state, *([cache_pages] * n_pages), *([kvc_pages] * n_pages),
      slopes, eg, gain, pair01, bo01)


def _head_sums(x, bo01):
    return jnp.concatenate([_dot_exact01(x[:, s * LANES:(s + 1) * LANES], bo01)
                            for s in range(x.shape[1] // LANES)], axis=1)


def _rwkv_prep_kernel(z_ref, first_ref, mu_ref, w0_ref, a0_ref, kk_ref, ka_ref, rk_ref, w2_ref, a2_ref, g2_ref,
                      bo_ref, r_o, w_o, k_o, v_o, kk_o, nb_o, g_o, bonus_o, *, seq_rows):
    z = z_ref[...]
    tm = z.shape[0]
    row = lax.broadcasted_iota(jnp.int32, (tm, 1), 0)
    if seq_rows >= tm:
        prev = jnp.where(row == 0, first_ref[0, 0:1, :], pltpu.roll(z, 1, 0))
    else:
        prev = jnp.where((row & (seq_rows - 1)) == 0, first_ref[...], pltpu.roll(z, 1, 0))
    zm = z + (prev - z) * mu_ref[...]
    w = RWKV_WIDTH
    r, k, v = zm[:, 0:w], zm[:, w:2 * w], zm[:, 2 * w:3 * w]
    wa = zm[:, 3 * w:3 * w + LANES]
    gd = zm[:, 3 * w + LANES:]
    wl = w0_ref[...] + _dot(jnp.tanh(wa).astype(BF16), w2_ref[...])
    neg = -wl
    softplus = jnp.maximum(neg, 0.0) + jnp.log1p(jnp.exp(-jnp.abs(neg)))
    decay = jnp.exp(-jnp.exp(-softplus - 0.5))
    a = jax.nn.sigmoid(a0_ref[...] + _dot(wa.astype(BF16), a2_ref[...]))
    g = _dot(jax.nn.sigmoid(gd).astype(BF16), g2_ref[...])
    bo = bo_ref[...]
    kk = k * kk_ref[...]
    kk = kk / jnp.maximum(jnp.sqrt(_head_sums(kk * kk, bo)), 1e-12)
    k2 = k * (1.0 + (a - 1.0) * ka_ref[...])
    r_o[...] = r
    w_o[...] = decay
    k_o[...] = k2
    v_o[...] = v
    kk_o[...] = kk
    nb_o[...] = -(kk * a)
    g_o[...] = g
    bonus_o[...] = _head_sums(r * k2 * rk_ref[...], bo) * v


def _rwkv_prep(z, first, row0, rows, tm, seq_rows, mu, w0, a0, k_k, k_a, r_k, w2p, a2p, g2p, bo01):
    blk0 = row0 // tm
    vec = lambda a: a.reshape(1, -1)
    cvec = pl.BlockSpec((1, RWKV_WIDTH), lambda i: (0, 0))
    if seq_rows >= tm:
        first_spec = pl.BlockSpec((1, 8, Z_PAD), lambda i: (i, 0, 0))
    else:
        first_spec = pl.BlockSpec((tm, Z_PAD), lambda i: (i, 0))
    out = jax.ShapeDtypeStruct((rows, RWKV_WIDTH), F32)
    ospec = pl.BlockSpec((tm, RWKV_WIDTH), lambda i: (i, 0))
    return pl.pallas_call(
        functools.partial(_rwkv_prep_kernel, seq_rows=seq_rows),
        grid=(rows // tm,),
        in_specs=[pl.BlockSpec((tm, Z_PAD), lambda i: (blk0 + i, 0)), first_spec,
                  pl.BlockSpec((1, Z_PAD), lambda i: (0, 0)), cvec, cvec, cvec, cvec, cvec,
                  pl.BlockSpec((LANES, RWKV_WIDTH), lambda i: (0, 0)),
                  pl.BlockSpec((LANES, RWKV_WIDTH), lambda i: (0, 0)),
                  pl.BlockSpec((2 * LANES, RWKV_WIDTH), lambda i: (0, 0)),
                  pl.BlockSpec((LANES, LANES), lambda i: (0, 0))],
        out_specs=[ospec] * 8,
        out_shape=[out] * 8,
        compiler_params=_cparams("parallel"),
        name="rwkv_prep",
    )(z, first, vec(mu), vec(w0), vec(a0), vec(k_k), vec(k_a), vec(r_k), w2p, a2p, g2p, bo01)


def _rwkv_scan_kernel(*refs, nbatch, tc, has_init):
    ops = refs[:6]
    pos = 6
    s0_ref = refs[pos] if has_init else None
    pos += 1 if has_init else 0
    pat_ref, bo_ref, y_ref, s_out, st = refs[pos:pos + 5]
    c = pl.program_id(1)
    nq = RWKV_HEADS // 4
    tw = 4 * HEAD_DIM

    @pl.when(c == 0)
    def _():
        if has_init:
            st[...] = s0_ref[...]
        else:
            st[...] = jnp.zeros_like(st)

    pat = pat_ref[...]
    bo = bo_ref[...]

    def step(t, carry):
        tiles = [(b, q) for b in range(nbatch) for q in range(nq)]
        rowv = lambda k, b, q: ops[k][b, pl.ds(t, 1), q * tw:(q + 1) * tw]
        lhs = []
        for b, q in tiles:
            s = st[b, q]
            lhs.append((s * rowv(4, b, q)).astype(BF16))
            lhs.append((pat * rowv(3, b, q)).astype(BF16))
        red = _dot(jnp.concatenate(lhs, axis=0), bo)
        outs = []
        for n, (b, q) in enumerate(tiles):
            sa = red[n * 2 * HEAD_DIM:n * 2 * HEAD_DIM + HEAD_DIM]
            vb = red[n * 2 * HEAD_DIM + HEAD_DIM:(n + 1) * 2 * HEAD_DIM]
            s2 = st[b, q] * rowv(1, b, q) + sa * rowv(5, b, q) + vb * rowv(2, b, q)
            st[b, q] = s2
            outs.append((s2 * rowv(0, b, q)).astype(BF16))
        yb = _dot(jnp.concatenate(outs, axis=0), bo)
        for n, (b, q) in enumerate(tiles):
            yrow = jnp.sum(yb[n * HEAD_DIM:(n + 1) * HEAD_DIM] * pat, axis=0, keepdims=True)
            y_ref[b, pl.ds(t, 1), q * tw:(q + 1) * tw] = yrow
        return carry

    lax.fori_loop(0, tc, step, 0)

    @pl.when(c == pl.num_programs(1) - 1)
    def _():
        s_out[...] = st[...]


def _rwkv_scan(ops, s0, nseq, seq_rows, nbatch, tc, pat, bo01):
    nq = RWKV_HEADS // 4
    tok_spec = pl.BlockSpec((nbatch, tc, RWKV_WIDTH), lambda gi, c: (gi, c, 0))
    st_spec = pl.BlockSpec((nbatch, nq, HEAD_DIM, 4 * HEAD_DIM), lambda gi, c: (gi, 0, 0, 0))
    in_specs = [tok_spec] * 6
    args = list(ops)
    if s0 is not None:
        in_specs.append(st_spec)
        args.append(s0)
    in_specs += [pl.BlockSpec((HEAD_DIM, 4 * HEAD_DIM), lambda gi, c: (0, 0)),
                 pl.BlockSpec((4 * HEAD_DIM, 4 * HEAD_DIM), lambda gi, c: (0, 0))]
    args += [pat, bo01]
    return pl.pallas_call(
        functools.partial(_rwkv_scan_kernel, nbatch=nbatch, tc=tc, has_init=s0 is not None),
        grid=(nseq // nbatch, seq_rows // tc),
        in_specs=in_specs,
        out_specs=[tok_spec, st_spec],
        out_shape=[jax.ShapeDtypeStruct((nseq, seq_rows, RWKV_WIDTH), F32),
                   jax.ShapeDtypeStruct((nseq, nq, HEAD_DIM, 4 * HEAD_DIM), F32)],
        scratch_shapes=[pltpu.VMEM((nbatch, nq, HEAD_DIM, 4 * HEAD_DIM), F32)],
        compiler_params=_cparams("parallel", "arbitrary"),
        name="rwkv_scan",
    )(*args)


def _rwkv_post_kernel(y_ref, g_ref, bonus_ref, lnw_ref, lnb_ref, bo_ref, o_ref):
    y = y_ref[...]
    bo = bo_ref[...]
    d = y - _head_sums(y, bo) * (1.0 / HEAD_DIM)
    var = _head_sums(d * d, bo) * (1.0 / HEAD_DIM)
    o = (d * lax.rsqrt(var + GN_EPS) * lnw_ref[...] + lnb_ref[...] + bonus_ref[...]) * g_ref[...]
    o_ref[...] = o.astype(o_ref.dtype)


def _rwkv_post(y, g, bonus, ln_w, ln_b, bo01, tm):
    rows = y.shape[0]
    tok = pl.BlockSpec((tm, RWKV_WIDTH), lambda i: (i, 0))
    cvec = pl.BlockSpec((1, RWKV_WIDTH), lambda i: (0, 0))
    return pl.pallas_call(
        _rwkv_post_kernel,
        grid=(rows // tm,),
        in_specs=[tok, tok, tok, cvec, cvec, pl.BlockSpec((LANES, LANES), lambda i: (0, 0))],
        out_specs=tok,
        out_shape=jax.ShapeDtypeStruct((rows, RWKV_WIDTH), BF16),
        compiler_params=_cparams("parallel"),
        name="rwkv_post",
    )(y, g, bonus, ln_w.reshape(1, -1), ln_b.reshape(1, -1), bo01)


def _rwkv_consts(w2, a2, g2):
    w2p = jnp.concatenate([w2, jnp.zeros((LANES - DECAY_LORA, RWKV_WIDTH), w2.dtype)], axis=0)
    a2p = jnp.concatenate([jnp.zeros((DECAY_LORA, RWKV_WIDTH), a2.dtype), a2], axis=0)
    g2p = jnp.concatenate([g2, jnp.zeros((2 * LANES - GATE_LORA, RWKV_WIDTH), g2.dtype)], axis=0)
    v = jnp.arange(HEAD_DIM)
    pat = (v[:, None] == (jnp.arange(4 * HEAD_DIM)[None, :] % HEAD_DIM)).astype(F32)
    return w2p.astype(BF16), a2p.astype(BF16), g2p.astype(BF16), pat


def _state_to_tiles(s):
    n = s.shape[0]
    return s.reshape(n, 4, 4, HEAD_DIM, HEAD_DIM).transpose(0, 1, 3, 2, 4).reshape(n, 4, HEAD_DIM, 4 * HEAD_DIM)


def _tiles_to_state(t):
    n = t.shape[0]
    return t.reshape(n, 4, HEAD_DIM, 4, HEAD_DIM).transpose(0, 1, 3, 2, 4).reshape(n, RWKV_HEADS, HEAD_DIM, HEAD_DIM)


def _outproj_kernel(x_ref, on_ref, or_ref, wa_ref, wb_ref, g_ref, rwh_ref, rwl_ref, rb_ref,
                    x1_ref, h2_ref, te_ref, tg_ref):
    x1 = x_ref[...] + _dot(on_ref[...], wa_ref[...]) + _dot(or_ref[...], wb_ref[...])
    x1_ref[...] = x1
    ms = jnp.mean(x1 * x1, axis=-1, keepdims=True)
    hf = x1 * lax.rsqrt(ms + NORM_EPS) * g_ref[...]
    h2_ref[...] = hf
    hh = hf.astype(BF16)
    hl = (hf - hh.astype(F32)).astype(BF16)
    logits = _dot(hh, rwh_ref[...]) + _dot(hl, rwh_ref[...]) + _dot(hh, rwl_ref[...]) + rb_ref[...]
    lane = lax.broadcasted_iota(jnp.int32, logits.shape, 1).astype(F32)
    vals, idxs = [], []
    for _ in range(MOE_TOPK):
        m = jnp.max(logits, axis=-1, keepdims=True)
        idx = jnp.min(jnp.where(logits == m, lane, float(LANES)), axis=-1, keepdims=True)
        vals.append(m)
        idxs.append(idx)
        logits = jnp.where(lane == idx, -jnp.inf, logits)
    es = [jnp.exp(v - vals[0]) for v in vals]
    denom = es[0] + es[1] + es[2] + es[3]
    te = jnp.zeros(logits.shape, F32)
    tg = jnp.zeros(logits.shape, F32)
    for k in range(MOE_TOPK):
        te = jnp.where(lane == float(k), idxs[k], te)
        tg = jnp.where(lane == float(k), es[k] / denom, tg)
    te_ref[...] = te.astype(jnp.int32)
    tg_ref[...] = tg


def _outproj_router(x, o_nsa, o_rwkv, wa, wb, g, rw_hi, rw_lo, rb, tm=256):
    t = x.shape[0]
    tok = lambda w: pl.BlockSpec((tm, w), lambda i: (i, 0))
    full = lambda a: pl.BlockSpec(a.shape, lambda i: (0,) * a.ndim)
    return pl.pallas_call(
        _outproj_kernel,
        grid=(t // tm,),
        in_specs=[tok(D_MODEL), tok(NSA_WIDTH), tok(RWKV_WIDTH), full(wa), full(wb), full(g), full(rw_hi),
                  full(rw_lo), full(rb)],
        out_specs=[tok(D_MODEL), tok(D_MODEL), tok(LANES), tok(LANES)],
        out_shape=[jax.ShapeDtypeStruct((t, D_MODEL), F32), jax.ShapeDtypeStruct((t, D_MODEL), F32),
                   jax.ShapeDtypeStruct((t, LANES), jnp.int32), jax.ShapeDtypeStruct((t, LANES), F32)],
        compiler_params=_cparams("parallel"),
        name="outproj_router",
    )(x, o_nsa, o_rwkv, wa, wb, g, rw_hi, rw_lo, rb)


MOE_BM = 256


def _route(top_e, bm):
    n_tok = top_e.shape[0]
    flat_e = top_e.reshape(-1)
    n_assign = flat_e.shape[0]
    onehot = (flat_e[:, None] == jnp.arange(N_EXPERTS, dtype=jnp.int32)[None, :]).astype(jnp.int32)
    cum = jnp.cumsum(onehot, axis=0)
    counts = cum[-1]
    pos = jnp.take_along_axis(cum, flat_e[:, None], axis=1)[:, 0] - 1
    padded = (counts + bm - 1) // bm * bm
    pad_end = jnp.cumsum(padded)
    dest = (pad_end - padded)[flat_e] + pos
    n_blocks = -(-n_assign // bm) + N_EXPERTS
    slot_tok = jnp.zeros((n_blocks * bm,), jnp.int32).at[dest].set(jnp.arange(n_assign, dtype=jnp.int32) // MOE_TOPK)
    n_used = (pad_end[-1] // bm).astype(jnp.int32)
    blk = jnp.minimum(jnp.arange(n_blocks, dtype=jnp.int32), n_used - 1)
    blk_e = jnp.minimum(jnp.searchsorted(pad_end, blk * bm, side="right"), N_EXPERTS - 1).astype(jnp.int32)
    return slot_tok, dest.reshape(n_tok, MOE_TOPK).astype(jnp.int32), blk_e, n_used.reshape(1), n_blocks


def _row_copy(src_ref, dst_ref, sem, src_row, dst_row):
    return pltpu.make_async_copy(src_ref.at[pl.ds(src_row, 1)], dst_ref.at[pl.ds(dst_row, 1)], sem)


def _moe_gather_kernel(tok_ref, h_ref, o_ref, buf, sem):
    bm = buf.shape[0]

    def issue(r, c):
        _row_copy(h_ref, buf, sem, tok_ref[0, 0, r], r).start()
        return c

    lax.fori_loop(0, bm, issue, 0)

    def drain(r, c):
        _row_copy(h_ref, buf, sem, 0, r).wait()
        return c

    lax.fori_loop(0, bm, drain, 0)
    o_ref[...] = buf[...].astype(o_ref.dtype)


def _moe_gather(slot_tok, h2, bm):
    n_slots = slot_tok.shape[0]
    nblk = n_slots // bm
    return pl.pallas_call(
        _moe_gather_kernel,
        grid=(nblk,),
        in_specs=[pl.BlockSpec((1, 1, bm), lambda b: (b, 0, 0), memory_space=pltpu.SMEM),
                  pl.BlockSpec(memory_space=pl.ANY)],
        out_specs=pl.BlockSpec((bm, D_MODEL), lambda b: (b, 0)),
        out_shape=jax.ShapeDtypeStruct((n_slots, D_MODEL), BF16),
        scratch_shapes=[pltpu.VMEM((bm, D_MODEL), F32), pltpu.SemaphoreType.DMA(())],
        compiler_params=_cparams("arbitrary"),
        name="moe_gather",
    )(slot_tok.reshape(nblk, 1, bm), h2)


MOE_FT = 512
MOE_NF = D_FF // MOE_FT


def _expert_changed(be_ref, b):
    prev = be_ref[jnp.maximum(b - 1, 0)]
    return (b == 0) | (be_ref[b] != prev)


def _moe_up_kernel(be_ref, nu_ref, x_ref, wg_ref, wl_ref, bg_ref, bl_ref, o_ref, wg_s, wl_s):
    b = pl.program_id(1)

    @pl.when(b < nu_ref[0])
    def _():
        @pl.when(_expert_changed(be_ref, b))
        def _():
            wg_s[...] = wg_ref[0].astype(BF16)
            wl_s[...] = wl_ref[0].astype(BF16)

        x = x_ref[...]
        glu = jnp.minimum(_dot(x, wg_s[...]) + bg_ref[0], SWIGLU_LIMIT)
        lin = jnp.clip(_dot(x, wl_s[...]) + bl_ref[0], -SWIGLU_LIMIT, SWIGLU_LIMIT)
        o_ref[...] = (glu * jax.nn.sigmoid(SWIGLU_ALPHA * glu) * (lin + 1.0)).astype(o_ref.dtype)

    @pl.when(b >= nu_ref[0])
    def _():
        o_ref[...] = jnp.zeros_like(o_ref)


def _moe_up(blk_e, n_used, xs, w1, b1, bm):
    n_slots = xs.shape[0]
    nblk = n_slots // bm
    live = lambda b, nu: jnp.minimum(b, nu[0] - 1)
    grid_spec = pltpu.PrefetchScalarGridSpec(
        num_scalar_prefetch=2,
        grid=(MOE_NF, nblk),
        in_specs=[
            pl.BlockSpec((bm, D_MODEL), lambda f, b, be, nu: (live(b, nu), 0)),
            pl.BlockSpec((1, D_MODEL, MOE_FT), lambda f, b, be, nu: (be[b], 0, f)),
            pl.BlockSpec((1, D_MODEL, MOE_FT), lambda f, b, be, nu: (be[b], 0, MOE_NF + f)),
            pl.BlockSpec((1, 1, MOE_FT), lambda f, b, be, nu: (be[b], 0, f)),
            pl.BlockSpec((1, 1, MOE_FT), lambda f, b, be, nu: (be[b], 0, MOE_NF + f)),
        ],
        out_specs=pl.BlockSpec((bm, MOE_FT), lambda f, b, be, nu: (b, f)),
        scratch_shapes=[pltpu.VMEM((D_MODEL, MOE_FT), BF16), pltpu.VMEM((D_MODEL, MOE_FT), BF16)],
    )
    return pl.pallas_call(
        _moe_up_kernel,
        grid_spec=grid_spec,
        out_shape=jax.ShapeDtypeStruct((n_slots, D_FF), BF16),
        compiler_params=_cparams("arbitrary", "arbitrary"),
        name="moe_up",
    )(blk_e, n_used, xs, w1, w1, b1, b1)


def _moe_down_kernel(be_ref, nu_ref, a_ref, w_ref, b_ref, o_ref, w_s):
    b = pl.program_id(0)

    @pl.when(b < nu_ref[0])
    def _():
        @pl.when(_expert_changed(be_ref, b))
        def _():
            w_s[...] = w_ref[0].astype(BF16)

        o_ref[...] = _dot(a_ref[...], w_s[...]) + b_ref[0]

    @pl.when(b >= nu_ref[0])
    def _():
        o_ref[...] = jnp.zeros_like(o_ref)


def _moe_down(blk_e, n_used, act, w2, b2, bm):
    n_slots = act.shape[0]
    nblk = n_slots // bm
    live = lambda b, nu: jnp.minimum(b, nu[0] - 1)
    grid_spec = pltpu.PrefetchScalarGridSpec(
        num_scalar_prefetch=2,
        grid=(nblk,),
        in_specs=[
            pl.BlockSpec((bm, D_FF), lambda b, be, nu: (live(b, nu), 0)),
            pl.BlockSpec((1, D_FF, D_MODEL), lambda b, be, nu: (be[b], 0, 0)),
            pl.BlockSpec((1, 1, D_MODEL), lambda b, be, nu: (be[b], 0, 0)),
        ],
        out_specs=pl.BlockSpec((bm, D_MODEL), lambda b, be, nu: (b, 0)),
        scratch_shapes=[pltpu.VMEM((D_FF, D_MODEL), BF16)],
    )
    return pl.pallas_call(
        _moe_down_kernel,
        grid_spec=grid_spec,
        out_shape=jax.ShapeDtypeStruct((n_slots, D_MODEL), F32),
        compiler_params=_cparams("arbitrary"),
        name="moe_down",
    )(blk_e, n_used, act, w2, b2)


def _combine_kernel(slot_ref, x_ref, tg_ref, g_ref, oh_ref, yb_ref, o_ref, buf, sem):
    tm = x_ref.shape[0]
    n = MOE_TOPK * tm

    def issue(r, c):
        _row_copy(yb_ref, buf, sem, slot_ref[0, 0, r], r).start()
        return c

    lax.fori_loop(0, n, issue, 0)

    def drain(r, c):
        _row_copy(yb_ref, buf, sem, 0, r).wait()
        return c

    lax.fori_loop(0, n, drain, 0)
    x = x_ref[...]
    gates = tg_ref[...]
    for k in range(MOE_TOPK):
        gk = _dot_exact01(gates, oh_ref[k])
        x = x + jnp.concatenate([gk] * (D_MODEL // LANES), axis=1) * buf[k * tm:(k + 1) * tm]
    ms = jnp.mean(x * x, axis=-1, keepdims=True)
    o_ref[...] = x * lax.rsqrt(ms + NORM_EPS) * g_ref[...]


def _moe_combine(slot_of, x1, tg, g, yb, tm=128):
    t = x1.shape[0]
    nt = t // tm
    slots = slot_of.reshape(nt, tm, MOE_TOPK).transpose(0, 2, 1).reshape(nt, 1, MOE_TOPK * tm)
    onehot = (jnp.arange(LANES)[None, :, None] == jnp.arange(MOE_TOPK)[:, None, None]).astype(BF16)
    onehot = jnp.broadcast_to(onehot, (MOE_TOPK, LANES, LANES))
    return pl.pallas_call(
        _combine_kernel,
        grid=(nt,),
        in_specs=[pl.BlockSpec((1, 1, MOE_TOPK * tm), lambda i: (i, 0, 0), memory_space=pltpu.SMEM),
                  pl.BlockSpec((tm, D_MODEL), lambda i: (i, 0)),
                  pl.BlockSpec((tm, LANES), lambda i: (i, 0)),
                  pl.BlockSpec((1, D_MODEL), lambda i: (0, 0)),
                  pl.BlockSpec((MOE_TOPK, LANES, LANES), lambda i: (0, 0, 0)),
                  pl.BlockSpec(memory_space=pl.ANY)],
        out_specs=pl.BlockSpec((tm, D_MODEL), lambda i: (i, 0)),
        out_shape=jax.ShapeDtypeStruct((t, D_MODEL), F32),
        scratch_shapes=[pltpu.VMEM((MOE_TOPK * tm, D_MODEL), F32), pltpu.SemaphoreType.DMA(())],
        compiler_params=_cparams("arbitrary"),
        name="moe_combine",
    )(slots, x1, tg, g.reshape(1, -1), onehot, yb)


def _head_perm():
    p, g, e, d = jnp.meshgrid(jnp.arange(2), jnp.arange(NSA_GROUP), jnp.arange(2), jnp.arange(HEAD_DIM),
                              indexing="ij")
    return (((2 * p + e) * NSA_GROUP + g) * HEAD_DIM + d).reshape(-1)


def _alibi_slopes():
    return 2.0 ** (-8.0 * jnp.arange(1, NSA_HEADS + 1, dtype=F32) / NSA_HEADS)


def _slope_rows(ne, tq, pair):
    g, e, t = jnp.meshgrid(jnp.arange(NSA_GROUP), jnp.arange(ne), jnp.arange(tq), indexing="ij")
    head = (pair * ne + e) * NSA_GROUP + g
    return _alibi_slopes()[head].reshape(-1, 1)


def _gate_expand():
    p, br, g, e, d = jnp.meshgrid(jnp.arange(2), jnp.arange(3), jnp.arange(NSA_GROUP), jnp.arange(2),
                                  jnp.arange(HEAD_DIM), indexing="ij")
    src = (((2 * p + e) * NSA_GROUP + g) * 3 + br).reshape(-1)
    return (jnp.arange(LANES)[:, None] == src[None, :]).astype(BF16)


def _pair01(n_cmp):
    return (jnp.arange(n_cmp)[:, None] // 2 == jnp.arange(LANES)[None, :]).astype(BF16)


def _block_ones(width):
    i = jnp.arange(width)
    return (i[:, None] // HEAD_DIM == i[None, :] // HEAD_DIM).astype(BF16)


def _cmp_weights_bd(w):
    eye = jnp.eye(NSA_KV_HEADS, dtype=w.dtype)
    return jnp.einsum("hg,bde->bhdge", eye, w).reshape(CMP_BLOCK, KV_WIDTH, KV_WIDTH).astype(BF16)


def _pad_cols(a, width):
    return jnp.pad(a, ((0, 0), (0, width - a.shape[1])))


def kernel(x_prompt, x_sample, cache_nsa_kv, state_win_kv, state_rwkv, state_shift, page_table, norm_attn, w_in,
           nsa_w_cmp_k, nsa_w_cmp_v, nsa_out_gain, rwkv_mu, rwkv_w0, rwkv_w2, rwkv_a0, rwkv_a2, rwkv_g2, rwkv_k_k,
           rwkv_k_a, rwkv_r_k, rwkv_ln_w, rwkv_ln_b, w_out, norm_ffn, router_w, router_b, moe_w1, moe_b1, moe_w2,
           moe_b2, norm_final):
    nb_p, seq_p, _ = x_prompt.shape
    nb_s, seq_s, _ = x_sample.shape
    depth = norm_attn.shape[0]
    assert depth == 1 and seq_s == 8 and seq_p % (2 * Q_BLOCK) == 0
    tp, ts = nb_p * seq_p, nb_s * seq_s
    n_phys = cache_nsa_kv.shape[1]
    w_len = state_win_kv.shape[2]
    l = 0
    x = jnp.concatenate([x_prompt.reshape(tp, D_MODEL), x_sample.reshape(ts, D_MODEL)], axis=0)
    t_all = tp + ts

    perm = _head_perm()
    w = w_in[l]
    c0, c1, c2 = NSA_WIDTH, NSA_WIDTH + 4 * KV_WIDTH, NSA_WIDTH + 6 * KV_WIDTH
    c3 = c2 + 3 * NSA_HEADS
    h = _rmsnorm(x, norm_attn[l], BF16)
    q = _matmul(h, (w[:, :c0] * (HEAD_DIM ** -0.5))[:, perm].astype(BF16), F32, 512, 512, "proj_q")
    kvn = _matmul(h, w[:, c0:c1].astype(BF16), F32, 512, 512, "proj_kv")
    winn = _matmul(h, w[:, c1:c2].astype(BF16), F32, 512, 512, "proj_win")
    gl = _matmul(h, _pad_cols(w[:, c2:c3], LANES).astype(BF16), F32, 512, LANES, "proj_gate")
    z = _matmul(h, _pad_cols(w[:, c3:], Z_PAD).astype(BF16), F32, 512, Z_PAD // 3, "proj_z")

    wk_bd, wv_bd = _cmp_weights_bd(nsa_w_cmp_k[l]), _cmp_weights_bd(nsa_w_cmp_v[l])
    row_w = CMP_BLOCK * 4 * KV_WIDTH
    kvc_new = _compress(kvn.reshape(t_all // CMP_BLOCK, row_w), wk_bd, wv_bd, t_all // CMP_BLOCK)
    blocks_phys = n_phys * (PAGE_SIZE // CMP_BLOCK)
    kvc_phys = _compress(cache_nsa_kv[l].reshape(blocks_phys, row_w), wk_bd, wv_bd, 2048)
    eg = _gate_expand()
    gain_p = nsa_out_gain[l][perm]
    o_p = _nsa_prompt(q, gl, kvn, winn, kvc_new, jnp.stack([_slope_rows(2, Q_BLOCK, p) for p in range(2)]),
                      eg.reshape(LANES, 2, 12 * LANES).transpose(1, 0, 2), gain_p.reshape(2, 1, 512),
                      _pair01(seq_p // CMP_BLOCK), _block_ones(LANES), nb_p, seq_p)
    o_s, new_win_s = _nsa_sample(
        page_table, q, gl, kvn, winn, state_win_kv[l].reshape(nb_s, w_len, 2 * KV_WIDTH),
        cache_nsa_kv[l].reshape(n_phys, PAGE_SIZE, 4 * KV_WIDTH),
        kvc_phys.reshape(n_phys, 1, (PAGE_SIZE // CMP_BLOCK) * 2 * KV_WIDTH),
        _slope_rows(NSA_KV_HEADS, 8, 0), eg, gain_p.reshape(1, NSA_WIDTH),
        _pair01(page_table.shape[1] * PAGE_SIZE // CMP_BLOCK), _block_ones(2 * LANES), tp)
    o_nsa = jnp.concatenate([o_p, o_s.astype(BF16)], axis=0)

    w2p, a2p, g2p, pat = _rwkv_consts(rwkv_w2[l], rwkv_a2[l], rwkv_g2[l])
    bo128 = _block_ones(LANES)
    mu = jnp.pad(rwkv_mu[l], (0, Z_PAD - SHIFT_WIDTH))
    vecs = (mu, rwkv_w0[l], rwkv_a0[l], rwkv_k_k[l], rwkv_k_a[l], rwkv_r_k[l], w2p, a2p, g2p, bo128)
    tm_p = 512
    nt_p = tp // tm_p
    last = z[:tp].reshape(nt_p, tm_p, Z_PAD)[:, tm_p - 1]
    prev = jnp.concatenate([jnp.zeros((1, Z_PAD), F32), last[:-1]], axis=0)
    prev = jnp.where((jnp.arange(nt_p) % (seq_p // tm_p) == 0)[:, None], 0.0, prev)
    first_p = jnp.zeros((nt_p, 8, Z_PAD), F32).at[:, 0].set(prev)
    first_s = jnp.zeros((nb_s, seq_s, Z_PAD), F32).at[:, 0, :SHIFT_WIDTH].set(state_shift[l]).reshape(ts, Z_PAD)
    prep_p = _rwkv_prep(z, first_p, 0, tp, tm_p, seq_p, *vecs)
    prep_s = _rwkv_prep(z, first_s, tp, ts, 256, seq_s, *vecs)
    bo256 = _block_ones(2 * LANES)
    y_p, st_p = _rwkv_scan([a.reshape(nb_p, seq_p, RWKV_WIDTH) for a in prep_p[:6]], None, nb_p, seq_p, nb_p, 128,
                           pat, bo256)
    y_s, st_s = _rwkv_scan([a.reshape(nb_s, seq_s, RWKV_WIDTH) for a in prep_s[:6]], _state_to_tiles(state_rwkv[l]),
                           nb_s, seq_s, 4, seq_s, pat, bo256)
    o_rw_p = _rwkv_post(y_p.reshape(tp, RWKV_WIDTH), prep_p[6], prep_p[7], rwkv_ln_w[l], rwkv_ln_b[l], bo128, 512)
    o_rw_s = _rwkv_post(y_s.reshape(ts, RWKV_WIDTH), prep_s[6], prep_s[7], rwkv_ln_w[l], rwkv_ln_b[l], bo128, 512)
    o_rwkv = jnp.concatenate([o_rw_p, o_rw_s], axis=0)

    wo = w_out[l]
    rw = _pad_cols(router_w[l], LANES)
    rw_hi = rw.astype(BF16)
    rw_lo = (rw - rw_hi.astype(F32)).astype(BF16)
    rb = jnp.concatenate([router_b[l].astype(F32), jnp.full((LANES - N_EXPERTS,), -1e30, F32)]).reshape(1, LANES)
    x1, h2, top_e, top_g = _outproj_router(x, o_nsa, o_rwkv, wo[:NSA_WIDTH][perm].astype(BF16),
                                           wo[NSA_WIDTH:].astype(BF16), norm_ffn[l].reshape(1, -1), rw_hi, rw_lo, rb)
    slot_tok, slot_of, blk_e, n_used, _ = _route(top_e[:, :MOE_TOPK], MOE_BM)
    xs = _moe_gather(slot_tok, h2, MOE_BM)
    act = _moe_up(blk_e, n_used, xs, moe_w1[l], moe_b1[l].reshape(N_EXPERTS, 1, 2 * D_FF), MOE_BM)
    yb = _moe_down(blk_e, n_used, act, moe_w2[l], moe_b2[l].reshape(N_EXPERTS, 1, D_MODEL), MOE_BM)
    y = _moe_combine(slot_of, x1, top_g, norm_final, yb)

    hd = (NSA_KV_HEADS, HEAD_DIM)
    kv_p = kvn[:tp].reshape(1, nb_p, seq_p, 4, *hd)
    kv_s = kvn[tp:].reshape(1, nb_s, seq_s, 4, *hd)
    win_keep = min(WINDOW, seq_p)
    win_p = winn[:tp].reshape(nb_p, seq_p, 2, *hd)[None, :, seq_p - win_keep:]
    win_s = new_win_s.reshape(1, nb_s, w_len, 2, *hd)
    sh_p = z[:tp].reshape(nb_p, seq_p, Z_PAD)[None, :, seq_p - 1, :SHIFT_WIDTH]
    sh_s = z[tp:].reshape(nb_s, seq_s, Z_PAD)[None, :, seq_s - 1, :SHIFT_WIDTH]
    return (y[:tp].reshape(nb_p, seq_p, D_MODEL), y[tp:].reshape(nb_s, seq_s, D_MODEL), kv_p, kv_s, win_p, win_s,
            _tiles_to_state(st_p)[None], _tiles_to_state(st_s)[None], sh_p, sh_s)
```

```python
import functools

import jax
import jax.numpy as jnp
from jax import lax
from jax.experimental import pallas as pl
from jax.experimental.pallas import tpu as pltpu

F32 = jnp.float32
BF16 = jnp.bfloat16

D_MODEL = 2048
HEAD_DIM = 64
NSA_HEADS = 16
NSA_KV_HEADS = 4
NSA_GROUP = 4
NSA_WIDTH = 1024
KV_WIDTH = 256
CMP_BLOCK = 32
SEL_BLOCK = 64
SEL_TOPK = 16
WINDOW = 512
Q_BLOCK = 128
PAGE_SIZE = 128
RWKV_HEADS = 16
RWKV_WIDTH = 1024
DECAY_LORA = 64
ICLR_LORA = 64
GATE_LORA = 160
SHIFT_WIDTH = 3 * RWKV_WIDTH + DECAY_LORA + ICLR_LORA + GATE_LORA
Z_PAD = 3456
N_EXPERTS = 32
MOE_TOPK = 4
D_FF = 2048
SWIGLU_LIMIT = 7.0
SWIGLU_ALPHA = 1.702
NORM_EPS = 1e-5
GN_EPS = 64e-5
LANES = 128
VMEM_LIMIT = 56 * 1024 * 1024


def _cparams(*sem):
    return pltpu.CompilerParams(dimension_semantics=sem, vmem_limit_bytes=VMEM_LIMIT)


def _dot(a, b):
    return jnp.dot(a, b, preferred_element_type=F32)


def _dot_nt(a, b):
    return lax.dot_general(a, b, (((1,), (1,)), ((), ())), preferred_element_type=F32)


def _split3(x):
    hi = x.astype(BF16)
    r1 = x - hi.astype(F32)
    mid = r1.astype(BF16)
    lo = (r1 - mid.astype(F32)).astype(BF16)
    return hi, mid, lo


def _dot_exact01(x, m01):
    hi, mid, lo = _split3(x)
    return _dot(hi, m01) + _dot(mid, m01) + _dot(lo, m01)


def _rmsnorm_kernel(x_ref, g_ref, o_ref):
    x = x_ref[...]
    ms = jnp.mean(x * x, axis=-1, keepdims=True)
    o_ref[...] = (x * lax.rsqrt(ms + NORM_EPS) * g_ref[...]).astype(o_ref.dtype)


def _rmsnorm(x, g, out_dtype, tm=512):
    m, d = x.shape
    return pl.pallas_call(
        _rmsnorm_kernel,
        grid=(m // tm,),
        in_specs=[pl.BlockSpec((tm, d), lambda i: (i, 0)), pl.BlockSpec((1, d), lambda i: (0, 0))],
        out_specs=pl.BlockSpec((tm, d), lambda i: (i, 0)),
        out_shape=jax.ShapeDtypeStruct((m, d), out_dtype),
        compiler_params=_cparams("parallel"),
        name="rmsnorm",
    )(x, g.reshape(1, d))


def _mm_kernel(a_ref, b_ref, o_ref):
    o_ref[...] = _dot(a_ref[...], b_ref[...]).astype(o_ref.dtype)


def _matmul(a, b, out_dtype, tm, tn, name):
    m, k = a.shape
    n = b.shape[1]
    return pl.pallas_call(
        _mm_kernel,
        grid=(m // tm, n // tn),
        in_specs=[pl.BlockSpec((tm, k), lambda i, j: (i, 0)), pl.BlockSpec((k, tn), lambda i, j: (0, j))],
        out_specs=pl.BlockSpec((tm, tn), lambda i, j: (i, j)),
        out_shape=jax.ShapeDtypeStruct((m, n), out_dtype),
        compiler_params=_cparams("parallel", "arbitrary"),
        name=name,
    )(a, b)


def _compress_kernel(x_ref, w_ref, o_ref):
    nb = o_ref.shape[0]
    acc = jnp.zeros((nb, LANES), F32)
    for b in range(CMP_BLOCK):
        a = x_ref[pl.ds(b, nb, stride=CMP_BLOCK), :].astype(BF16)
        acc = acc + _dot(a, w_ref[0, b])
    o_ref[...] = acc


def _compress(rows, w_bd, n_blocks, nb):
    slabs = 2 * KV_WIDTH // LANES
    return pl.pallas_call(
        _compress_kernel,
        grid=(n_blocks // nb, slabs),
        in_specs=[
            pl.BlockSpec((nb * CMP_BLOCK, LANES), lambda i, s: (i, s)),
            pl.BlockSpec((1, CMP_BLOCK, LANES, LANES), lambda i, s: (s // 2, 0, 0, 0)),
        ],
        out_specs=pl.BlockSpec((nb, LANES), lambda i, s: (i, s)),
        out_shape=jax.ShapeDtypeStruct((n_blocks, 2 * KV_WIDTH), F32),
        compiler_params=_cparams("parallel", "arbitrary"),
        name="nsa_compress",
    )(rows, w_bd)


def _lane_masks(ne):
    lane = lax.broadcasted_iota(jnp.int32, (1, ne * HEAD_DIM), 1)
    return [(lane >= e * HEAD_DIM) & (lane < (e + 1) * HEAD_DIM) for e in range(ne)]


def _build_qbd(slabs, masks):
    parts = [jnp.where(m, s, 0.0) for s in slabs for m in masks]
    return jnp.concatenate(parts, axis=0).astype(BF16)


def _masked_softmax(s, mask):
    s = jnp.where(mask, s, -jnp.inf)
    m = jnp.max(s, axis=-1, keepdims=True)
    m = jnp.where(jnp.isfinite(m), m, 0.0)
    e = jnp.exp(s - m)
    return e / jnp.maximum(jnp.sum(e, axis=-1, keepdims=True), 1e-30)


def _cmp_branch(qbd, kc, vc, slope, tpos, n_cmp):
    s = _dot_nt(qbd, kc.astype(BF16))
    cmp_end = (lax.broadcasted_iota(jnp.int32, (1, n_cmp), 1) + 1) * CMP_BLOCK - 1
    dist = (tpos - cmp_end).astype(F32)
    p = _masked_softmax(s - slope * dist, dist >= 0)
    return _dot(p.astype(BF16), vc.astype(BF16)), p


def _select_blocks(p_cmp, pair01, tpos_et, n_rank, rows_et):
    psum = p_cmp[0:rows_et]
    for g in range(1, NSA_GROUP):
        psum = psum + p_cmp[g * rows_et:(g + 1) * rows_et]
    imp = _dot_exact01(psum, pair01)
    j = lax.broadcasted_iota(jnp.int32, (1, LANES), 1)
    cur = jnp.right_shift(tpos_et, 6)
    forced = (j == cur) | (j == 0)
    key = jnp.where(forced, jnp.inf, jnp.where(j < cur, imp, -jnp.inf))
    rank = jnp.zeros(key.shape, F32)
    for i in range(n_rank):
        ci = key[:, i:i + 1]
        ahead = (ci > key) | ((ci == key) & (j > i))
        rank = rank + jnp.where(ahead, 1.0, 0.0)
    return jnp.where(rank < SEL_TOPK, 1.0, 0.0)


def _sel_scores(qbd, kt, sel_bf, k0, tk, tpos, slope):
    s = _dot_nt(qbd, kt)
    kpos = k0 + lax.broadcasted_iota(jnp.int32, (1, tk), 1)
    jrow = lax.broadcasted_iota(jnp.int32, (LANES, 1), 0)
    expand = jnp.where(jnp.right_shift(kpos, 6) == jrow, 1.0, 0.0).astype(BF16)
    chosen = _dot(sel_bf, expand)
    chosen = jnp.concatenate([chosen] * NSA_GROUP, axis=0)
    dist = (tpos - kpos).astype(F32)
    vis = (chosen > 0.5) & (dist >= 0)
    return jnp.where(vis, s - slope * dist, -jnp.inf)


def _online_update(s, vt, m_old, l_old, acc_old):
    m_new = jnp.maximum(m_old, jnp.max(s, axis=-1, keepdims=True))
    alpha = jnp.exp(m_old - m_new)
    p = jnp.exp(s - m_new)
    l_new = alpha * l_old + jnp.sum(p, axis=-1, keepdims=True)
    acc_new = alpha * acc_old + _dot(p.astype(BF16), vt)
    return m_new, l_new, acc_new


M_INIT = -1e30


def _win_branch(qbd, kw, vw, start, tpos, slope):
    wk = kw.shape[0]
    s = _dot_nt(qbd, kw.astype(BF16))
    kpos = start + lax.broadcasted_iota(jnp.int32, (1, wk), 1)
    dist = tpos - kpos
    ok = (dist >= 0) & (dist <= WINDOW)
    p = _masked_softmax(s - slope * dist.astype(F32), ok)
    return _dot(p.astype(BF16), vw.astype(BF16))


def _merge_heads(o, g, masks, tq):
    ne = len(masks)
    out = None
    for e in range(ne):
        r0 = (g * ne + e) * tq
        part = jnp.where(masks[e], o[r0:r0 + tq], 0.0)
        out = part if out is None else out + part
    return out


def _gate_norm(o_c, o_s, o_w, gates, gain, bo01, masks, tq, width):
    outs = []
    for g in range(NSA_GROUP):
        x = None
        for br, o in enumerate((o_c, o_s, o_w)):
            gt = gates[:, (br * NSA_GROUP + g) * width:(br * NSA_GROUP + g + 1) * width]
            term = gt * _merge_heads(o, g, masks, tq)
            x = term if x is None else x + term
        ms = _dot_exact01(x * x, bo01) * (1.0 / HEAD_DIM)
        outs.append(x * lax.rsqrt(ms + NORM_EPS) * gain[:, g * width:(g + 1) * width])
    return outs


def _nsa_prompt_kernel(q_ref, gl_ref, ks_ref, vs_ref, kw_ref, vw_ref, kc_ref, vc_ref, slope_ref, eg_ref,
                       gain_ref, pair_ref, bo_ref, o_ref, m_ref, l_ref, acc_ref, *, seq):
    ne, tq = 2, Q_BLOCK
    width = ne * HEAD_DIM
    rows_et = ne * tq
    n_cmp = seq // CMP_BLOCK
    tk = 2 * Q_BLOCK
    i = pl.program_id(1)
    p0 = i * tq
    masks = _lane_masks(ne)
    qf = q_ref[...]
    qbd = _build_qbd([qf[:, g * width:(g + 1) * width] for g in range(NSA_GROUP)], masks)
    r = lax.broadcasted_iota(jnp.int32, (NSA_GROUP * rows_et, 1), 0)
    tpos = p0 + (r & (tq - 1))
    slope = slope_ref[0]

    o_c, p_c = _cmp_branch(qbd, kc_ref[...], vc_ref[...], slope, tpos, n_cmp)
    sel = _select_blocks(p_c, pair_ref[...], tpos[0:rows_et], seq // SEL_BLOCK, rows_et).astype(BF16)

    m_ref[...] = jnp.full(m_ref.shape, M_INIT, F32)
    l_ref[...] = jnp.zeros(l_ref.shape, F32)
    acc_ref[...] = jnp.zeros(acc_ref.shape, F32)

    def body(t, carry):
        k0 = pl.multiple_of(t * tk, tk)
        kt = ks_ref[pl.ds(k0, tk), :].astype(BF16)
        vt = vs_ref[pl.ds(k0, tk), :].astype(BF16)
        s = _sel_scores(qbd, kt, sel, k0, tk, tpos, slope)
        m_new, l_new, acc_new = _online_update(s, vt, m_ref[...], l_ref[...], acc_ref[...])
        m_ref[...] = m_new
        l_ref[...] = l_new
        acc_ref[...] = acc_new
        return carry

    lax.fori_loop(0, lax.shift_right_logical(i + 2, 1), body, 0)
    o_s = acc_ref[...] / jnp.maximum(l_ref[...], 1e-30)

    start = pl.multiple_of(jnp.maximum(p0 - WINDOW, 0), Q_BLOCK)
    wk = WINDOW + tq
    o_w = _win_branch(qbd, kw_ref[pl.ds(start, wk), :], vw_ref[pl.ds(start, wk), :], start, tpos, slope)

    gates = _dot_exact01(jax.nn.sigmoid(gl_ref[...]), eg_ref[0])
    outs = _gate_norm(o_c, o_s, o_w, gates, gain_ref[0], bo_ref[...], masks, tq, width)
    for g in range(NSA_GROUP):
        o_ref[:, g * width:(g + 1) * width] = outs[g].astype(o_ref.dtype)


def _nsa_prompt(q, gl, kvn, winn, kvc, slopes, eg, gain, pair01, bo01, n, seq):
    nq = seq // Q_BLOCK
    n_cmp = seq // CMP_BLOCK
    rr = NSA_GROUP * 2 * Q_BLOCK
    kern = functools.partial(_nsa_prompt_kernel, seq=seq)
    return pl.pallas_call(
        kern,
        grid=(n, nq, 2),
        in_specs=[
            pl.BlockSpec((Q_BLOCK, 512), lambda b, i, p: (b * nq + i, p)),
            pl.BlockSpec((Q_BLOCK, LANES), lambda b, i, p: (b * nq + i, 0)),
            pl.BlockSpec((seq, LANES), lambda b, i, p: (b, 4 + p)),
            pl.BlockSpec((seq, LANES), lambda b, i, p: (b, 6 + p)),
            pl.BlockSpec((seq, LANES), lambda b, i, p: (b, p)),
            pl.BlockSpec((seq, LANES), lambda b, i, p: (b, 2 + p)),
            pl.BlockSpec((n_cmp, LANES), lambda b, i, p: (b, p)),
            pl.BlockSpec((n_cmp, LANES), lambda b, i, p: (b, 2 + p)),
            pl.BlockSpec((1, rr, 1), lambda b, i, p: (p, 0, 0)),
            pl.BlockSpec((1, LANES, 12 * LANES), lambda b, i, p: (p, 0, 0)),
            pl.BlockSpec((1, 1, 512), lambda b, i, p: (p, 0, 0)),
            pl.BlockSpec((n_cmp, LANES), lambda b, i, p: (0, 0)),
            pl.BlockSpec((LANES, LANES), lambda b, i, p: (0, 0)),
        ],
        out_specs=pl.BlockSpec((Q_BLOCK, 512), lambda b, i, p: (b * nq + i, p)),
        out_shape=jax.ShapeDtypeStruct((n * seq, NSA_WIDTH), BF16),
        scratch_shapes=[pltpu.VMEM((rr, 1), F32), pltpu.VMEM((rr, 1), F32), pltpu.VMEM((rr, LANES), F32)],
        compiler_params=_cparams("parallel", "parallel", "parallel"),
        name="nsa_prompt",
    )(q, gl, kvn, kvn, winn, winn, kvc, kvc, slopes, eg, gain, pair01, bo01)


def _pad_rows(x, rows):
    return jnp.concatenate([x, jnp.zeros((rows - x.shape[0], x.shape[1]), x.dtype)], axis=0)


def _nsa_sample_kernel(pt_ref, q_ref, gl_ref, kvn_ref, winn_ref, wst_ref, *rest, past_len):
    del pt_ref
    n_pages = past_len // PAGE_SIZE
    page_refs = rest[:n_pages]
    kc_refs = rest[n_pages:2 * n_pages]
    slope_ref, eg_ref, gain_ref, pair_ref, bo_ref, o_ref, nw_ref, kc_s, vc_s = rest[2 * n_pages:]
    ne, tq = NSA_KV_HEADS, 8
    width = ne * HEAD_DIM
    rows_et = ne * tq
    n_cmp = past_len // CMP_BLOCK
    blocks_per_page = PAGE_SIZE // CMP_BLOCK
    masks = _lane_masks(ne)

    def pair_slabs(x, g, stride):
        return jnp.concatenate([x[:, g * LANES:(g + 1) * LANES],
                                x[:, stride + g * LANES:stride + (g + 1) * LANES]], axis=1)

    qf = q_ref[...]
    qbd = _build_qbd([pair_slabs(qf, g, 4 * LANES) for g in range(NSA_GROUP)], masks)
    r = lax.broadcasted_iota(jnp.int32, (NSA_GROUP * rows_et, 1), 0)
    tpos = past_len + (r & (tq - 1))
    slope = slope_ref[...]

    for j in range(n_pages):
        for b in range(blocks_per_page):
            c = j * blocks_per_page + b
            kc_s[c:c + 1, :] = kc_refs[j][0, :, b * 2 * KV_WIDTH:b * 2 * KV_WIDTH + KV_WIDTH]
            vc_s[c:c + 1, :] = kc_refs[j][0, :, b * 2 * KV_WIDTH + KV_WIDTH:(b + 1) * 2 * KV_WIDTH]
    o_c, p_c = _cmp_branch(qbd, kc_s[...], vc_s[...], slope, tpos, n_cmp)
    n_sel_past = past_len // SEL_BLOCK
    sel = _select_blocks(p_c, pair_ref[...], tpos[0:rows_et], n_sel_past + 1, rows_et).astype(BF16)

    m = jnp.full((NSA_GROUP * rows_et, 1), M_INIT, F32)
    l = jnp.zeros((NSA_GROUP * rows_et, 1), F32)
    acc = jnp.zeros((NSA_GROUP * rows_et, width), F32)
    for j in range(n_pages):
        kt = page_refs[j][:, 0:KV_WIDTH].astype(BF16)
        vt = page_refs[j][:, KV_WIDTH:2 * KV_WIDTH].astype(BF16)
        s = _sel_scores(qbd, kt, sel, j * PAGE_SIZE, PAGE_SIZE, tpos, slope)
        m, l, acc = _online_update(s, vt, m, l, acc)
    kt = _pad_rows(kvn_ref[:, 2 * KV_WIDTH:3 * KV_WIDTH], PAGE_SIZE).astype(BF16)
    vt = _pad_rows(kvn_ref[:, 3 * KV_WIDTH:4 * KV_WIDTH], PAGE_SIZE).astype(BF16)
    s = _sel_scores(qbd, kt, sel, past_len, PAGE_SIZE, tpos, slope)
    m, l, acc = _online_update(s, vt, m, l, acc)
    o_s = acc / jnp.maximum(l, 1e-30)

    kw = jnp.concatenate([wst_ref[0, :, 0:KV_WIDTH], _pad_rows(winn_ref[:, 0:KV_WIDTH], PAGE_SIZE)], axis=0)
    vw = jnp.concatenate([wst_ref[0, :, KV_WIDTH:], _pad_rows(winn_ref[:, KV_WIDTH:], PAGE_SIZE)], axis=0)
    o_w = _win_branch(qbd, kw, vw, past_len - WINDOW, tpos, slope)

    gx = _dot_exact01(jax.nn.sigmoid(gl_ref[...]), eg_ref[...])
    gates = jnp.concatenate([pair_slabs(gx, br * NSA_GROUP + g, 12 * LANES)
                             for br in range(3) for g in range(NSA_GROUP)], axis=1)
    gain = jnp.concatenate([pair_slabs(gain_ref[...], g, 4 * LANES) for g in range(NSA_GROUP)], axis=1)
    outs = _gate_norm(o_c, o_s, o_w, gates, gain, bo_ref[...], masks, tq, width)
    for g in range(NSA_GROUP):
        o_ref[:, g * LANES:(g + 1) * LANES] = outs[g][:, 0:LANES]
        o_ref[:, (NSA_GROUP + g) * LANES:(NSA_GROUP + g + 1) * LANES] = outs[g][:, LANES:]

    w_len = wst_ref.shape[1]
    nw_ref[0, 0:w_len - tq, :] = wst_ref[0, tq:w_len, :]
    nw_ref[0, w_len - tq:w_len, :] = winn_ref[...]


def _nsa_sample(page_table, q, gl, kvn, winn, win_state, cache_pages, kvc_pages, slopes, eg, gain, pair01, bo01,
                row0):
    n_seq, n_pages = page_table.shape
    past_len = n_pages * PAGE_SIZE
    w_len = win_state.shape[1]
    blk0 = row0 // 8
    rr = NSA_GROUP * NSA_KV_HEADS * 8
    tok = lambda s, pt: (blk0 + s, 0)
    const2 = lambda s, pt: (0, 0)
    in_specs = [
        pl.BlockSpec((8, NSA_WIDTH), tok),
        pl.BlockSpec((8, LANES), tok),
        pl.BlockSpec((8, 4 * KV_WIDTH), tok),
        pl.BlockSpec((8, 2 * KV_WIDTH), tok),
        pl.BlockSpec((1, w_len, 2 * KV_WIDTH), lambda s, pt: (s, 0, 0)),
    ]
    in_specs += [pl.BlockSpec((PAGE_SIZE, 2 * KV_WIDTH), functools.partial(lambda s, pt, j: (pt[s, j], 1), j=j))
                 for j in range(n_pages)]
    in_specs += [pl.BlockSpec((1, 1, 8 * KV_WIDTH), functools.partial(lambda s, pt, j: (pt[s, j], 0, 0), j=j))
                 for j in range(n_pages)]
    in_specs += [
        pl.BlockSpec((rr, 1), const2),
        pl.BlockSpec((LANES, 24 * LANES), const2),
        pl.BlockSpec((1, NSA_WIDTH), const2),
        pl.BlockSpec((past_len // CMP_BLOCK, LANES), const2),
        pl.BlockSpec((2 * LANES, 2 * LANES), const2),
    ]
    grid_spec = pltpu.PrefetchScalarGridSpec(
        num_scalar_prefetch=1,
        grid=(n_seq,),
        in_specs=in_specs,
        out_specs=[pl.BlockSpec((8, NSA_WIDTH), lambda s, pt: (s, 0)),
                   pl.BlockSpec((1, w_len, 2 * KV_WIDTH), lambda s, pt: (s, 0, 0))],
        scratch_shapes=[pltpu.VMEM((past_len // CMP_BLOCK, KV_WIDTH), F32),
                        pltpu.VMEM((past_len // CMP_BLOCK, KV_WIDTH), F32)],
    )
    return pl.pallas_call(
        functools.partial(_nsa_sample_kernel, past_len=past_len),
        grid_spec=grid_spec,
        out_shape=[jax.ShapeDtypeStruct((n_seq * 8, NSA_WIDTH), F32),
                   jax.ShapeDtypeStruct((n_seq, w_len, 2 * KV_WIDTH), F32)],
        compiler_params=_cparams("parallel"),
        name="nsa_sample",
    )(page_table, q, gl, kvn, winn, win_state, *([cache_pages] * n_pages), *([kvc_pages] * n_pages),
      slopes, eg, gain, pair01, bo01)


def _head_sums(x, bo01):
    return jnp.concatenate([_dot_exact01(x[:, s * LANES:(s + 1) * LANES], bo01)
                            for s in range(x.shape[1] // LANES)], axis=1)


def _rwkv_prep_kernel(z_ref, first_ref, mu_ref, w0_ref, a0_ref, kk_ref, ka_ref, rk_ref, w2_ref, a2_ref, g2_ref,
                      bo_ref, r_o, w_o, k_o, v_o, kk_o, nb_o, g_o, bonus_o, *, seq_rows):
    z = z_ref[...]
    tm = z.shape[0]
    row = lax.broadcasted_iota(jnp.int32, (tm, 1), 0)
    if seq_rows >= tm:
        prev = jnp.where(row == 0, first_ref[0, 0:1, :], pltpu.roll(z, 1, 0))
    else:
        prev = jnp.where((row & (seq_rows - 1)) == 0, first_ref[...], pltpu.roll(z, 1, 0))
    zm = z + (prev - z) * mu_ref[...]
    w = RWKV_WIDTH
    r, k, v = zm[:, 0:w], zm[:, w:2 * w], zm[:, 2 * w:3 * w]
    wa = zm[:, 3 * w:3 * w + LANES]
    gd = zm[:, 3 * w + LANES:]
    wl = w0_ref[...] + _dot(jnp.tanh(wa).astype(BF16), w2_ref[...])
    neg = -wl
    softplus = jnp.maximum(neg, 0.0) + jnp.log1p(jnp.exp(-jnp.abs(neg)))
    decay = jnp.exp(-jnp.exp(-softplus - 0.5))
    a = jax.nn.sigmoid(a0_ref[...] + _dot(wa.astype(BF16), a2_ref[...]))
    g = _dot(jax.nn.sigmoid(gd).astype(BF16), g2_ref[...])
    bo = bo_ref[...]
    kk = k * kk_ref[...]
    kk = kk / jnp.maximum(jnp.sqrt(_head_sums(kk * kk, bo)), 1e-12)
    k2 = k * (1.0 + (a - 1.0) * ka_ref[...])
    r_o[...] = r
    w_o[...] = decay
    k_o[...] = k2
    v_o[...] = v
    kk_o[...] = kk
    nb_o[...] = -(kk * a)
    g_o[...] = g
    bonus_o[...] = _head_sums(r * k2 * rk_ref[...], bo) * v


def _rwkv_prep(z, first, row0, rows, tm, seq_rows, mu, w0, a0, k_k, k_a, r_k, w2p, a2p, g2p, bo01):
    blk0 = row0 // tm
    vec = lambda a: a.reshape(1, -1)
    cvec = pl.BlockSpec((1, RWKV_WIDTH), lambda i: (0, 0))
    if seq_rows >= tm:
        first_spec = pl.BlockSpec((1, 8, Z_PAD), lambda i: (i, 0, 0))
    else:
        first_spec = pl.BlockSpec((tm, Z_PAD), lambda i: (i, 0))
    out = jax.ShapeDtypeStruct((rows, RWKV_WIDTH), F32)
    ospec = pl.BlockSpec((tm, RWKV_WIDTH), lambda i: (i, 0))
    return pl.pallas_call(
        functools.partial(_rwkv_prep_kernel, seq_rows=seq_rows),
        grid=(rows // tm,),
        in_specs=[pl.BlockSpec((tm, Z_PAD), lambda i: (blk0 + i, 0)), first_spec,
                  pl.BlockSpec((1, Z_PAD), lambda i: (0, 0)), cvec, cvec, cvec, cvec, cvec,
                  pl.BlockSpec((LANES, RWKV_WIDTH), lambda i: (0, 0)),
                  pl.BlockSpec((LANES, RWKV_WIDTH), lambda i: (0, 0)),
                  pl.BlockSpec((2 * LANES, RWKV_WIDTH), lambda i: (0, 0)),
                  pl.BlockSpec((LANES, LANES), lambda i: (0, 0))],
        out_specs=[ospec] * 8,
        out_shape=[out] * 8,
        compiler_params=_cparams("parallel"),
        name="rwkv_prep",
    )(z, first, vec(mu), vec(w0), vec(a0), vec(k_k), vec(k_a), vec(r_k), w2p, a2p, g2p, bo01)


def _rwkv_scan_kernel(*refs, nbatch, tc, has_init):
    ops = refs[:6]
    pos = 6
    s0_ref = refs[pos] if has_init else None
    pos += 1 if has_init else 0
    pat_ref, bo_ref, y_ref, s_out, st = refs[pos:pos + 5]
    c = pl.program_id(1)
    nq = RWKV_HEADS // 4
    tw = 4 * HEAD_DIM

    @pl.when(c == 0)
    def _():
        if has_init:
            st[...] = s0_ref[...]
        else:
            st[...] = jnp.zeros_like(st)

    pat = pat_ref[...]
    bo = bo_ref[...]

    def step(t, carry):
        tiles = [(b, q) for b in range(nbatch) for q in range(nq)]
        rowv = lambda k, b, q: ops[k][b, pl.ds(t, 1), q * tw:(q + 1) * tw]
        lhs = []
        for b, q in tiles:
            s = st[b, q]
            lhs.append((s * rowv(4, b, q)).astype(BF16))
            lhs.append((pat * rowv(3, b, q)).astype(BF16))
        red = _dot(jnp.concatenate(lhs, axis=0), bo)
        outs = []
        for n, (b, q) in enumerate(tiles):
            sa = red[n * 2 * HEAD_DIM:n * 2 * HEAD_DIM + HEAD_DIM]
            vb = red[n * 2 * HEAD_DIM + HEAD_DIM:(n + 1) * 2 * HEAD_DIM]
            s2 = st[b, q] * rowv(1, b, q) + sa * rowv(5, b, q) + vb * rowv(2, b, q)
            st[b, q] = s2
            outs.append((s2 * rowv(0, b, q)).astype(BF16))
        yb = _dot(jnp.concatenate(outs, axis=0), bo)
        for n, (b, q) in enumerate(tiles):
            yrow = jnp.sum(yb[n * HEAD_DIM:(n + 1) * HEAD_DIM] * pat, axis=0, keepdims=True)
            y_ref[b, pl.ds(t, 1), q * tw:(q + 1) * tw] = yrow
        return carry

    lax.fori_loop(0, tc, step, 0, unroll=2)

    @pl.when(c == pl.num_programs(1) - 1)
    def _():
        s_out[...] = st[...]


def _rwkv_scan(ops, s0, nseq, seq_rows, nbatch, tc, pat, bo01):
    nq = RWKV_HEADS // 4
    tok_spec = pl.BlockSpec((nbatch, tc, RWKV_WIDTH), lambda gi, c: (gi, c, 0))
    st_spec = pl.BlockSpec((nbatch, nq, HEAD_DIM, 4 * HEAD_DIM), lambda gi, c: (gi, 0, 0, 0))
    in_specs = [tok_spec] * 6
    args = list(ops)
    if s0 is not None:
        in_specs.append(st_spec)
        args.append(s0)
    in_specs += [pl.BlockSpec((HEAD_DIM, 4 * HEAD_DIM), lambda gi, c: (0, 0)),
                 pl.BlockSpec((4 * HEAD_DIM, 4 * HEAD_DIM), lambda gi, c: (0, 0))]
    args += [pat, bo01]
    return pl.pallas_call(
        functools.partial(_rwkv_scan_kernel, nbatch=nbatch, tc=tc, has_init=s0 is not None),
        grid=(nseq // nbatch, seq_rows // tc),
        in_specs=in_specs,
        out_specs=[tok_spec, st_spec],
        out_shape=[jax.ShapeDtypeStruct((nseq, seq_rows, RWKV_WIDTH), F32),
                   jax.ShapeDtypeStruct((nseq, nq, HEAD_DIM, 4 * HEAD_DIM), F32)],
        scratch_shapes=[pltpu.VMEM((nbatch, nq, HEAD_DIM, 4 * HEAD_DIM), F32)],
        compiler_params=_cparams("parallel", "arbitrary"),
        name="rwkv_scan",
    )(*args)


def _rwkv_post_kernel(y_ref, g_ref, bonus_ref, lnw_ref, lnb_ref, bo_ref, o_ref):
    y = y_ref[...]
    bo = bo_ref[...]
    d = y - _head_sums(y, bo) * (1.0 / HEAD_DIM)
    var = _head_sums(d * d, bo) * (1.0 / HEAD_DIM)
    o = (d * lax.rsqrt(var + GN_EPS) * lnw_ref[...] + lnb_ref[...] + bonus_ref[...]) * g_ref[...]
    o_ref[...] = o.astype(o_ref.dtype)


def _rwkv_post(y, g, bonus, ln_w, ln_b, bo01, tm):
    rows = y.shape[0]
    tok = pl.BlockSpec((tm, RWKV_WIDTH), lambda i: (i, 0))
    cvec = pl.BlockSpec((1, RWKV_WIDTH), lambda i: (0, 0))
    return pl.pallas_call(
        _rwkv_post_kernel,
        grid=(rows // tm,),
        in_specs=[tok, tok, tok, cvec, cvec, pl.BlockSpec((LANES, LANES), lambda i: (0, 0))],
        out_specs=tok,
        out_shape=jax.ShapeDtypeStruct((rows, RWKV_WIDTH), BF16),
        compiler_params=_cparams("parallel"),
        name="rwkv_post",
    )(y, g, bonus, ln_w.reshape(1, -1), ln_b.reshape(1, -1), bo01)


def _rwkv_consts(w2, a2, g2):
    w2p = jnp.concatenate([w2, jnp.zeros((LANES - DECAY_LORA, RWKV_WIDTH), w2.dtype)], axis=0)
    a2p = jnp.concatenate([jnp.zeros((DECAY_LORA, RWKV_WIDTH), a2.dtype), a2], axis=0)
    g2p = jnp.concatenate([g2, jnp.zeros((2 * LANES - GATE_LORA, RWKV_WIDTH), g2.dtype)], axis=0)
    v = jnp.arange(HEAD_DIM)
    pat = (v[:, None] == (jnp.arange(4 * HEAD_DIM)[None, :] % HEAD_DIM)).astype(F32)
    return w2p.astype(BF16), a2p.astype(BF16), g2p.astype(BF16), pat


def _state_to_tiles(s):
    n = s.shape[0]
    return s.reshape(n, 4, 4, HEAD_DIM, HEAD_DIM).transpose(0, 1, 3, 2, 4).reshape(n, 4, HEAD_DIM, 4 * HEAD_DIM)


def _tiles_to_state(t):
    n = t.shape[0]
    return t.reshape(n, 4, HEAD_DIM, 4, HEAD_DIM).transpose(0, 1, 3, 2, 4).reshape(n, RWKV_HEADS, HEAD_DIM, HEAD_DIM)


def _outproj_kernel(x_ref, on_ref, or_ref, wa_ref, wb_ref, g_ref, rwh_ref, rwl_ref, rb_ref,
                    x1_ref, h2_ref, te_ref, tg_ref):
    x1 = x_ref[...] + _dot(on_ref[...], wa_ref[...]) + _dot(or_ref[...], wb_ref[...])
    x1_ref[...] = x1
    ms = jnp.mean(x1 * x1, axis=-1, keepdims=True)
    hf = x1 * lax.rsqrt(ms + NORM_EPS) * g_ref[...]
    h2_ref[...] = hf
    hh = hf.astype(BF16)
    hl = (hf - hh.astype(F32)).astype(BF16)
    logits = _dot(hh, rwh_ref[...]) + _dot(hl, rwh_ref[...]) + _dot(hh, rwl_ref[...]) + rb_ref[...]
    lane = lax.broadcasted_iota(jnp.int32, logits.shape, 1).astype(F32)
    vals, idxs = [], []
    for _ in range(MOE_TOPK):
        m = jnp.max(logits, axis=-1, keepdims=True)
        idx = jnp.min(jnp.where(logits == m, lane, float(LANES)), axis=-1, keepdims=True)
        vals.append(m)
        idxs.append(idx)
        logits = jnp.where(lane == idx, -jnp.inf, logits)
    es = [jnp.exp(v - vals[0]) for v in vals]
    denom = es[0] + es[1] + es[2] + es[3]
    te = jnp.zeros(logits.shape, F32)
    tg = jnp.zeros(logits.shape, F32)
    for k in range(MOE_TOPK):
        te = jnp.where(lane == float(k), idxs[k], te)
        tg = jnp.where(lane == float(k), es[k] / denom, tg)
    te_ref[...] = te.astype(jnp.int32)
    tg_ref[...] = tg


def _outproj_router(x, o_nsa, o_rwkv, wa, wb, g, rw_hi, rw_lo, rb, tm=256):
    t = x.shape[0]
    tok = lambda w: pl.BlockSpec((tm, w), lambda i: (i, 0))
    full = lambda a: pl.BlockSpec(a.shape, lambda i: (0,) * a.ndim)
    return pl.pallas_call(
        _outproj_kernel,
        grid=(t // tm,),
        in_specs=[tok(D_MODEL), tok(NSA_WIDTH), tok(RWKV_WIDTH), full(wa), full(wb), full(g), full(rw_hi),
                  full(rw_lo), full(rb)],
        out_specs=[tok(D_MODEL), tok(D_MODEL), tok(LANES), tok(LANES)],
        out_shape=[jax.ShapeDtypeStruct((t, D_MODEL), F32), jax.ShapeDtypeStruct((t, D_MODEL), F32),
                   jax.ShapeDtypeStruct((t, LANES), jnp.int32), jax.ShapeDtypeStruct((t, LANES), F32)],
        compiler_params=_cparams("parallel"),
        name="outproj_router",
    )(x, o_nsa, o_rwkv, wa, wb, g, rw_hi, rw_lo, rb)


MOE_BM = 256


def _route(top_e, bm):
    n_tok = top_e.shape[0]
    flat_e = top_e.reshape(-1)
    n_assign = flat_e.shape[0]
    onehot = (flat_e[:, None] == jnp.arange(N_EXPERTS, dtype=jnp.int32)[None, :]).astype(jnp.int32)
    cum = jnp.cumsum(onehot, axis=0)
    counts = cum[-1]
    pos = jnp.take_along_axis(cum, flat_e[:, None], axis=1)[:, 0] - 1
    padded = (counts + bm - 1) // bm * bm
    pad_end = jnp.cumsum(padded)
    dest = (pad_end - padded)[flat_e] + pos
    n_blocks = -(-n_assign // bm) + N_EXPERTS
    slot_tok = jnp.zeros((n_blocks * bm,), jnp.int32).at[dest].set(jnp.arange(n_assign, dtype=jnp.int32) // MOE_TOPK)
    n_used = (pad_end[-1] // bm).astype(jnp.int32)
    blk = jnp.minimum(jnp.arange(n_blocks, dtype=jnp.int32), n_used - 1)
    blk_e = jnp.minimum(jnp.searchsorted(pad_end, blk * bm, side="right"), N_EXPERTS - 1).astype(jnp.int32)
    return slot_tok, dest.reshape(n_tok, MOE_TOPK).astype(jnp.int32), blk_e, n_used.reshape(1), n_blocks


def _row_copy(src_ref, dst_ref, sem, src_row, dst_row):
    return pltpu.make_async_copy(src_ref.at[pl.ds(src_row, 1)], dst_ref.at[pl.ds(dst_row, 1)], sem)


def _gather_rows(idx_ref, src_ref, buf, sem):
    n = buf.shape[0]

    def issue(r, c):
        _row_copy(src_ref, buf, sem, idx_ref[0, 0, r], r).start()
        return c

    lax.fori_loop(0, n, issue, 0, unroll=8)
    pltpu.make_async_copy(src_ref.at[pl.ds(0, n)], buf, sem).wait()


def _moe_gather_kernel(tok_ref, h_ref, o_ref, buf, sem):
    _gather_rows(tok_ref, h_ref, buf, sem)
    o_ref[...] = buf[...].astype(o_ref.dtype)


def _moe_gather(slot_tok, h2, bm):
    n_slots = slot_tok.shape[0]
    nblk = n_slots // bm
    return pl.pallas_call(
        _moe_gather_kernel,
        grid=(nblk,),
        in_specs=[pl.BlockSpec((1, 1, bm), lambda b: (b, 0, 0), memory_space=pltpu.SMEM),
                  pl.BlockSpec(memory_space=pl.ANY)],
        out_specs=pl.BlockSpec((bm, D_MODEL), lambda b: (b, 0)),
        out_shape=jax.ShapeDtypeStruct((n_slots, D_MODEL), BF16),
        scratch_shapes=[pltpu.VMEM((bm, D_MODEL), F32), pltpu.SemaphoreType.DMA(())],
        compiler_params=_cparams("arbitrary"),
        name="moe_gather",
    )(slot_tok.reshape(nblk, 1, bm), h2)


MOE_FT = 512
MOE_NF = D_FF // MOE_FT


def _expert_changed(be_ref, b):
    prev = be_ref[jnp.maximum(b - 1, 0)]
    return (b == 0) | (be_ref[b] != prev)


def _moe_up_kernel(be_ref, nu_ref, x_ref, wg_ref, wl_ref, bg_ref, bl_ref, o_ref, wg_s, wl_s):
    b = pl.program_id(1)

    @pl.when(b < nu_ref[0])
    def _():
        @pl.when(_expert_changed(be_ref, b))
        def _():
            wg_s[...] = wg_ref[0].astype(BF16)
            wl_s[...] = wl_ref[0].astype(BF16)

        x = x_ref[...]
        glu = jnp.minimum(_dot(x, wg_s[...]) + bg_ref[0], SWIGLU_LIMIT)
        lin = jnp.clip(_dot(x, wl_s[...]) + bl_ref[0], -SWIGLU_LIMIT, SWIGLU_LIMIT)
        o_ref[...] = (glu * jax.nn.sigmoid(SWIGLU_ALPHA * glu) * (lin + 1.0)).astype(o_ref.dtype)

    @pl.when(b >= nu_ref[0])
    def _():
        o_ref[...] = jnp.zeros_like(o_ref)


def _moe_up(blk_e, n_used, xs, w1, b1, bm):
    n_slots = xs.shape[0]
    nblk = n_slots // bm
    live = lambda b, nu: jnp.minimum(b, nu[0] - 1)
    grid_spec = pltpu.PrefetchScalarGridSpec(
        num_scalar_prefetch=2,
        grid=(MOE_NF, nblk),
        in_specs=[
            pl.BlockSpec((bm, D_MODEL), lambda f, b, be, nu: (live(b, nu), 0)),
            pl.BlockSpec((1, D_MODEL, MOE_FT), lambda f, b, be, nu: (be[b], 0, f)),
            pl.BlockSpec((1, D_MODEL, MOE_FT), lambda f, b, be, nu: (be[b], 0, MOE_NF + f)),
            pl.BlockSpec((1, 1, MOE_FT), lambda f, b, be, nu: (be[b], 0, f)),
            pl.BlockSpec((1, 1, MOE_FT), lambda f, b, be, nu: (be[b], 0, MOE_NF + f)),
        ],
        out_specs=pl.BlockSpec((bm, MOE_FT), lambda f, b, be, nu: (b, f)),
        scratch_shapes=[pltpu.VMEM((D_MODEL, MOE_FT), BF16), pltpu.VMEM((D_MODEL, MOE_FT), BF16)],
    )
    return pl.pallas_call(
        _moe_up_kernel,
        grid_spec=grid_spec,
        out_shape=jax.ShapeDtypeStruct((n_slots, D_FF), BF16),
        compiler_params=_cparams("arbitrary", "arbitrary"),
        name="moe_up",
    )(blk_e, n_used, xs, w1, w1, b1, b1)


def _moe_down_kernel(be_ref, nu_ref, a_ref, w_ref, b_ref, o_ref, w_s):
    b = pl.program_id(0)

    @pl.when(b < nu_ref[0])
    def _():
        @pl.when(_expert_changed(be_ref, b))
        def _():
            w_s[...] = w_ref[0].astype(BF16)

        o_ref[...] = _dot(a_ref[...], w_s[...]) + b_ref[0]

    @pl.when(b >= nu_ref[0])
    def _():
        o_ref[...] = jnp.zeros_like(o_ref)


def _moe_down(blk_e, n_used, act, w2, b2, bm):
    n_slots = act.shape[0]
    nblk = n_slots // bm
    live = lambda b, nu: jnp.minimum(b, nu[0] - 1)
    grid_spec = pltpu.PrefetchScalarGridSpec(
        num_scalar_prefetch=2,
        grid=(nblk,),
        in_specs=[
            pl.BlockSpec((bm, D_FF), lambda b, be, nu: (live(b, nu), 0)),
            pl.BlockSpec((1, D_FF, D_MODEL), lambda b, be, nu: (be[b], 0, 0)),
            pl.BlockSpec((1, 1, D_MODEL), lambda b, be, nu: (be[b], 0, 0)),
        ],
        out_specs=pl.BlockSpec((bm, D_MODEL), lambda b, be, nu: (b, 0)),
        scratch_shapes=[pltpu.VMEM((D_FF, D_MODEL), BF16)],
    )
    return pl.pallas_call(
        _moe_down_kernel,
        grid_spec=grid_spec,
        out_shape=jax.ShapeDtypeStruct((n_slots, D_MODEL), F32),
        compiler_params=_cparams("arbitrary"),
        name="moe_down",
    )(blk_e, n_used, act, w2, b2)


def _combine_kernel(slot_ref, x_ref, tg_ref, g_ref, oh_ref, yb_ref, o_ref, buf, sem):
    tm = x_ref.shape[0]
    _gather_rows(slot_ref, yb_ref, buf, sem)
    x = x_ref[...]
    gates = tg_ref[...]
    for k in range(MOE_TOPK):
        gk = _dot_exact01(gates, oh_ref[k])
        x = x + jnp.concatenate([gk] * (D_MODEL // LANES), axis=1) * buf[k * tm:(k + 1) * tm]
    ms = jnp.mean(x * x, axis=-1, keepdims=True)
    o_ref[...] = x * lax.rsqrt(ms + NORM_EPS) * g_ref[...]


def _moe_combine(slot_of, x1, tg, g, yb, row0, t, tm=128):
    nt = t // tm
    blk0 = row0 // tm
    slots = slot_of[row0:row0 + t].reshape(nt, tm, MOE_TOPK).transpose(0, 2, 1).reshape(nt, 1, MOE_TOPK * tm)
    onehot = (jnp.arange(LANES)[None, :, None] == jnp.arange(MOE_TOPK)[:, None, None]).astype(BF16)
    onehot = jnp.broadcast_to(onehot, (MOE_TOPK, LANES, LANES))
    return pl.pallas_call(
        _combine_kernel,
        grid=(nt,),
        in_specs=[pl.BlockSpec((1, 1, MOE_TOPK * tm), lambda i: (i, 0, 0), memory_space=pltpu.SMEM),
                  pl.BlockSpec((tm, D_MODEL), lambda i: (blk0 + i, 0)),
                  pl.BlockSpec((tm, LANES), lambda i: (blk0 + i, 0)),
                  pl.BlockSpec((1, D_MODEL), lambda i: (0, 0)),
                  pl.BlockSpec((MOE_TOPK, LANES, LANES), lambda i: (0, 0, 0)),
                  pl.BlockSpec(memory_space=pl.ANY)],
        out_specs=pl.BlockSpec((tm, D_MODEL), lambda i: (i, 0)),
        out_shape=jax.ShapeDtypeStruct((t, D_MODEL), F32),
        scratch_shapes=[pltpu.VMEM((MOE_TOPK * tm, D_MODEL), F32), pltpu.SemaphoreType.DMA(())],
        compiler_params=_cparams("arbitrary"),
        name="moe_combine",
    )(slots, x1, tg, g.reshape(1, -1), onehot, yb)


def _head_perm():
    p, g, e, d = jnp.meshgrid(jnp.arange(2), jnp.arange(NSA_GROUP), jnp.arange(2), jnp.arange(HEAD_DIM),
                              indexing="ij")
    return (((2 * p + e) * NSA_GROUP + g) * HEAD_DIM + d).reshape(-1)


def _alibi_slopes():
    return 2.0 ** (-8.0 * jnp.arange(1, NSA_HEADS + 1, dtype=F32) / NSA_HEADS)


def _slope_rows(ne, tq, pair):
    g, e, t = jnp.meshgrid(jnp.arange(NSA_GROUP), jnp.arange(ne), jnp.arange(tq), indexing="ij")
    head = (pair * ne + e) * NSA_GROUP + g
    return _alibi_slopes()[head].reshape(-1, 1)


def _gate_expand():
    p, br, g, e, d = jnp.meshgrid(jnp.arange(2), jnp.arange(3), jnp.arange(NSA_GROUP), jnp.arange(2),
                                  jnp.arange(HEAD_DIM), indexing="ij")
    src = (((2 * p + e) * NSA_GROUP + g) * 3 + br).reshape(-1)
    return (jnp.arange(LANES)[:, None] == src[None, :]).astype(BF16)


def _pair01(n_cmp):
    return (jnp.arange(n_cmp)[:, None] // 2 == jnp.arange(LANES)[None, :]).astype(BF16)


def _block_ones(width):
    i = jnp.arange(width)
    return (i[:, None] // HEAD_DIM == i[None, :] // HEAD_DIM).astype(BF16)


def _cmp_weights_bd(wk, wv):
    eye = jnp.eye(2, dtype=wk.dtype)
    bd = lambda w: jnp.einsum("hg,bde->bhdge", eye, w).reshape(CMP_BLOCK, LANES, LANES)
    return jnp.stack([bd(wk), bd(wv)]).astype(BF16)


def _pad_cols(a, width):
    return jnp.pad(a, ((0, 0), (0, width - a.shape[1])))


def kernel(x_prompt, x_sample, cache_nsa_kv, state_win_kv, state_rwkv, state_shift, page_table, norm_attn, w_in,
           nsa_w_cmp_k, nsa_w_cmp_v, nsa_out_gain, rwkv_mu, rwkv_w0, rwkv_w2, rwkv_a0, rwkv_a2, rwkv_g2, rwkv_k_k,
           rwkv_k_a, rwkv_r_k, rwkv_ln_w, rwkv_ln_b, w_out, norm_ffn, router_w, router_b, moe_w1, moe_b1, moe_w2,
           moe_b2, norm_final):
    nb_p, seq_p, _ = x_prompt.shape
    nb_s, seq_s, _ = x_sample.shape
    depth = norm_attn.shape[0]
    assert depth == 1 and seq_s == 8 and seq_p % (2 * Q_BLOCK) == 0
    tp, ts = nb_p * seq_p, nb_s * seq_s
    n_phys = cache_nsa_kv.shape[1]
    w_len = state_win_kv.shape[2]
    l = 0
    x = jnp.concatenate([x_prompt.reshape(tp, D_MODEL), x_sample.reshape(ts, D_MODEL)], axis=0)
    t_all = tp + ts

    perm = _head_perm()
    w = w_in[l]
    c0, c1, c2 = NSA_WIDTH, NSA_WIDTH + 4 * KV_WIDTH, NSA_WIDTH + 6 * KV_WIDTH
    c3 = c2 + 3 * NSA_HEADS
    h = _rmsnorm(x, norm_attn[l], BF16)
    q = _matmul(h, (w[:, :c0] * (HEAD_DIM ** -0.5))[:, perm].astype(BF16), F32, 512, 512, "proj_q")
    kvn = _matmul(h, w[:, c0:c1].astype(BF16), F32, 512, 512, "proj_kv")
    winn = _matmul(h, w[:, c1:c2].astype(BF16), F32, 512, 512, "proj_win")
    gl = _matmul(h, _pad_cols(w[:, c2:c3], LANES).astype(BF16), F32, 512, LANES, "proj_gate")
    z = _matmul(h, _pad_cols(w[:, c3:], Z_PAD).astype(BF16), F32, 512, Z_PAD // 3, "proj_z")

    w_bd = _cmp_weights_bd(nsa_w_cmp_k[l], nsa_w_cmp_v[l])
    cache_rows = cache_nsa_kv[l].reshape(n_phys * PAGE_SIZE, 4 * KV_WIDTH)
    kvc_new = _compress(kvn, w_bd, tp // CMP_BLOCK, 256)
    blocks_phys = n_phys * (PAGE_SIZE // CMP_BLOCK)
    kvc_phys = _compress(cache_rows, w_bd, blocks_phys, 512)
    eg = _gate_expand()
    gain_p = nsa_out_gain[l][perm]
    o_p = _nsa_prompt(q, gl, kvn, winn, kvc_new, jnp.stack([_slope_rows(2, Q_BLOCK, p) for p in range(2)]),
                      eg.reshape(LANES, 2, 12 * LANES).transpose(1, 0, 2), gain_p.reshape(2, 1, 512),
                      _pair01(seq_p // CMP_BLOCK), _block_ones(LANES), nb_p, seq_p)
    o_s, new_win_s = _nsa_sample(
        page_table, q, gl, kvn, winn, state_win_kv[l].reshape(nb_s, w_len, 2 * KV_WIDTH),
        cache_rows,
        kvc_phys.reshape(n_phys, 1, (PAGE_SIZE // CMP_BLOCK) * 2 * KV_WIDTH),
        _slope_rows(NSA_KV_HEADS, 8, 0), eg, gain_p.reshape(1, NSA_WIDTH),
        _pair01(page_table.shape[1] * PAGE_SIZE // CMP_BLOCK), _block_ones(2 * LANES), tp)
    o_nsa = jnp.concatenate([o_p, o_s.astype(BF16)], axis=0)

    w2p, a2p, g2p, pat = _rwkv_consts(rwkv_w2[l], rwkv_a2[l], rwkv_g2[l])
    bo128 = _block_ones(LANES)
    mu = jnp.pad(rwkv_mu[l], (0, Z_PAD - SHIFT_WIDTH))
    vecs = (mu, rwkv_w0[l], rwkv_a0[l], rwkv_k_k[l], rwkv_k_a[l], rwkv_r_k[l], w2p, a2p, g2p, bo128)
    tm_p = 512
    nt_p = tp // tm_p
    last = z[tm_p - 1:tp:tm_p]
    prev = jnp.concatenate([jnp.zeros((1, Z_PAD), F32), last[:-1]], axis=0)
    prev = jnp.where((jnp.arange(nt_p) % (seq_p // tm_p) == 0)[:, None], 0.0, prev)
    first_p = jnp.zeros((nt_p, 8, Z_PAD), F32).at[:, 0].set(prev)
    first_s = jnp.zeros((nb_s, seq_s, Z_PAD), F32).at[:, 0, :SHIFT_WIDTH].set(state_shift[l]).reshape(ts, Z_PAD)
    prep_p = _rwkv_prep(z, first_p, 0, tp, tm_p, seq_p, *vecs)
    prep_s = _rwkv_prep(z, first_s, tp, ts, 256, seq_s, *vecs)
    bo256 = _block_ones(2 * LANES)
    y_p, st_p = _rwkv_scan([a.reshape(nb_p, seq_p, RWKV_WIDTH) for a in prep_p[:6]], None, nb_p, seq_p, nb_p, 128,
                           pat, bo256)
    y_s, st_s = _rwkv_scan([a.reshape(nb_s, seq_s, RWKV_WIDTH) for a in prep_s[:6]], _state_to_tiles(state_rwkv[l]),
                           nb_s, seq_s, 4, seq_s, pat, bo256)
    o_rw_p = _rwkv_post(y_p.reshape(tp, RWKV_WIDTH), prep_p[6], prep_p[7], rwkv_ln_w[l], rwkv_ln_b[l], bo128, 512)
    o_rw_s = _rwkv_post(y_s.reshape(ts, RWKV_WIDTH), prep_s[6], prep_s[7], rwkv_ln_w[l], rwkv_ln_b[l], bo128, 512)
    o_rwkv = jnp.concatenate([o_rw_p, o_rw_s], axis=0)

    wo = w_out[l]
    rw = _pad_cols(router_w[l], LANES)
    rw_hi = rw.astype(BF16)
    rw_lo = (rw - rw_hi.astype(F32)).astype(BF16)
    rb = jnp.concatenate([router_b[l].astype(F32), jnp.full((LANES - N_EXPERTS,), -1e30, F32)]).reshape(1, LANES)
    x1, h2, top_e, top_g = _outproj_router(x, o_nsa, o_rwkv, wo[:NSA_WIDTH][perm].astype(BF16),
                                           wo[NSA_WIDTH:].astype(BF16), norm_ffn[l].reshape(1, -1), rw_hi, rw_lo, rb)
    slot_tok, slot_of, blk_e, n_used, _ = _route(top_e[:, :MOE_TOPK], MOE_BM)
    xs = _moe_gather(slot_tok, h2, MOE_BM)
    act = _moe_up(blk_e, n_used, xs, moe_w1[l], moe_b1[l].reshape(N_EXPERTS, 1, 2 * D_FF), MOE_BM)
    yb = _moe_down(blk_e, n_used, act, moe_w2[l], moe_b2[l].reshape(N_EXPERTS, 1, D_MODEL), MOE_BM)
    y_p = _moe_combine(slot_of, x1, top_g, norm_final, yb, 0, tp)
    y_s = _moe_combine(slot_of, x1, top_g, norm_final, yb, tp, ts)

    hd = (NSA_KV_HEADS, HEAD_DIM)
    kv_p = kvn[:tp].reshape(1, nb_p, seq_p, 4, *hd)
    kv_s = kvn[tp:].reshape(1, nb_s, seq_s, 4, *hd)
    win_keep = min(WINDOW, seq_p)
    win_p = winn[:tp].reshape(nb_p, seq_p, 2, *hd)[None, :, seq_p - win_keep:]
    win_s = new_win_s.reshape(1, nb_s, w_len, 2, *hd)
    sh_p = z[seq_p - 1:tp:seq_p, :SHIFT_WIDTH][None]
    sh_s = z[tp + seq_s - 1::seq_s, :SHIFT_WIDTH][None]
    return (y_p.reshape(nb_p, seq_p, D_MODEL), y_s.reshape(nb_s, seq_s, D_MODEL), kv_p, kv_s, win_p, win_s,
            _tiles_to_state(st_p)[None], _tiles_to_state(st_s)[None], sh_p, sh_s)
```

```python
import functools

import jax
import jax.numpy as jnp
from jax import lax
from jax.experimental import pallas as pl
from jax.experimental.pallas import tpu as pltpu

F32 = jnp.float32
BF16 = jnp.bfloat16

D_MODEL = 2048
HEAD_DIM = 64
NSA_HEADS = 16
NSA_KV_HEADS = 4
NSA_GROUP = 4
NSA_WIDTH = 1024
KV_WIDTH = 256
CMP_BLOCK = 32
SEL_BLOCK = 64
SEL_TOPK = 16
WINDOW = 512
Q_BLOCK = 128
PAGE_SIZE = 128
RWKV_HEADS = 16
RWKV_WIDTH = 1024
DECAY_LORA = 64
ICLR_LORA = 64
GATE_LORA = 160
SHIFT_WIDTH = 3 * RWKV_WIDTH + DECAY_LORA + ICLR_LORA + GATE_LORA
Z_PAD = 3456
N_EXPERTS = 32
MOE_TOPK = 4
D_FF = 2048
SWIGLU_LIMIT = 7.0
SWIGLU_ALPHA = 1.702
NORM_EPS = 1e-5
GN_EPS = 64e-5
LANES = 128
VMEM_LIMIT = 56 * 1024 * 1024
NEG = -1e30


def _cparams(*sem):
    return pltpu.CompilerParams(dimension_semantics=sem, vmem_limit_bytes=VMEM_LIMIT)


def _dot(a, b):
    return jnp.dot(a, b, preferred_element_type=F32)


def _split3(x):
    hi = x.astype(BF16)
    r1 = x - hi.astype(F32)
    mid = r1.astype(BF16)
    lo = (r1 - mid.astype(F32)).astype(BF16)
    return hi, mid, lo


def _dot_exact01(x, m01):
    hi, mid, lo = _split3(x)
    return _dot(hi, m01) + _dot(mid, m01) + _dot(lo, m01)


def _rmsnorm_kernel(x_ref, g_ref, o_ref):
    x = x_ref[...]
    ms = jnp.mean(x * x, axis=-1, keepdims=True)
    o_ref[...] = (x * lax.rsqrt(ms + NORM_EPS) * g_ref[...]).astype(o_ref.dtype)


def _rmsnorm(x, g, out_dtype, tm=512):
    m, d = x.shape
    return pl.pallas_call(
        _rmsnorm_kernel,
        grid=(m // tm,),
        in_specs=[pl.BlockSpec((tm, d), lambda i: (i, 0)), pl.BlockSpec((1, d), lambda i: (0, 0))],
        out_specs=pl.BlockSpec((tm, d), lambda i: (i, 0)),
        out_shape=jax.ShapeDtypeStruct((m, d), out_dtype),
        compiler_params=_cparams("parallel"),
        name="rmsnorm",
    )(x, g.reshape(1, d))


def _mm_kernel(a_ref, b_ref, o_ref):
    o_ref[...] = _dot(a_ref[...], b_ref[...]).astype(o_ref.dtype)


def _matmul(a, b, out_dtype, tm, tn, name):
    m, k = a.shape
    n = b.shape[1]
    return pl.pallas_call(
        _mm_kernel,
        grid=(m // tm, n // tn),
        in_specs=[pl.BlockSpec((tm, k), lambda i, j: (i, 0)), pl.BlockSpec((k, tn), lambda i, j: (0, j))],
        out_specs=pl.BlockSpec((tm, tn), lambda i, j: (i, j)),
        out_shape=jax.ShapeDtypeStruct((m, n), out_dtype),
        compiler_params=_cparams("parallel", "arbitrary"),
        name=name,
    )(a, b)


def _compress_kernel(x_ref, w_ref, o_ref):
    nb = o_ref.shape[0]
    acc = jnp.zeros((nb, LANES), F32)
    for b in range(CMP_BLOCK):
        a = x_ref[pl.ds(b, nb, stride=CMP_BLOCK), :].astype(BF16)
        acc = acc + _dot(a, w_ref[0, b])
    o_ref[...] = acc


def _compress(rows, w_bd, n_blocks, nb):
    slabs = 2 * KV_WIDTH // LANES
    return pl.pallas_call(
        _compress_kernel,
        grid=(n_blocks // nb, slabs),
        in_specs=[
            pl.BlockSpec((nb * CMP_BLOCK, LANES), lambda i, s: (i, s)),
            pl.BlockSpec((1, CMP_BLOCK, LANES, LANES), lambda i, s: (s // 2, 0, 0, 0)),
        ],
        out_specs=pl.BlockSpec((nb, LANES), lambda i, s: (i, s)),
        out_shape=jax.ShapeDtypeStruct((n_blocks, 2 * KV_WIDTH), F32),
        compiler_params=_cparams("parallel", "arbitrary"),
        name="nsa_compress",
    )(rows, w_bd)


def _dot01_left(m01, x):
    hi, mid, lo = _split3(x)
    return _dot(m01, hi) + _dot(m01, mid) + _dot(m01, lo)


def _softmax0(s, mask):
    s = jnp.where(mask, s, -jnp.inf)
    m = jnp.max(s, axis=0, keepdims=True)
    m = jnp.where(jnp.isfinite(m), m, 0.0)
    e = jnp.exp(s - m)
    return e / jnp.maximum(jnp.sum(e, axis=0, keepdims=True), 1e-30)


def _select_blocks_t(imp, cur, n_rank):
    j = lax.broadcasted_iota(jnp.int32, (imp.shape[0], 1), 0)
    forced = (j == cur) | (j == 0)
    key = jnp.where(forced, jnp.inf, jnp.where(j < cur, imp, -jnp.inf))
    rank = jnp.zeros(key.shape, F32)
    for i in range(n_rank):
        ri = key[i:i + 1, :]
        ahead = (ri > key) | ((ri == key) & (j > i))
        rank = rank + jnp.where(ahead, 1.0, 0.0)
    return jnp.where(rank < SEL_TOPK, 1.0, 0.0)


def _dyn_row(x, r):
    rows = lax.broadcasted_iota(jnp.int32, (x.shape[0], 1), 0)
    return jnp.sum(jnp.where(rows == r, x, 0.0), axis=0, keepdims=True)


def _nsa_prompt_kernel(q_ref, gl_ref, ks_ref, vs_ref, kw_ref, vw_ref, kc_ref, vc_ref, slope_ref, gain_ref,
                       pair_ref, o_ref, m_ref, l_ref, acc_ref, b0_ref, *, seq):
    ne, tq = 2, Q_BLOCK
    ncol = NSA_GROUP * ne * tq
    cet = ne * tq
    n_cmp = seq // CMP_BLOCK
    tk = 2 * Q_BLOCK
    i = pl.program_id(1)
    p = pl.program_id(2)
    p0 = i * tq

    qt = q_ref[...].T
    sub = lax.broadcasted_iota(jnp.int32, (ne * HEAD_DIM, 1), 0)
    cols = []
    for g in range(NSA_GROUP):
        qg = qt[g * LANES:(g + 1) * LANES]
        for e in range(ne):
            cols.append(jnp.where((sub >= e * HEAD_DIM) & (sub < (e + 1) * HEAD_DIM), qg, 0.0))
    qbd = jnp.concatenate(cols, axis=1).astype(BF16)

    lane = lax.broadcasted_iota(jnp.int32, (1, ncol), 1)
    tloc = lane & (tq - 1)
    tpos = p0 + tloc
    slope = slope_ref[0]

    s = _dot(kc_ref[...].astype(BF16), qbd)
    cmp_end = (lax.broadcasted_iota(jnp.int32, (n_cmp, 1), 0) + 1) * CMP_BLOCK - 1
    dist = (tpos - cmp_end).astype(F32)
    pc = _softmax0(s - slope * dist, dist >= 0)
    o_c = _dot(vc_ref[...].T.astype(BF16), pc.astype(BF16))

    psum = pc[:, 0:cet]
    for g in range(1, NSA_GROUP):
        psum = psum + pc[:, g * cet:(g + 1) * cet]
    n_sel = seq // SEL_BLOCK
    imp = _dot01_left(pair_ref[...], psum)
    sel = _select_blocks_t(imp, jnp.right_shift(tpos[:, 0:cet], 6), n_sel).astype(BF16)

    ksub = lax.broadcasted_iota(jnp.int32, (tk, 1), 0)
    b0_ref[...] = slope * (tloc - ksub).astype(F32)
    m_ref[...] = jnp.full(m_ref.shape, NEG, F32)
    l_ref[...] = jnp.zeros(l_ref.shape, F32)
    acc_ref[...] = jnp.zeros(acc_ref.shape, F32)
    jrow = lax.broadcasted_iota(jnp.int32, (1, n_sel), 1)

    def tile(k0, causal):
        kt = ks_ref[pl.ds(k0, tk), :].astype(BF16)
        vt = vs_ref[pl.ds(k0, tk), :].T.astype(BF16)
        st = _dot(kt, qbd)
        expand = jnp.where(jnp.right_shift(k0 + ksub, 6) == jrow, 1.0, 0.0).astype(BF16)
        chosen = _dot(expand, sel) > 0.5
        off = (p0 - k0).astype(F32)
        if causal:
            chosen = chosen & ((tloc[:, 0:cet] - ksub + (p0 - k0)) >= 0)
        cb = jnp.where(chosen, 0.0, -jnp.inf)
        cb = jnp.concatenate([cb] * NSA_GROUP, axis=1)
        s2 = st - b0_ref[...] - slope * off + cb
        m_old = m_ref[...]
        m_new = jnp.maximum(m_old, jnp.max(s2, axis=0, keepdims=True))
        alpha = jnp.exp(m_old - m_new)
        pt = jnp.exp(s2 - m_new)
        l_ref[...] = alpha * l_ref[...] + jnp.sum(pt, axis=0, keepdims=True)
        acc_ref[...] = alpha * acc_ref[...] + _dot(vt, pt.astype(BF16))
        m_ref[...] = m_new

    n_past = lax.shift_right_logical(i, 1)

    def pair(t, carry):
        tile(pl.multiple_of(2 * t * tk, tk), False)
        tile(pl.multiple_of((2 * t + 1) * tk, tk), False)
        return carry

    def single(t, carry):
        tile(pl.multiple_of((n_past - 1) * tk, tk), False)
        return carry

    lax.fori_loop(0, lax.shift_right_logical(n_past, 1), pair, 0)
    lax.fori_loop(0, n_past & 1, single, 0)
    tile(pl.multiple_of(n_past * tk, tk), True)
    o_s = acc_ref[...] / jnp.maximum(l_ref[...], 1e-30)

    start = pl.multiple_of(jnp.maximum(p0 - WINDOW, 0), Q_BLOCK)
    wk = WINDOW + tq
    sw = _dot(kw_ref[pl.ds(start, wk), :].astype(BF16), qbd)
    kpos = start + lax.broadcasted_iota(jnp.int32, (wk, 1), 0)
    dw = tpos - kpos
    pw = _softmax0(sw - slope * dw.astype(F32), (dw >= 0) & (dw <= WINDOW))
    o_w = _dot(vw_ref[pl.ds(start, wk), :].T.astype(BF16), pw.astype(BF16))

    gate = jax.nn.sigmoid(gl_ref[...].T)
    for g in range(NSA_GROUP):
        halves = []
        for e in range(ne):
            c0 = (g * ne + e) * tq
            r0 = e * HEAD_DIM
            x = None
            for br, o in enumerate((o_c, o_s, o_w)):
                hrow = ((2 * p + e) * NSA_GROUP + g) * 3 + br
                gt = _dyn_row(gate, hrow)
                term = gt * o[r0:r0 + HEAD_DIM, c0:c0 + tq]
                x = term if x is None else x + term
            ms = jnp.mean(x * x, axis=0, keepdims=True)
            halves.append(x * lax.rsqrt(ms + NORM_EPS) * gain_ref[0, g * ne + e])
        o_ref[:, g * LANES:(g + 1) * LANES] = jnp.concatenate(halves, axis=0).T.astype(o_ref.dtype)


def _nsa_prompt(q, gl, kvn, winn, kvc, slopes, gain, pair01, n, seq):
    nq = seq // Q_BLOCK
    n_cmp = seq // CMP_BLOCK
    n_sel = seq // SEL_BLOCK
    ncol = NSA_GROUP * 2 * Q_BLOCK
    kern = functools.partial(_nsa_prompt_kernel, seq=seq)
    return pl.pallas_call(
        kern,
        grid=(n, nq, 2),
        in_specs=[
            pl.BlockSpec((Q_BLOCK, 512), lambda b, i, p: (b * nq + i, p)),
            pl.BlockSpec((Q_BLOCK, LANES), lambda b, i, p: (b * nq + i, 0)),
            pl.BlockSpec((seq, LANES), lambda b, i, p: (b, 4 + p)),
            pl.BlockSpec((seq, LANES), lambda b, i, p: (b, 6 + p)),
            pl.BlockSpec((seq, LANES), lambda b, i, p: (b, p)),
            pl.BlockSpec((seq, LANES), lambda b, i, p: (b, 2 + p)),
            pl.BlockSpec((n_cmp, LANES), lambda b, i, p: (b, p)),
            pl.BlockSpec((n_cmp, LANES), lambda b, i, p: (b, 2 + p)),
            pl.BlockSpec((1, 1, ncol), lambda b, i, p: (p, 0, 0)),
            pl.BlockSpec((1, 8, HEAD_DIM, LANES), lambda b, i, p: (p, 0, 0, 0)),
            pl.BlockSpec((n_sel, n_cmp), lambda b, i, p: (0, 0)),
        ],
        out_specs=pl.BlockSpec((Q_BLOCK, 512), lambda b, i, p: (b * nq + i, p)),
        out_shape=jax.ShapeDtypeStruct((n * seq, NSA_WIDTH), BF16),
        scratch_shapes=[pltpu.VMEM((1, ncol), F32), pltpu.VMEM((1, ncol), F32), pltpu.VMEM((LANES, ncol), F32),
                        pltpu.VMEM((2 * Q_BLOCK, ncol), F32)],
        compiler_params=_cparams("parallel", "parallel", "parallel"),
        name="nsa_prompt",
    )(q, gl, kvn, kvn, winn, winn, kvc, kvc, slopes, gain, pair01)


def _slope_cols(ne, tq, pair):
    g, e, t = jnp.meshgrid(jnp.arange(NSA_GROUP), jnp.arange(ne), jnp.arange(tq), indexing="ij")
    head = (pair * ne + e) * NSA_GROUP + g
    return _alibi_slopes()[head].reshape(1, -1)


def _gain_cols(gain):
    gh = gain.reshape(NSA_HEADS, HEAD_DIM)
    p, g, e = jnp.meshgrid(jnp.arange(2), jnp.arange(NSA_GROUP), jnp.arange(2), indexing="ij")
    head = ((2 * p + e) * NSA_GROUP + g).reshape(2, 8)
    return jnp.broadcast_to(gh[head][..., None], (2, 8, HEAD_DIM, LANES))


def _pair01_t(n_cmp):
    return (jnp.arange(n_cmp // 2)[:, None] == jnp.arange(n_cmp)[None, :] // 2).astype(BF16)


def _pad_rows(x, rows):
    return jnp.concatenate([x, jnp.zeros((rows - x.shape[0], x.shape[1]), x.dtype)], axis=0)


def _nsa_sample_kernel(pt_ref, q_ref, gl_ref, kvn_ref, winn_ref, wst_ref, *rest, past_len):
    del pt_ref
    n_pages = past_len // PAGE_SIZE
    page_refs = rest[:n_pages]
    kc_refs = rest[n_pages:2 * n_pages]
    (slope_ref, gain_ref, valid_ref, pair_ref, sumg_ref, tile_ref, gsel_ref,
     o_ref, nw_ref, kc_s, vc_s) = rest[2 * n_pages:]
    ne, tq = NSA_KV_HEADS, 8
    width = ne * HEAD_DIM
    ncol = NSA_GROUP * ne * tq
    blocks_per_page = PAGE_SIZE // CMP_BLOCK

    qf = q_ref[...]
    lane_w = lax.broadcasted_iota(jnp.int32, (1, width), 1)
    parts = []
    for g in range(NSA_GROUP):
        slab = jnp.concatenate([qf[:, g * LANES:(g + 1) * LANES],
                                qf[:, 4 * LANES + g * LANES:4 * LANES + (g + 1) * LANES]], axis=1)
        for e in range(ne):
            parts.append(jnp.where((lane_w >= e * HEAD_DIM) & (lane_w < (e + 1) * HEAD_DIM), slab, 0.0))
    qbd = jnp.concatenate(parts, axis=0).T.astype(BF16)

    lane = lax.broadcasted_iota(jnp.int32, (1, ncol), 1)
    tloc = lane & (tq - 1)
    tpos = past_len + tloc
    slope = slope_ref[...]

    kc_s[...] = jnp.zeros_like(kc_s)
    vc_s[...] = jnp.zeros_like(vc_s)
    for j in range(n_pages):
        for b in range(blocks_per_page):
            c = j * blocks_per_page + b
            kc_s[c:c + 1, :] = kc_refs[j][0, :, b * 2 * KV_WIDTH:b * 2 * KV_WIDTH + KV_WIDTH]
            vc_s[c:c + 1, :] = kc_refs[j][0, :, b * 2 * KV_WIDTH + KV_WIDTH:(b + 1) * 2 * KV_WIDTH]
    ncp = kc_s.shape[0]
    s = _dot(kc_s[...].astype(BF16), qbd)
    cmp_end = (lax.broadcasted_iota(jnp.int32, (ncp, 1), 0) + 1) * CMP_BLOCK - 1
    dist = (tpos - cmp_end).astype(F32)
    pc = _softmax0(s - slope * dist, dist >= 0)
    o_c = _dot(vc_s[...].T.astype(BF16), pc.astype(BF16))

    psum = _dot_exact01(pc, sumg_ref[...])
    imp = _dot01_left(pair_ref[...], psum)
    n_sel_past = past_len // SEL_BLOCK
    sel = _select_blocks_t(imp, jnp.right_shift(tpos, 6), n_sel_past + 1).astype(BF16)
    sel = _dot(sel, tile_ref[...]).astype(BF16)

    ksub = lax.broadcasted_iota(jnp.int32, (PAGE_SIZE, 1), 0)
    jrow = lax.broadcasted_iota(jnp.int32, (1, sel.shape[0]), 1)
    b0 = slope * (tloc - ksub).astype(F32)

    def scores(kt, k0, causal):
        st = _dot(kt.astype(BF16), qbd)
        expand = jnp.where(jnp.right_shift(k0 + ksub, 6) == jrow, 1.0, 0.0).astype(BF16)
        chosen = _dot(expand, sel) > 0.5
        if causal:
            chosen = chosen & ((tloc - ksub + (past_len - k0)) >= 0)
        return st - b0 - slope * float(past_len - k0) + jnp.where(chosen, 0.0, -jnp.inf)

    knew = _pad_rows(kvn_ref[:, 2 * KV_WIDTH:3 * KV_WIDTH], PAGE_SIZE)
    vnew = _pad_rows(kvn_ref[:, 3 * KV_WIDTH:4 * KV_WIDTH], PAGE_SIZE)
    s_all = [scores(page_refs[j][:, 0:KV_WIDTH], j * PAGE_SIZE, False) for j in range(n_pages)]
    s_all.append(scores(knew, past_len, True))
    m = s_all[0]
    for sj in s_all[1:]:
        m = jnp.maximum(m, sj)
    m = jnp.maximum(jnp.max(m, axis=0, keepdims=True), NEG)
    l = jnp.zeros((1, ncol), F32)
    acc = jnp.zeros((width, ncol), F32)
    for j, sj in enumerate(s_all):
        pt = jnp.exp(sj - m)
        vt = page_refs[j][:, KV_WIDTH:2 * KV_WIDTH] if j < n_pages else vnew
        l = l + jnp.sum(pt, axis=0, keepdims=True)
        acc = acc + _dot(vt.T.astype(BF16), pt.astype(BF16))
    o_s = acc / jnp.maximum(l, 1e-30)

    kw = jnp.concatenate([wst_ref[0, :, 0:KV_WIDTH], _pad_rows(winn_ref[:, 0:KV_WIDTH], PAGE_SIZE)], axis=0)
    vw = jnp.concatenate([wst_ref[0, :, KV_WIDTH:], _pad_rows(winn_ref[:, KV_WIDTH:], PAGE_SIZE)], axis=0)
    start = past_len - WINDOW
    sw = _dot(kw.astype(BF16), qbd)
    kpos = start + lax.broadcasted_iota(jnp.int32, (kw.shape[0], 1), 0)
    dw = tpos - kpos
    pw = _softmax0(sw - slope * dw.astype(F32), (dw >= 0) & (dw <= WINDOW))
    o_w = _dot(vw.T.astype(BF16), pw.astype(BF16))

    sg = jax.nn.sigmoid(gl_ref[...])
    trow = lax.broadcasted_iota(jnp.int32, (tq, 1), 0)
    x = None
    for br, o in enumerate((o_c, o_s, o_w)):
        spread = _dot_exact01(sg, gsel_ref[br])
        grow = jnp.sum(jnp.where(trow == tloc, spread, 0.0), axis=0, keepdims=True)
        x = grow * o if x is None else x + grow * o
    x = jnp.where(valid_ref[...] > 0.5, x, 0.0)
    ms = jnp.sum(x * x, axis=0, keepdims=True) * (1.0 / HEAD_DIM)
    y = (x * lax.rsqrt(ms + NORM_EPS) * gain_ref[...]).T
    for g in range(NSA_GROUP):
        og = y[g * ne * tq:g * ne * tq + tq]
        for e in range(1, ne):
            og = og + y[(g * ne + e) * tq:(g * ne + e + 1) * tq]
        o_ref[:, g * LANES:(g + 1) * LANES] = og[:, 0:LANES]
        o_ref[:, (NSA_GROUP + g) * LANES:(NSA_GROUP + g + 1) * LANES] = og[:, LANES:]

    w_len = wst_ref.shape[1]
    nw_ref[0, 0:w_len - tq, :] = wst_ref[0, tq:w_len, :]
    nw_ref[0, w_len - tq:w_len, :] = winn_ref[...]


def _sample_consts(gain):
    ne, tq = NSA_KV_HEADS, 8
    c = jnp.arange(NSA_GROUP * ne * tq)
    cg, ck = c // (ne * tq), (c // tq) % ne
    chead = ck * NSA_GROUP + cg
    r = jnp.arange(ne * HEAD_DIM)
    rk, rd = r // HEAD_DIM, r % HEAD_DIM
    slope = _alibi_slopes()[chead].reshape(1, -1)
    valid = (rk[:, None] == ck[None, :]).astype(F32)
    gain_t = gain.reshape(NSA_HEADS, HEAD_DIM)[chead[None, :], rd[:, None]] * valid
    lanes = jnp.arange(LANES)
    sumg = ((lanes[None, :] < ne * tq) & (c[:, None] % (ne * tq) == lanes[None, :])).astype(BF16)
    tile = ((lanes[:, None] < ne * tq) & (lanes[:, None] == c[None, :] % (ne * tq))).astype(BF16)
    gsel = jnp.stack([(lanes[:, None] == (chead * 3 + br)[None, :]).astype(BF16) for br in range(3)])
    return slope, gain_t, valid, sumg, tile, gsel


def _nsa_sample(page_table, q, gl, kvn, winn, win_state, cache_rows, kvc_pages, consts, pair01, row0):
    slope, gain_t, valid, sumg, tile, gsel = consts
    n_seq, n_pages = page_table.shape
    past_len = n_pages * PAGE_SIZE
    w_len = win_state.shape[1]
    blk0 = row0 // 8
    tok = lambda s, pt: (blk0 + s, 0)
    full = lambda a: pl.BlockSpec(a.shape, lambda s, pt: (0,) * a.ndim)
    in_specs = [
        pl.BlockSpec((8, NSA_WIDTH), tok),
        pl.BlockSpec((8, LANES), tok),
        pl.BlockSpec((8, 4 * KV_WIDTH), tok),
        pl.BlockSpec((8, 2 * KV_WIDTH), tok),
        pl.BlockSpec((1, w_len, 2 * KV_WIDTH), lambda s, pt: (s, 0, 0)),
    ]
    in_specs += [pl.BlockSpec((PAGE_SIZE, 2 * KV_WIDTH), functools.partial(lambda s, pt, j: (pt[s, j], 1), j=j))
                 for j in range(n_pages)]
    in_specs += [pl.BlockSpec((1, 1, 8 * KV_WIDTH), functools.partial(lambda s, pt, j: (pt[s, j], 0, 0), j=j))
                 for j in range(n_pages)]
    in_specs += [full(slope), full(gain_t), full(valid), full(pair01), full(sumg), full(tile), full(gsel)]
    ncp = -(-(past_len // CMP_BLOCK) // LANES) * LANES
    grid_spec = pltpu.PrefetchScalarGridSpec(
        num_scalar_prefetch=1,
        grid=(n_seq,),
        in_specs=in_specs,
        out_specs=[pl.BlockSpec((8, NSA_WIDTH), lambda s, pt: (s, 0)),
                   pl.BlockSpec((1, w_len, 2 * KV_WIDTH), lambda s, pt: (s, 0, 0))],
        scratch_shapes=[pltpu.VMEM((ncp, KV_WIDTH), F32), pltpu.VMEM((ncp, KV_WIDTH), F32)],
    )
    return pl.pallas_call(
        functools.partial(_nsa_sample_kernel, past_len=past_len),
        grid_spec=grid_spec,
        out_shape=[jax.ShapeDtypeStruct((n_seq * 8, NSA_WIDTH), F32),
                   jax.ShapeDtypeStruct((n_seq, w_len, 2 * KV_WIDTH), F32)],
        compiler_params=_cparams("parallel"),
        name="nsa_sample",
    )(page_table, q, gl, kvn, winn, win_state, *([cache_rows] * n_pages), *([kvc_pages] * n_pages),
      slope, gain_t, valid, pair01, sumg, tile, gsel)


def _head_sums(x, bo01):
    return jnp.concatenate([_dot_exact01(x[:, s * LANES:(s + 1) * LANES], bo01)
                            for s in range(x.shape[1] // LANES)], axis=1)


def _rwkv_prep_kernel(z_ref, first_ref, mu_ref, w0_ref, a0_ref, kk_ref, ka_ref, rk_ref, w2_ref, a2_ref, g2_ref,
                      bo_ref, r_o, w_o, k_o, v_o, kk_o, nb_o, g_o, bonus_o, *, seq_rows):
    z = z_ref[...]
    tm = z.shape[0]
    row = lax.broadcasted_iota(jnp.int32, (tm, 1), 0)
    if seq_rows >= tm:
        prev = jnp.where(row == 0, first_ref[0, 0:1, :], pltpu.roll(z, 1, 0))
    else:
        prev = jnp.where((row & (seq_rows - 1)) == 0, first_ref[...], pltpu.roll(z, 1, 0))
    zm = z + (prev - z) * mu_ref[...]
    w = RWKV_WIDTH
    r, k, v = zm[:, 0:w], zm[:, w:2 * w], zm[:, 2 * w:3 * w]
    wa = zm[:, 3 * w:3 * w + LANES]
    gd = zm[:, 3 * w + LANES:]
    wl = w0_ref[...] + _dot(jnp.tanh(wa).astype(BF16), w2_ref[...])
    neg = -wl
    softplus = jnp.maximum(neg, 0.0) + jnp.log1p(jnp.exp(-jnp.abs(neg)))
    decay = jnp.exp(-jnp.exp(-softplus - 0.5))
    a = jax.nn.sigmoid(a0_ref[...] + _dot(wa.astype(BF16), a2_ref[...]))
    g = _dot(jax.nn.sigmoid(gd).astype(BF16), g2_ref[...])
    bo = bo_ref[...]
    kk = k * kk_ref[...]
    kk = kk / jnp.maximum(jnp.sqrt(_head_sums(kk * kk, bo)), 1e-12)
    k2 = k * (1.0 + (a - 1.0) * ka_ref[...])
    r_o[...] = r
    w_o[...] = decay
    k_o[...] = k2
    v_o[...] = v
    kk_o[...] = kk
    nb_o[...] = -(kk * a)
    g_o[...] = g
    bonus_o[...] = _head_sums(r * k2 * rk_ref[...], bo) * v


def _rwkv_prep(z, first, row0, rows, tm, seq_rows, mu, w0, a0, k_k, k_a, r_k, w2p, a2p, g2p, bo01):
    blk0 = row0 // tm
    vec = lambda a: a.reshape(1, -1)
    cvec = pl.BlockSpec((1, RWKV_WIDTH), lambda i: (0, 0))
    if seq_rows >= tm:
        first_spec = pl.BlockSpec((1, 8, Z_PAD), lambda i: (i, 0, 0))
    else:
        first_spec = pl.BlockSpec((tm, Z_PAD), lambda i: (i, 0))
    out = jax.ShapeDtypeStruct((rows, RWKV_WIDTH), F32)
    ospec = pl.BlockSpec((tm, RWKV_WIDTH), lambda i: (i, 0))
    return pl.pallas_call(
        functools.partial(_rwkv_prep_kernel, seq_rows=seq_rows),
        grid=(rows // tm,),
        in_specs=[pl.BlockSpec((tm, Z_PAD), lambda i: (blk0 + i, 0)), first_spec,
                  pl.BlockSpec((1, Z_PAD), lambda i: (0, 0)), cvec, cvec, cvec, cvec, cvec,
                  pl.BlockSpec((LANES, RWKV_WIDTH), lambda i: (0, 0)),
                  pl.BlockSpec((LANES, RWKV_WIDTH), lambda i: (0, 0)),
                  pl.BlockSpec((2 * LANES, RWKV_WIDTH), lambda i: (0, 0)),
                  pl.BlockSpec((LANES, LANES), lambda i: (0, 0))],
        out_specs=[ospec] * 8,
        out_shape=[out] * 8,
        compiler_params=_cparams("parallel"),
        name="rwkv_prep",
    )(z, first, vec(mu), vec(w0), vec(a0), vec(k_k), vec(k_a), vec(r_k), w2p, a2p, g2p, bo01)


def _rwkv_scan_kernel(*refs, nbatch, tc, has_init):
    ops = refs[:6]
    pos = 6
    s0_ref = refs[pos] if has_init else None
    pos += 1 if has_init else 0
    pat_ref, bo_ref, y_ref, s_out, st = refs[pos:pos + 5]
    c = pl.program_id(1)
    nq = RWKV_HEADS // 4
    tw = 4 * HEAD_DIM

    @pl.when(c == 0)
    def _():
        if has_init:
            st[...] = s0_ref[...]
        else:
            st[...] = jnp.zeros_like(st)

    pat = pat_ref[...]
    bo = bo_ref[...]

    def step(t, carry):
        tiles = [(b, q) for b in range(nbatch) for q in range(nq)]
        rowv = lambda k, b, q: ops[k][b, pl.ds(t, 1), q * tw:(q + 1) * tw]
        lhs = []
        for b, q in tiles:
            s = st[b, q]
            lhs.append((s * rowv(4, b, q)).astype(BF16))
            lhs.append((pat * rowv(3, b, q)).astype(BF16))
        red = _dot(jnp.concatenate(lhs, axis=0), bo)
        outs = []
        for n, (b, q) in enumerate(tiles):
            sa = red[n * 2 * HEAD_DIM:n * 2 * HEAD_DIM + HEAD_DIM]
            vb = red[n * 2 * HEAD_DIM + HEAD_DIM:(n + 1) * 2 * HEAD_DIM]
            s2 = st[b, q] * rowv(1, b, q) + sa * rowv(5, b, q) + vb * rowv(2, b, q)
            st[b, q] = s2
            outs.append((s2 * rowv(0, b, q)).astype(BF16))
        yb = _dot(jnp.concatenate(outs, axis=0), bo)
        for n, (b, q) in enumerate(tiles):
            yrow = jnp.sum(yb[n * HEAD_DIM:(n + 1) * HEAD_DIM] * pat, axis=0, keepdims=True)
            y_ref[b, pl.ds(t, 1), q * tw:(q + 1) * tw] = yrow
        return carry

    lax.fori_loop(0, tc, step, 0, unroll=2)

    @pl.when(c == pl.num_programs(1) - 1)
    def _():
        s_out[...] = st[...]


def _rwkv_scan(ops, s0, nseq, seq_rows, nbatch, tc, pat, bo01):
    nq = RWKV_HEADS // 4
    tok_spec = pl.BlockSpec((nbatch, tc, RWKV_WIDTH), lambda gi, c: (gi, c, 0))
    st_spec = pl.BlockSpec((nbatch, nq, HEAD_DIM, 4 * HEAD_DIM), lambda gi, c: (gi, 0, 0, 0))
    in_specs = [tok_spec] * 6
    args = list(ops)
    if s0 is not None:
        in_specs.append(st_spec)
        args.append(s0)
    in_specs += [pl.BlockSpec((HEAD_DIM, 4 * HEAD_DIM), lambda gi, c: (0, 0)),
                 pl.BlockSpec((4 * HEAD_DIM, 4 * HEAD_DIM), lambda gi, c: (0, 0))]
    args += [pat, bo01]
    return pl.pallas_call(
        functools.partial(_rwkv_scan_kernel, nbatch=nbatch, tc=tc, has_init=s0 is not None),
        grid=(nseq // nbatch, seq_rows // tc),
        in_specs=in_specs,
        out_specs=[tok_spec, st_spec],
        out_shape=[jax.ShapeDtypeStruct((nseq, seq_rows, RWKV_WIDTH), F32),
                   jax.ShapeDtypeStruct((nseq, nq, HEAD_DIM, 4 * HEAD_DIM), F32)],
        scratch_shapes=[pltpu.VMEM((nbatch, nq, HEAD_DIM, 4 * HEAD_DIM), F32)],
        compiler_params=_cparams("parallel", "arbitrary"),
        name="rwkv_scan",
    )(*args)


def _rwkv_post_kernel(y_ref, g_ref, bonus_ref, lnw_ref, lnb_ref, bo_ref, o_ref):
    y = y_ref[...]
    bo = bo_ref[...]
    d = y - _head_sums(y, bo) * (1.0 / HEAD_DIM)
    var = _head_sums(d * d, bo) * (1.0 / HEAD_DIM)
    o = (d * lax.rsqrt(var + GN_EPS) * lnw_ref[...] + lnb_ref[...] + bonus_ref[...]) * g_ref[...]
    o_ref[...] = o.astype(o_ref.dtype)


def _rwkv_post(y, g, bonus, ln_w, ln_b, bo01, tm):
    rows = y.shape[0]
    tok = pl.BlockSpec((tm, RWKV_WIDTH), lambda i: (i, 0))
    cvec = pl.BlockSpec((1, RWKV_WIDTH), lambda i: (0, 0))
    return pl.pallas_call(
        _rwkv_post_kernel,
        grid=(rows // tm,),
        in_specs=[tok, tok, tok, cvec, cvec, pl.BlockSpec((LANES, LANES), lambda i: (0, 0))],
        out_specs=tok,
        out_shape=jax.ShapeDtypeStruct((rows, RWKV_WIDTH), BF16),
        compiler_params=_cparams("parallel"),
        name="rwkv_post",
    )(y, g, bonus, ln_w.reshape(1, -1), ln_b.reshape(1, -1), bo01)


def _rwkv_consts(w2, a2, g2):
    w2p = jnp.concatenate([w2, jnp.zeros((LANES - DECAY_LORA, RWKV_WIDTH), w2.dtype)], axis=0)
    a2p = jnp.concatenate([jnp.zeros((DECAY_LORA, RWKV_WIDTH), a2.dtype), a2], axis=0)
    g2p = jnp.concatenate([g2, jnp.zeros((2 * LANES - GATE_LORA, RWKV_WIDTH), g2.dtype)], axis=0)
    v = jnp.arange(HEAD_DIM)
    pat = (v[:, None] == (jnp.arange(4 * HEAD_DIM)[None, :] % HEAD_DIM)).astype(F32)
    return w2p.astype(BF16), a2p.astype(BF16), g2p.astype(BF16), pat


def _state_to_tiles(s):
    n = s.shape[0]
    return s.reshape(n, 4, 4, HEAD_DIM, HEAD_DIM).transpose(0, 1, 3, 2, 4).reshape(n, 4, HEAD_DIM, 4 * HEAD_DIM)


def _tiles_to_state(t):
    n = t.shape[0]
    return t.reshape(n, 4, HEAD_DIM, 4, HEAD_DIM).transpose(0, 1, 3, 2, 4).reshape(n, RWKV_HEADS, HEAD_DIM, HEAD_DIM)


def _outproj_kernel(x_ref, on_ref, or_ref, wa_ref, wb_ref, g_ref, rwh_ref, rwl_ref, rb_ref,
                    x1_ref, h2_ref, te_ref, tg_ref):
    x1 = x_ref[...] + _dot(on_ref[...], wa_ref[...]) + _dot(or_ref[...], wb_ref[...])
    x1_ref[...] = x1
    ms = jnp.mean(x1 * x1, axis=-1, keepdims=True)
    hf = x1 * lax.rsqrt(ms + NORM_EPS) * g_ref[...]
    h2_ref[...] = hf
    hh = hf.astype(BF16)
    hl = (hf - hh.astype(F32)).astype(BF16)
    logits = _dot(hh, rwh_ref[...]) + _dot(hl, rwh_ref[...]) + _dot(hh, rwl_ref[...]) + rb_ref[...]
    lane = lax.broadcasted_iota(jnp.int32, logits.shape, 1).astype(F32)
    vals, idxs = [], []
    for _ in range(MOE_TOPK):
        m = jnp.max(logits, axis=-1, keepdims=True)
        idx = jnp.min(jnp.where(logits == m, lane, float(LANES)), axis=-1, keepdims=True)
        vals.append(m)
        idxs.append(idx)
        logits = jnp.where(lane == idx, -jnp.inf, logits)
    es = [jnp.exp(v - vals[0]) for v in vals]
    denom = es[0] + es[1] + es[2] + es[3]
    te = jnp.zeros(logits.shape, F32)
    tg = jnp.zeros(logits.shape, F32)
    for k in range(MOE_TOPK):
        te = jnp.where(lane == float(k), idxs[k], te)
        tg = jnp.where(lane == float(k), es[k] / denom, tg)
    te_ref[...] = te.astype(jnp.int32)
    tg_ref[...] = tg


def _outproj_router(x, o_nsa, o_rwkv, wa, wb, g, rw_hi, rw_lo, rb, tm=256):
    t = x.shape[0]
    tok = lambda w: pl.BlockSpec((tm, w), lambda i: (i, 0))
    full = lambda a: pl.BlockSpec(a.shape, lambda i: (0,) * a.ndim)
    return pl.pallas_call(
        _outproj_kernel,
        grid=(t // tm,),
        in_specs=[tok(D_MODEL), tok(NSA_WIDTH), tok(RWKV_WIDTH), full(wa), full(wb), full(g), full(rw_hi),
                  full(rw_lo), full(rb)],
        out_specs=[tok(D_MODEL), tok(D_MODEL), tok(LANES), tok(LANES)],
        out_shape=[jax.ShapeDtypeStruct((t, D_MODEL), F32), jax.ShapeDtypeStruct((t, D_MODEL), F32),
                   jax.ShapeDtypeStruct((t, LANES), jnp.int32), jax.ShapeDtypeStruct((t, LANES), F32)],
        compiler_params=_cparams("parallel"),
        name="outproj_router",
    )(x, o_nsa, o_rwkv, wa, wb, g, rw_hi, rw_lo, rb)


MOE_BM = 256


def _route(top_e, bm):
    n_tok = top_e.shape[0]
    flat_e = top_e.reshape(-1)
    n_assign = flat_e.shape[0]
    onehot = (flat_e[:, None] == jnp.arange(N_EXPERTS, dtype=jnp.int32)[None, :]).astype(jnp.int32)
    cum = jnp.cumsum(onehot, axis=0)
    counts = cum[-1]
    pos = jnp.take_along_axis(cum, flat_e[:, None], axis=1)[:, 0] - 1
    padded = (counts + bm - 1) // bm * bm
    pad_end = jnp.cumsum(padded)
    dest = (pad_end - padded)[flat_e] + pos
    n_blocks = -(-n_assign // bm) + N_EXPERTS
    slot_tok = jnp.zeros((n_blocks * bm,), jnp.int32).at[dest].set(jnp.arange(n_assign, dtype=jnp.int32) // MOE_TOPK)
    n_used = (pad_end[-1] // bm).astype(jnp.int32)
    blk = jnp.minimum(jnp.arange(n_blocks, dtype=jnp.int32), n_used - 1)
    blk_e = jnp.minimum(jnp.searchsorted(pad_end, blk * bm, side="right"), N_EXPERTS - 1).astype(jnp.int32)
    return slot_tok, dest.reshape(n_tok, MOE_TOPK).astype(jnp.int32), blk_e, n_used.reshape(1), n_blocks


def _row_copy(src_ref, dst_ref, sem, src_row, dst_row):
    return pltpu.make_async_copy(src_ref.at[pl.ds(src_row, 1)], dst_ref.at[pl.ds(dst_row, 1)], sem)


def _gather_rows(idx_ref, src_ref, buf, sem):
    n = buf.shape[0]

    def issue(r2, c):
        for par in range(2):
            r = 2 * r2 + par
            _row_copy(src_ref, buf, sem, idx_ref[0, 0, r], r).start(priority=par)
        return c

    lax.fori_loop(0, n // 2, issue, 0, unroll=4)
    pltpu.make_async_copy(src_ref.at[pl.ds(0, n)], buf, sem).wait()


def _moe_gather_kernel(tok_ref, h_ref, o_ref, buf, sem):
    _gather_rows(tok_ref, h_ref, buf, sem)
    o_ref[...] = buf[...].astype(o_ref.dtype)


def _moe_gather(slot_tok, h2, bm):
    n_slots = slot_tok.shape[0]
    nblk = n_slots // bm
    return pl.pallas_call(
        _moe_gather_kernel,
        grid=(nblk,),
        in_specs=[pl.BlockSpec((1, 1, bm), lambda b: (b, 0, 0), memory_space=pltpu.SMEM),
                  pl.BlockSpec(memory_space=pl.ANY)],
        out_specs=pl.BlockSpec((bm, D_MODEL), lambda b: (b, 0)),
        out_shape=jax.ShapeDtypeStruct((n_slots, D_MODEL), BF16),
        scratch_shapes=[pltpu.VMEM((bm, D_MODEL), F32), pltpu.SemaphoreType.DMA(())],
        compiler_params=_cparams("arbitrary"),
        name="moe_gather",
    )(slot_tok.reshape(nblk, 1, bm), h2)


MOE_FT = 1024
MOE_NF = D_FF // MOE_FT


def _expert_changed(be_ref, b):
    prev = be_ref[jnp.maximum(b - 1, 0)]
    return (b == 0) | (be_ref[b] != prev)


def _moe_up_kernel(be_ref, nu_ref, x_ref, wg_ref, wl_ref, bg_ref, bl_ref, o_ref, wg_s, wl_s):
    b = pl.program_id(1)

    @pl.when(b < nu_ref[0])
    def _():
        @pl.when(_expert_changed(be_ref, b))
        def _():
            wg_s[...] = wg_ref[0].astype(BF16)
            wl_s[...] = wl_ref[0].astype(BF16)

        x = x_ref[...]
        glu = jnp.minimum(_dot(x, wg_s[...]) + bg_ref[0], SWIGLU_LIMIT)
        lin = jnp.clip(_dot(x, wl_s[...]) + bl_ref[0], -SWIGLU_LIMIT, SWIGLU_LIMIT)
        o_ref[...] = (glu * jax.nn.sigmoid(SWIGLU_ALPHA * glu) * (lin + 1.0)).astype(o_ref.dtype)

    @pl.when(b >= nu_ref[0])
    def _():
        o_ref[...] = jnp.zeros_like(o_ref)


def _moe_up(blk_e, n_used, xs, w1, b1, bm):
    n_slots = xs.shape[0]
    nblk = n_slots // bm
    live = lambda b, nu: jnp.minimum(b, nu[0] - 1)
    grid_spec = pltpu.PrefetchScalarGridSpec(
        num_scalar_prefetch=2,
        grid=(MOE_NF, nblk),
        in_specs=[
            pl.BlockSpec((bm, D_MODEL), lambda f, b, be, nu: (live(b, nu), 0)),
            pl.BlockSpec((1, D_MODEL, MOE_FT), lambda f, b, be, nu: (be[b], 0, f)),
            pl.BlockSpec((1, D_MODEL, MOE_FT), lambda f, b, be, nu: (be[b], 0, MOE_NF + f)),
            pl.BlockSpec((1, 1, MOE_FT), lambda f, b, be, nu: (be[b], 0, f)),
            pl.BlockSpec((1, 1, MOE_FT), lambda f, b, be, nu: (be[b], 0, MOE_NF + f)),
        ],
        out_specs=pl.BlockSpec((bm, MOE_FT), lambda f, b, be, nu: (b, f)),
        scratch_shapes=[pltpu.VMEM((D_MODEL, MOE_FT), BF16), pltpu.VMEM((D_MODEL, MOE_FT), BF16)],
    )
    return pl.pallas_call(
        _moe_up_kernel,
        grid_spec=grid_spec,
        out_shape=jax.ShapeDtypeStruct((n_slots, D_FF), BF16),
        compiler_params=_cparams("arbitrary", "arbitrary"),
        name="moe_up",
    )(blk_e, n_used, xs, w1, w1, b1, b1)


def _moe_down_kernel(be_ref, nu_ref, a_ref, w_ref, b_ref, o_ref, w_s):
    b = pl.program_id(0)

    @pl.when(b < nu_ref[0])
    def _():
        @pl.when(_expert_changed(be_ref, b))
        def _():
            w_s[...] = w_ref[0].astype(BF16)

        o_ref[...] = _dot(a_ref[...], w_s[...]) + b_ref[0]

    @pl.when(b >= nu_ref[0])
    def _():
        o_ref[...] = jnp.zeros_like(o_ref)


def _moe_down(blk_e, n_used, act, w2, b2, bm):
    n_slots = act.shape[0]
    nblk = n_slots // bm
    live = lambda b, nu: jnp.minimum(b, nu[0] - 1)
    grid_spec = pltpu.PrefetchScalarGridSpec(
        num_scalar_prefetch=2,
        grid=(nblk,),
        in_specs=[
            pl.BlockSpec((bm, D_FF), lambda b, be, nu: (live(b, nu), 0)),
            pl.BlockSpec((1, D_FF, D_MODEL), lambda b, be, nu: (be[b], 0, 0)),
            pl.BlockSpec((1, 1, D_MODEL), lambda b, be, nu: (be[b], 0, 0)),
        ],
        out_specs=pl.BlockSpec((bm, D_MODEL), lambda b, be, nu: (b, 0)),
        scratch_shapes=[pltpu.VMEM((D_FF, D_MODEL), BF16)],
    )
    return pl.pallas_call(
        _moe_down_kernel,
        grid_spec=grid_spec,
        out_shape=jax.ShapeDtypeStruct((n_slots, D_MODEL), F32),
        compiler_params=_cparams("arbitrary"),
        name="moe_down",
    )(blk_e, n_used, act, w2, b2)


def _combine_kernel(slot_ref, x_ref, tg_ref, g_ref, oh_ref, yb_ref, o_ref, buf, sem):
    tm = x_ref.shape[0]
    _gather_rows(slot_ref, yb_ref, buf, sem)
    x = x_ref[...]
    gates = tg_ref[...]
    for k in range(MOE_TOPK):
        gk = _dot_exact01(gates, oh_ref[k])
        x = x + jnp.concatenate([gk] * (D_MODEL // LANES), axis=1) * buf[k * tm:(k + 1) * tm]
    ms = jnp.mean(x * x, axis=-1, keepdims=True)
    o_ref[...] = x * lax.rsqrt(ms + NORM_EPS) * g_ref[...]


def _moe_combine(slot_of, x1, tg, g, yb, row0, t, tm=128):
    nt = t // tm
    blk0 = row0 // tm
    slots = slot_of[row0:row0 + t].reshape(nt, tm, MOE_TOPK).transpose(0, 2, 1).reshape(nt, 1, MOE_TOPK * tm)
    onehot = (jnp.arange(LANES)[None, :, None] == jnp.arange(MOE_TOPK)[:, None, None]).astype(BF16)
    onehot = jnp.broadcast_to(onehot, (MOE_TOPK, LANES, LANES))
    return pl.pallas_call(
        _combine_kernel,
        grid=(nt,),
        in_specs=[pl.BlockSpec((1, 1, MOE_TOPK * tm), lambda i: (i, 0, 0), memory_space=pltpu.SMEM),
                  pl.BlockSpec((tm, D_MODEL), lambda i: (blk0 + i, 0)),
                  pl.BlockSpec((tm, LANES), lambda i: (blk0 + i, 0)),
                  pl.BlockSpec((1, D_MODEL), lambda i: (0, 0)),
                  pl.BlockSpec((MOE_TOPK, LANES, LANES), lambda i: (0, 0, 0)),
                  pl.BlockSpec(memory_space=pl.ANY)],
        out_specs=pl.BlockSpec((tm, D_MODEL), lambda i: (i, 0)),
        out_shape=jax.ShapeDtypeStruct((t, D_MODEL), F32),
        scratch_shapes=[pltpu.VMEM((MOE_TOPK * tm, D_MODEL), F32), pltpu.SemaphoreType.DMA(())],
        compiler_params=_cparams("arbitrary"),
        name="moe_combine",
    )(slots, x1, tg, g.reshape(1, -1), onehot, yb)


def _head_perm():
    p, g, e, d = jnp.meshgrid(jnp.arange(2), jnp.arange(NSA_GROUP), jnp.arange(2), jnp.arange(HEAD_DIM),
                              indexing="ij")
    return (((2 * p + e) * NSA_GROUP + g) * HEAD_DIM + d).reshape(-1)


def _alibi_slopes():
    return 2.0 ** (-8.0 * jnp.arange(1, NSA_HEADS + 1, dtype=F32) / NSA_HEADS)


def _block_ones(width):
    i = jnp.arange(width)
    return (i[:, None] // HEAD_DIM == i[None, :] // HEAD_DIM).astype(BF16)


def _cmp_weights_bd(wk, wv):
    eye = jnp.eye(2, dtype=wk.dtype)
    bd = lambda w: jnp.einsum("hg,bde->bhdge", eye, w).reshape(CMP_BLOCK, LANES, LANES)
    return jnp.stack([bd(wk), bd(wv)]).astype(BF16)


def _pad_cols(a, width):
    return jnp.pad(a, ((0, 0), (0, width - a.shape[1])))


def kernel(x_prompt, x_sample, cache_nsa_kv, state_win_kv, state_rwkv, state_shift, page_table, norm_attn, w_in,
           nsa_w_cmp_k, nsa_w_cmp_v, nsa_out_gain, rwkv_mu, rwkv_w0, rwkv_w2, rwkv_a0, rwkv_a2, rwkv_g2, rwkv_k_k,
           rwkv_k_a, rwkv_r_k, rwkv_ln_w, rwkv_ln_b, w_out, norm_ffn, router_w, router_b, moe_w1, moe_b1, moe_w2,
           moe_b2, norm_final):
    nb_p, seq_p, _ = x_prompt.shape
    nb_s, seq_s, _ = x_sample.shape
    depth = norm_attn.shape[0]
    assert depth == 1 and seq_s == 8 and seq_p % (2 * Q_BLOCK) == 0
    tp, ts = nb_p * seq_p, nb_s * seq_s
    n_phys = cache_nsa_kv.shape[1]
    w_len = state_win_kv.shape[2]
    l = 0
    x = jnp.concatenate([x_prompt.reshape(tp, D_MODEL), x_sample.reshape(ts, D_MODEL)], axis=0)

    perm = _head_perm()
    w = w_in[l]
    c0, c1, c2 = NSA_WIDTH, NSA_WIDTH + 4 * KV_WIDTH, NSA_WIDTH + 6 * KV_WIDTH
    c3 = c2 + 3 * NSA_HEADS
    h = _rmsnorm(x, norm_attn[l], BF16)
    q = _matmul(h, (w[:, :c0] * (HEAD_DIM ** -0.5))[:, perm].astype(BF16), F32, 512, 512, "proj_q")
    kvn = _matmul(h, w[:, c0:c1].astype(BF16), F32, 512, 512, "proj_kv")
    winn = _matmul(h, w[:, c1:c2].astype(BF16), F32, 512, 512, "proj_win")
    gl = _matmul(h, _pad_cols(w[:, c2:c3], LANES).astype(BF16), F32, 512, LANES, "proj_gate")
    z = _matmul(h, _pad_cols(w[:, c3:], Z_PAD).astype(BF16), F32, 512, Z_PAD // 3, "proj_z")

    w_bd = _cmp_weights_bd(nsa_w_cmp_k[l], nsa_w_cmp_v[l])
    cache_rows = cache_nsa_kv[l].reshape(n_phys * PAGE_SIZE, 4 * KV_WIDTH)
    kvc_new = _compress(kvn, w_bd, tp // CMP_BLOCK, 256)
    blocks_phys = n_phys * (PAGE_SIZE // CMP_BLOCK)
    kvc_phys = _compress(cache_rows, w_bd, blocks_phys, 512)
    o_p = _nsa_prompt(q, gl, kvn, winn, kvc_new, jnp.stack([_slope_cols(2, Q_BLOCK, p) for p in range(2)]),
                      _gain_cols(nsa_out_gain[l]), _pair01_t(seq_p // CMP_BLOCK), nb_p, seq_p)
    n_cmp_pad = -(-(page_table.shape[1] * PAGE_SIZE // CMP_BLOCK) // LANES) * LANES
    o_s, new_win_s = _nsa_sample(
        page_table, q, gl, kvn, winn, state_win_kv[l].reshape(nb_s, w_len, 2 * KV_WIDTH), cache_rows,
        kvc_phys.reshape(n_phys, 1, (PAGE_SIZE // CMP_BLOCK) * 2 * KV_WIDTH),
        _sample_consts(nsa_out_gain[l]), _pair01_t(n_cmp_pad), tp)
    o_nsa = jnp.concatenate([o_p, o_s.astype(BF16)], axis=0)

    w2p, a2p, g2p, pat = _rwkv_consts(rwkv_w2[l], rwkv_a2[l], rwkv_g2[l])
    bo128 = _block_ones(LANES)
    mu = jnp.pad(rwkv_mu[l], (0, Z_PAD - SHIFT_WIDTH))
    vecs = (mu, rwkv_w0[l], rwkv_a0[l], rwkv_k_k[l], rwkv_k_a[l], rwkv_r_k[l], w2p, a2p, g2p, bo128)
    tm_p = 512
    nt_p = tp // tm_p
    last = z[tm_p - 1:tp:tm_p]
    prev = jnp.concatenate([jnp.zeros((1, Z_PAD), F32), last[:-1]], axis=0)
    prev = jnp.where((jnp.arange(nt_p) % (seq_p // tm_p) == 0)[:, None], 0.0, prev)
    first_p = jnp.zeros((nt_p, 8, Z_PAD), F32).at[:, 0].set(prev)
    first_s = jnp.zeros((nb_s, seq_s, Z_PAD), F32).at[:, 0, :SHIFT_WIDTH].set(state_shift[l]).reshape(ts, Z_PAD)
    prep_p = _rwkv_prep(z, first_p, 0, tp, tm_p, seq_p, *vecs)
    prep_s = _rwkv_prep(z, first_s, tp, ts, 256, seq_s, *vecs)
    bo256 = _block_ones(2 * LANES)
    y_p, st_p = _rwkv_scan([a.reshape(nb_p, seq_p, RWKV_WIDTH) for a in prep_p[:6]], None, nb_p, seq_p, nb_p, 128,
                           pat, bo256)
    y_s, st_s = _rwkv_scan([a.reshape(nb_s, seq_s, RWKV_WIDTH) for a in prep_s[:6]], _state_to_tiles(state_rwkv[l]),
                           nb_s, seq_s, 4, seq_s, pat, bo256)
    o_rw_p = _rwkv_post(y_p.reshape(tp, RWKV_WIDTH), prep_p[6], prep_p[7], rwkv_ln_w[l], rwkv_ln_b[l], bo128, 512)
    o_rw_s = _rwkv_post(y_s.reshape(ts, RWKV_WIDTH), prep_s[6], prep_s[7], rwkv_ln_w[l], rwkv_ln_b[l], bo128, 512)
    o_rwkv = jnp.concatenate([o_rw_p, o_rw_s], axis=0)

    wo = w_out[l]
    rw = _pad_cols(router_w[l], LANES)
    rw_hi = rw.astype(BF16)
    rw_lo = (rw - rw_hi.astype(F32)).astype(BF16)
    rb = jnp.concatenate([router_b[l].astype(F32), jnp.full((LANES - N_EXPERTS,), -1e30, F32)]).reshape(1, LANES)
    x1, h2, top_e, top_g = _outproj_router(x, o_nsa, o_rwkv, wo[:NSA_WIDTH][perm].astype(BF16),
                                           wo[NSA_WIDTH:].astype(BF16), norm_ffn[l].reshape(1, -1), rw_hi, rw_lo, rb)
    slot_tok, slot_of, blk_e, n_used, _ = _route(top_e[:, :MOE_TOPK], MOE_BM)
    xs = _moe_gather(slot_tok, h2, MOE_BM)
    act = _moe_up(blk_e, n_used, xs, moe_w1[l], moe_b1[l].reshape(N_EXPERTS, 1, 2 * D_FF), MOE_BM)
    yb = _moe_down(blk_e, n_used, act, moe_w2[l], moe_b2[l].reshape(N_EXPERTS, 1, D_MODEL), MOE_BM)
    y_p = _moe_combine(slot_of, x1, top_g, norm_final, yb, 0, tp)
    y_s = _moe_combine(slot_of, x1, top_g, norm_final, yb, tp, ts)

    hd = (NSA_KV_HEADS, HEAD_DIM)
    kv_p = kvn[:tp].reshape(1, nb_p, seq_p, 4, *hd)
    kv_s = kvn[tp:].reshape(1, nb_s, seq_s, 4, *hd)
    win_keep = min(WINDOW, seq_p)
    win_p = winn[:tp].reshape(nb_p, seq_p, 2, *hd)[None, :, seq_p - win_keep:]
    win_s = new_win_s.reshape(1, nb_s, w_len, 2, *hd)
    sh_p = z[seq_p - 1:tp:seq_p, :SHIFT_WIDTH][None]
    sh_s = z[tp + seq_s - 1::seq_s, :SHIFT_WIDTH][None]
    return (y_p.reshape(nb_p, seq_p, D_MODEL), y_s.reshape(nb_s, seq_s, D_MODEL), kv_p, kv_s, win_p, win_s,
            _tiles_to_state(st_p)[None], _tiles_to_state(st_s)[None], sh_p, sh_s)
```

```python
import functools

import jax
import jax.numpy as jnp
from jax import lax
from jax.experimental import pallas as pl
from jax.experimental.pallas import tpu as pltpu

F32 = jnp.float32
BF16 = jnp.bfloat16

D_MODEL = 2048
HEAD_DIM = 64
NSA_HEADS = 16
NSA_KV_HEADS = 4
NSA_GROUP = 4
NSA_WIDTH = 1024
KV_WIDTH = 256
CMP_BLOCK = 32
SEL_BLOCK = 64
SEL_TOPK = 16
WINDOW = 512
Q_BLOCK = 128
PAGE_SIZE = 128
RWKV_HEADS = 16
RWKV_WIDTH = 1024
DECAY_LORA = 64
ICLR_LORA = 64
GATE_LORA = 160
SHIFT_WIDTH = 3 * RWKV_WIDTH + DECAY_LORA + ICLR_LORA + GATE_LORA
Z_PAD = 3456
N_EXPERTS = 32
MOE_TOPK = 4
D_FF = 2048
SWIGLU_LIMIT = 7.0
SWIGLU_ALPHA = 1.702
NORM_EPS = 1e-5
GN_EPS = 64e-5
LANES = 128
VMEM_LIMIT = 56 * 1024 * 1024
NEG = -1e30


def _cparams(*sem):
    return pltpu.CompilerParams(dimension_semantics=sem, vmem_limit_bytes=VMEM_LIMIT)


def _dot(a, b):
    return jnp.dot(a, b, preferred_element_type=F32)


def _split3(x):
    hi = x.astype(BF16)
    r1 = x - hi.astype(F32)
    mid = r1.astype(BF16)
    lo = (r1 - mid.astype(F32)).astype(BF16)
    return hi, mid, lo


def _dot_exact01(x, m01):
    hi, mid, lo = _split3(x)
    return _dot(hi, m01) + _dot(mid, m01) + _dot(lo, m01)


def _rmsnorm_kernel(x_ref, g_ref, o_ref):
    x = x_ref[...]
    ms = jnp.mean(x * x, axis=-1, keepdims=True)
    o_ref[...] = (x * lax.rsqrt(ms + NORM_EPS) * g_ref[...]).astype(o_ref.dtype)


def _rmsnorm(x, g, out_dtype, tm=512):
    m, d = x.shape
    return pl.pallas_call(
        _rmsnorm_kernel,
        grid=(m // tm,),
        in_specs=[pl.BlockSpec((tm, d), lambda i: (i, 0)), pl.BlockSpec((1, d), lambda i: (0, 0))],
        out_specs=pl.BlockSpec((tm, d), lambda i: (i, 0)),
        out_shape=jax.ShapeDtypeStruct((m, d), out_dtype),
        compiler_params=_cparams("parallel"),
        name="rmsnorm",
    )(x, g.reshape(1, d))


def _mm_kernel(a_ref, b_ref, o_ref):
    o_ref[...] = _dot(a_ref[...], b_ref[...]).astype(o_ref.dtype)


def _matmul(a, b, out_dtype, tm, tn, name):
    m, k = a.shape
    n = b.shape[1]
    return pl.pallas_call(
        _mm_kernel,
        grid=(m // tm, n // tn),
        in_specs=[pl.BlockSpec((tm, k), lambda i, j: (i, 0)), pl.BlockSpec((k, tn), lambda i, j: (0, j))],
        out_specs=pl.BlockSpec((tm, tn), lambda i, j: (i, j)),
        out_shape=jax.ShapeDtypeStruct((m, n), out_dtype),
        compiler_params=_cparams("parallel", "arbitrary"),
        name=name,
    )(a, b)


def _compress_kernel(x_ref, w_ref, o_ref):
    nb = o_ref.shape[0]
    acc = jnp.zeros((nb, LANES), F32)
    for b in range(CMP_BLOCK):
        a = x_ref[pl.ds(b, nb, stride=CMP_BLOCK), :].astype(BF16)
        acc = acc + _dot(a, w_ref[0, b])
    o_ref[...] = acc


def _compress(rows, w_bd, n_blocks, nb):
    slabs = 2 * KV_WIDTH // LANES
    return pl.pallas_call(
        _compress_kernel,
        grid=(n_blocks // nb, slabs),
        in_specs=[
            pl.BlockSpec((nb * CMP_BLOCK, LANES), lambda i, s: (i, s)),
            pl.BlockSpec((1, CMP_BLOCK, LANES, LANES), lambda i, s: (s // 2, 0, 0, 0)),
        ],
        out_specs=pl.BlockSpec((nb, LANES), lambda i, s: (i, s)),
        out_shape=jax.ShapeDtypeStruct((n_blocks, 2 * KV_WIDTH), F32),
        compiler_params=_cparams("parallel", "arbitrary"),
        name="nsa_compress",
    )(rows, w_bd)


def _dot01_left(m01, x):
    hi, mid, lo = _split3(x)
    return _dot(m01, hi) + _dot(m01, mid) + _dot(m01, lo)


def _softmax0(s, mask):
    s = jnp.where(mask, s, -jnp.inf)
    m = jnp.max(s, axis=0, keepdims=True)
    m = jnp.where(jnp.isfinite(m), m, 0.0)
    e = jnp.exp(s - m)
    return e / jnp.maximum(jnp.sum(e, axis=0, keepdims=True), 1e-30)


def _select_blocks_t(imp, cur, n_rank):
    j = lax.broadcasted_iota(jnp.int32, (imp.shape[0], 1), 0)
    forced = (j == cur) | (j == 0)
    key = jnp.where(forced, jnp.inf, jnp.where(j < cur, imp, -jnp.inf))
    rank = jnp.zeros(key.shape, F32)
    for i in range(n_rank):
        ri = key[i:i + 1, :]
        ahead = (ri > key) | ((ri == key) & (j > i))
        rank = rank + jnp.where(ahead, 1.0, 0.0)
    return jnp.where(rank < SEL_TOPK, 1.0, 0.0)


def _dyn_row(x, r):
    rows = lax.broadcasted_iota(jnp.int32, (x.shape[0], 1), 0)
    return jnp.sum(jnp.where(rows == r, x, 0.0), axis=0, keepdims=True)


def _nsa_prompt_kernel(q_ref, gl_ref, ks_ref, vs_ref, kw_ref, vw_ref, kc_ref, vc_ref, slope_ref, gain_ref,
                       pair_ref, o_ref, m_ref, l_ref, acc_ref, b0_ref, *, seq):
    ne, tq = 2, Q_BLOCK
    ncol = NSA_GROUP * ne * tq
    cet = ne * tq
    n_cmp = seq // CMP_BLOCK
    tk = 2 * Q_BLOCK
    i = pl.program_id(1)
    p = pl.program_id(2)
    p0 = i * tq

    qt = q_ref[...].T
    sub = lax.broadcasted_iota(jnp.int32, (ne * HEAD_DIM, 1), 0)
    cols = []
    for g in range(NSA_GROUP):
        qg = qt[g * LANES:(g + 1) * LANES]
        for e in range(ne):
            cols.append(jnp.where((sub >= e * HEAD_DIM) & (sub < (e + 1) * HEAD_DIM), qg, 0.0))
    qbd = jnp.concatenate(cols, axis=1).astype(BF16)

    lane = lax.broadcasted_iota(jnp.int32, (1, ncol), 1)
    tloc = lane & (tq - 1)
    tpos = p0 + tloc
    slope = slope_ref[0]

    s = _dot(kc_ref[...].astype(BF16), qbd)
    cmp_end = (lax.broadcasted_iota(jnp.int32, (n_cmp, 1), 0) + 1) * CMP_BLOCK - 1
    dist = (tpos - cmp_end).astype(F32)
    pc = _softmax0(s - slope * dist, dist >= 0)
    o_c = _dot(vc_ref[...].T.astype(BF16), pc.astype(BF16))

    psum = pc[:, 0:cet]
    for g in range(1, NSA_GROUP):
        psum = psum + pc[:, g * cet:(g + 1) * cet]
    n_sel = seq // SEL_BLOCK
    imp = _dot01_left(pair_ref[...], psum)
    sel = _select_blocks_t(imp, jnp.right_shift(tpos[:, 0:cet], 6), n_sel).astype(BF16)

    ksub = lax.broadcasted_iota(jnp.int32, (tk, 1), 0)
    b0_ref[...] = slope * (tloc - ksub).astype(F32)
    m_ref[...] = jnp.full(m_ref.shape, NEG, F32)
    l_ref[...] = jnp.zeros(l_ref.shape, F32)
    acc_ref[...] = jnp.zeros(acc_ref.shape, F32)
    jrow = lax.broadcasted_iota(jnp.int32, (1, n_sel), 1)

    def tile(k0, causal):
        kt = ks_ref[pl.ds(k0, tk), :].astype(BF16)
        vt = vs_ref[pl.ds(k0, tk), :].T.astype(BF16)
        st = _dot(kt, qbd)
        expand = jnp.where(jnp.right_shift(k0 + ksub, 6) == jrow, 1.0, 0.0).astype(BF16)
        chosen = _dot(expand, sel) > 0.5
        off = (p0 - k0).astype(F32)
        if causal:
            chosen = chosen & ((tloc[:, 0:cet] - ksub + (p0 - k0)) >= 0)
        cb = jnp.where(chosen, 0.0, -jnp.inf)
        cb = jnp.concatenate([cb] * NSA_GROUP, axis=1)
        s2 = st - b0_ref[...] - slope * off + cb
        m_old = m_ref[...]
        m_new = jnp.maximum(m_old, jnp.max(s2, axis=0, keepdims=True))
        alpha = jnp.exp(m_old - m_new)
        pt = jnp.exp(s2 - m_new)
        l_ref[...] = alpha * l_ref[...] + jnp.sum(pt, axis=0, keepdims=True)
        acc_ref[...] = alpha * acc_ref[...] + _dot(vt, pt.astype(BF16))
        m_ref[...] = m_new

    n_past = lax.shift_right_logical(i, 1)

    def pair(t, carry):
        tile(pl.multiple_of(2 * t * tk, tk), False)
        tile(pl.multiple_of((2 * t + 1) * tk, tk), False)
        return carry

    def single(t, carry):
        tile(pl.multiple_of((n_past - 1) * tk, tk), False)
        return carry

    lax.fori_loop(0, lax.shift_right_logical(n_past, 1), pair, 0)
    lax.fori_loop(0, n_past & 1, single, 0)
    tile(pl.multiple_of(n_past * tk, tk), True)
    o_s = acc_ref[...] / jnp.maximum(l_ref[...], 1e-30)

    start = pl.multiple_of(jnp.maximum(p0 - WINDOW, 0), Q_BLOCK)
    wk = WINDOW + tq
    sw = _dot(kw_ref[pl.ds(start, wk), :].astype(BF16), qbd)
    kpos = start + lax.broadcasted_iota(jnp.int32, (wk, 1), 0)
    dw = tpos - kpos
    pw = _softmax0(sw - slope * dw.astype(F32), (dw >= 0) & (dw <= WINDOW))
    o_w = _dot(vw_ref[pl.ds(start, wk), :].T.astype(BF16), pw.astype(BF16))

    gate = jax.nn.sigmoid(gl_ref[...].T)
    for g in range(NSA_GROUP):
        halves = []
        for e in range(ne):
            c0 = (g * ne + e) * tq
            r0 = e * HEAD_DIM
            x = None
            for br, o in enumerate((o_c, o_s, o_w)):
                hrow = ((2 * p + e) * NSA_GROUP + g) * 3 + br
                gt = _dyn_row(gate, hrow)
                term = gt * o[r0:r0 + HEAD_DIM, c0:c0 + tq]
                x = term if x is None else x + term
            ms = jnp.mean(x * x, axis=0, keepdims=True)
            halves.append(x * lax.rsqrt(ms + NORM_EPS) * gain_ref[0, g * ne + e])
        o_ref[:, g * LANES:(g + 1) * LANES] = jnp.concatenate(halves, axis=0).T.astype(o_ref.dtype)


def _nsa_prompt(q, gl, kvn, winn, kvc, slopes, gain, pair01, n, seq):
    nq = seq // Q_BLOCK
    n_cmp = seq // CMP_BLOCK
    n_sel = seq // SEL_BLOCK
    ncol = NSA_GROUP * 2 * Q_BLOCK
    kern = functools.partial(_nsa_prompt_kernel, seq=seq)
    return pl.pallas_call(
        kern,
        grid=(n, nq, 2),
        in_specs=[
            pl.BlockSpec((Q_BLOCK, 512), lambda b, i, p: (b * nq + i, p)),
            pl.BlockSpec((Q_BLOCK, LANES), lambda b, i, p: (b * nq + i, 0)),
            pl.BlockSpec((seq, LANES), lambda b, i, p: (b, 4 + p)),
            pl.BlockSpec((seq, LANES), lambda b, i, p: (b, 6 + p)),
            pl.BlockSpec((seq, LANES), lambda b, i, p: (b, p)),
            pl.BlockSpec((seq, LANES), lambda b, i, p: (b, 2 + p)),
            pl.BlockSpec((n_cmp, LANES), lambda b, i, p: (b, p)),
            pl.BlockSpec((n_cmp, LANES), lambda b, i, p: (b, 2 + p)),
            pl.BlockSpec((1, 1, ncol), lambda b, i, p: (p, 0, 0)),
            pl.BlockSpec((1, 8, HEAD_DIM, LANES), lambda b, i, p: (p, 0, 0, 0)),
            pl.BlockSpec((n_sel, n_cmp), lambda b, i, p: (0, 0)),
        ],
        out_specs=pl.BlockSpec((Q_BLOCK, 512), lambda b, i, p: (b * nq + i, p)),
        out_shape=jax.ShapeDtypeStruct((n * seq, NSA_WIDTH), BF16),
        scratch_shapes=[pltpu.VMEM((1, ncol), F32), pltpu.VMEM((1, ncol), F32), pltpu.VMEM((LANES, ncol), F32),
                        pltpu.VMEM((2 * Q_BLOCK, ncol), F32)],
        compiler_params=_cparams("parallel", "parallel", "parallel"),
        name="nsa_prompt",
    )(q, gl, kvn, kvn, winn, winn, kvc, kvc, slopes, gain, pair01)


def _slope_cols(ne, tq, pair):
    g, e, t = jnp.meshgrid(jnp.arange(NSA_GROUP), jnp.arange(ne), jnp.arange(tq), indexing="ij")
    head = (pair * ne + e) * NSA_GROUP + g
    return _alibi_slopes()[head].reshape(1, -1)


def _gain_cols(gain):
    gh = gain.reshape(2, 2, NSA_GROUP, HEAD_DIM).transpose(0, 2, 1, 3).reshape(2, 8, HEAD_DIM)
    return jnp.broadcast_to(gh[..., None], (2, 8, HEAD_DIM, LANES))


def _pair01_t(n_cmp):
    return (jnp.arange(n_cmp // 2)[:, None] == jnp.arange(n_cmp)[None, :] // 2).astype(BF16)


def _pad_rows(x, rows):
    return jnp.concatenate([x, jnp.zeros((rows - x.shape[0], x.shape[1]), x.dtype)], axis=0)


def _nsa_sample_kernel(pt_ref, q_ref, gl_ref, kvn_ref, winn_ref, wst_ref, *rest, past_len):
    del pt_ref
    n_pages = past_len // PAGE_SIZE
    page_refs = rest[:n_pages]
    kc_refs = rest[n_pages:2 * n_pages]
    (slope_ref, gain_ref, valid_ref, pair_ref, sumg_ref, tile_ref, gsel_ref,
     o_ref, nw_ref, kc_s, vc_s) = rest[2 * n_pages:]
    ne, tq = NSA_KV_HEADS, 8
    width = ne * HEAD_DIM
    ncol = NSA_GROUP * ne * tq
    blocks_per_page = PAGE_SIZE // CMP_BLOCK

    qf = q_ref[...]
    lane_w = lax.broadcasted_iota(jnp.int32, (1, width), 1)
    parts = []
    for g in range(NSA_GROUP):
        slab = jnp.concatenate([qf[:, g * LANES:(g + 1) * LANES],
                                qf[:, 4 * LANES + g * LANES:4 * LANES + (g + 1) * LANES]], axis=1)
        for e in range(ne):
            parts.append(jnp.where((lane_w >= e * HEAD_DIM) & (lane_w < (e + 1) * HEAD_DIM), slab, 0.0))
    qbd = jnp.concatenate(parts, axis=0).T.astype(BF16)

    lane = lax.broadcasted_iota(jnp.int32, (1, ncol), 1)
    tloc = lane & (tq - 1)
    tpos = past_len + tloc
    slope = slope_ref[...]

    kc_s[...] = jnp.zeros_like(kc_s)
    vc_s[...] = jnp.zeros_like(vc_s)
    for j in range(n_pages):
        for b in range(blocks_per_page):
            c = j * blocks_per_page + b
            kc_s[c:c + 1, :] = kc_refs[j][0, :, b * 2 * KV_WIDTH:b * 2 * KV_WIDTH + KV_WIDTH]
            vc_s[c:c + 1, :] = kc_refs[j][0, :, b * 2 * KV_WIDTH + KV_WIDTH:(b + 1) * 2 * KV_WIDTH]
    ncp = kc_s.shape[0]
    s = _dot(kc_s[...].astype(BF16), qbd)
    cmp_end = (lax.broadcasted_iota(jnp.int32, (ncp, 1), 0) + 1) * CMP_BLOCK - 1
    dist = (tpos - cmp_end).astype(F32)
    pc = _softmax0(s - slope * dist, dist >= 0)
    o_c = _dot(vc_s[...].T.astype(BF16), pc.astype(BF16))

    psum = _dot_exact01(pc, sumg_ref[...])
    imp = _dot01_left(pair_ref[...], psum)
    n_sel_past = past_len // SEL_BLOCK
    sel = _select_blocks_t(imp, jnp.right_shift(tpos, 6), n_sel_past + 1).astype(BF16)
    sel = _dot(sel, tile_ref[...]).astype(BF16)

    ksub = lax.broadcasted_iota(jnp.int32, (PAGE_SIZE, 1), 0)
    jrow = lax.broadcasted_iota(jnp.int32, (1, sel.shape[0]), 1)
    b0 = slope * (tloc - ksub).astype(F32)

    def scores(kt, k0, causal):
        st = _dot(kt.astype(BF16), qbd)
        expand = jnp.where(jnp.right_shift(k0 + ksub, 6) == jrow, 1.0, 0.0).astype(BF16)
        chosen = _dot(expand, sel) > 0.5
        if causal:
            chosen = chosen & ((tloc - ksub + (past_len - k0)) >= 0)
        return st - b0 - slope * float(past_len - k0) + jnp.where(chosen, 0.0, -jnp.inf)

    knew = _pad_rows(kvn_ref[:, 2 * KV_WIDTH:3 * KV_WIDTH], PAGE_SIZE)
    vnew = _pad_rows(kvn_ref[:, 3 * KV_WIDTH:4 * KV_WIDTH], PAGE_SIZE)
    s_all = [scores(page_refs[j][:, 0:KV_WIDTH], j * PAGE_SIZE, False) for j in range(n_pages)]
    s_all.append(scores(knew, past_len, True))
    m = s_all[0]
    for sj in s_all[1:]:
        m = jnp.maximum(m, sj)
    m = jnp.maximum(jnp.max(m, axis=0, keepdims=True), NEG)
    l = jnp.zeros((1, ncol), F32)
    acc = jnp.zeros((width, ncol), F32)
    for j, sj in enumerate(s_all):
        pt = jnp.exp(sj - m)
        vt = page_refs[j][:, KV_WIDTH:2 * KV_WIDTH] if j < n_pages else vnew
        l = l + jnp.sum(pt, axis=0, keepdims=True)
        acc = acc + _dot(vt.T.astype(BF16), pt.astype(BF16))
    o_s = acc / jnp.maximum(l, 1e-30)

    kw = jnp.concatenate([wst_ref[0, :, 0:KV_WIDTH], _pad_rows(winn_ref[:, 0:KV_WIDTH], PAGE_SIZE)], axis=0)
    vw = jnp.concatenate([wst_ref[0, :, KV_WIDTH:], _pad_rows(winn_ref[:, KV_WIDTH:], PAGE_SIZE)], axis=0)
    start = past_len - WINDOW
    sw = _dot(kw.astype(BF16), qbd)
    kpos = start + lax.broadcasted_iota(jnp.int32, (kw.shape[0], 1), 0)
    dw = tpos - kpos
    pw = _softmax0(sw - slope * dw.astype(F32), (dw >= 0) & (dw <= WINDOW))
    o_w = _dot(vw.T.astype(BF16), pw.astype(BF16))

    sg = jax.nn.sigmoid(gl_ref[...])
    trow = lax.broadcasted_iota(jnp.int32, (tq, 1), 0)
    x = None
    for br, o in enumerate((o_c, o_s, o_w)):
        spread = _dot_exact01(sg, gsel_ref[br])
        grow = jnp.sum(jnp.where(trow == tloc, spread, 0.0), axis=0, keepdims=True)
        x = grow * o if x is None else x + grow * o
    x = jnp.where(valid_ref[...] > 0.5, x, 0.0)
    ms = jnp.sum(x * x, axis=0, keepdims=True) * (1.0 / HEAD_DIM)
    y = (x * lax.rsqrt(ms + NORM_EPS) * gain_ref[...]).T
    for g in range(NSA_GROUP):
        og = y[g * ne * tq:g * ne * tq + tq]
        for e in range(1, ne):
            og = og + y[(g * ne + e) * tq:(g * ne + e + 1) * tq]
        o_ref[:, g * LANES:(g + 1) * LANES] = og[:, 0:LANES]
        o_ref[:, (NSA_GROUP + g) * LANES:(NSA_GROUP + g + 1) * LANES] = og[:, LANES:]

    w_len = wst_ref.shape[1]
    nw_ref[0, 0:w_len - tq, :] = wst_ref[0, tq:w_len, :]
    nw_ref[0, w_len - tq:w_len, :] = winn_ref[...]


def _sample_consts(gain):
    ne, tq = NSA_KV_HEADS, 8
    c = jnp.arange(NSA_GROUP * ne * tq)
    cg, ck = c // (ne * tq), (c // tq) % ne
    chead = ck * NSA_GROUP + cg
    r = jnp.arange(ne * HEAD_DIM)
    rk = r // HEAD_DIM
    slope = _alibi_slopes()[chead].reshape(1, -1)
    valid = (rk[:, None] == ck[None, :]).astype(F32)
    gain_t = gain.reshape(ne, NSA_GROUP, HEAD_DIM).transpose(0, 2, 1).reshape(ne * HEAD_DIM, NSA_GROUP)
    gain_t = jnp.repeat(gain_t, ne * tq, axis=1) * valid
    lanes = jnp.arange(LANES)
    sumg = ((lanes[None, :] < ne * tq) & (c[:, None] % (ne * tq) == lanes[None, :])).astype(BF16)
    tile = ((lanes[:, None] < ne * tq) & (lanes[:, None] == c[None, :] % (ne * tq))).astype(BF16)
    gsel = jnp.stack([(lanes[:, None] == (chead * 3 + br)[None, :]).astype(BF16) for br in range(3)])
    return slope, gain_t, valid, sumg, tile, gsel


def _nsa_sample(page_table, q, gl, kvn, winn, win_state, cache_rows, kvc_pages, consts, pair01, row0):
    slope, gain_t, valid, sumg, tile, gsel = consts
    n_seq, n_pages = page_table.shape
    past_len = n_pages * PAGE_SIZE
    w_len = win_state.shape[1]
    blk0 = row0 // 8
    tok = lambda s, pt: (blk0 + s, 0)
    full = lambda a: pl.BlockSpec(a.shape, lambda s, pt: (0,) * a.ndim)
    in_specs = [
        pl.BlockSpec((8, NSA_WIDTH), tok),
        pl.BlockSpec((8, LANES), tok),
        pl.BlockSpec((8, 4 * KV_WIDTH), tok),
        pl.BlockSpec((8, 2 * KV_WIDTH), tok),
        pl.BlockSpec((1, w_len, 2 * KV_WIDTH), lambda s, pt: (s, 0, 0)),
    ]
    in_specs += [pl.BlockSpec((PAGE_SIZE, 2 * KV_WIDTH), functools.partial(lambda s, pt, j: (pt[s, j], 1), j=j))
                 for j in range(n_pages)]
    in_specs += [pl.BlockSpec((1, 1, 8 * KV_WIDTH), functools.partial(lambda s, pt, j: (pt[s, j], 0, 0), j=j))
                 for j in range(n_pages)]
    in_specs += [full(slope), full(gain_t), full(valid), full(pair01), full(sumg), full(tile), full(gsel)]
    ncp = -(-(past_len // CMP_BLOCK) // LANES) * LANES
    grid_spec = pltpu.PrefetchScalarGridSpec(
        num_scalar_prefetch=1,
        grid=(n_seq,),
        in_specs=in_specs,
        out_specs=[pl.BlockSpec((8, NSA_WIDTH), lambda s, pt: (s, 0)),
                   pl.BlockSpec((1, w_len, 2 * KV_WIDTH), lambda s, pt: (s, 0, 0))],
        scratch_shapes=[pltpu.VMEM((ncp, KV_WIDTH), F32), pltpu.VMEM((ncp, KV_WIDTH), F32)],
    )
    return pl.pallas_call(
        functools.partial(_nsa_sample_kernel, past_len=past_len),
        grid_spec=grid_spec,
        out_shape=[jax.ShapeDtypeStruct((n_seq * 8, NSA_WIDTH), F32),
                   jax.ShapeDtypeStruct((n_seq, w_len, 2 * KV_WIDTH), F32)],
        compiler_params=_cparams("parallel"),
        name="nsa_sample",
    )(page_table, q, gl, kvn, winn, win_state, *([cache_rows] * n_pages), *([kvc_pages] * n_pages),
      slope, gain_t, valid, pair01, sumg, tile, gsel)


def _head_sums(x, bo01):
    return jnp.concatenate([_dot_exact01(x[:, s * LANES:(s + 1) * LANES], bo01)
                            for s in range(x.shape[1] // LANES)], axis=1)


def _rwkv_prep_kernel(z_ref, first_ref, mu_ref, w0_ref, a0_ref, kk_ref, ka_ref, rk_ref, w2_ref, a2_ref, g2_ref,
                      bo_ref, r_o, w_o, k_o, v_o, kk_o, nb_o, g_o, bonus_o, *, seq_rows):
    z = z_ref[...]
    tm = z.shape[0]
    row = lax.broadcasted_iota(jnp.int32, (tm, 1), 0)
    if seq_rows >= tm:
        at_start = pl.program_id(0) % (seq_rows // tm) == 0
        before = jnp.where(at_start, 0.0, first_ref[7:8, :])
        prev = jnp.where(row == 0, before, pltpu.roll(z, 1, 0))
    else:
        prev = jnp.where((row & (seq_rows - 1)) == 0, first_ref[...], pltpu.roll(z, 1, 0))
    zm = z + (prev - z) * mu_ref[...]
    w = RWKV_WIDTH
    r, k, v = zm[:, 0:w], zm[:, w:2 * w], zm[:, 2 * w:3 * w]
    wa = zm[:, 3 * w:3 * w + LANES]
    gd = zm[:, 3 * w + LANES:]
    wl = w0_ref[...] + _dot(jnp.tanh(wa).astype(BF16), w2_ref[...])
    neg = -wl
    softplus = jnp.maximum(neg, 0.0) + jnp.log1p(jnp.exp(-jnp.abs(neg)))
    decay = jnp.exp(-jnp.exp(-softplus - 0.5))
    a = jax.nn.sigmoid(a0_ref[...] + _dot(wa.astype(BF16), a2_ref[...]))
    g = _dot(jax.nn.sigmoid(gd).astype(BF16), g2_ref[...])
    bo = bo_ref[...]
    kk = k * kk_ref[...]
    kk = kk / jnp.maximum(jnp.sqrt(_head_sums(kk * kk, bo)), 1e-12)
    k2 = k * (1.0 + (a - 1.0) * ka_ref[...])
    r_o[...] = r
    w_o[...] = decay
    k_o[...] = k2
    v_o[...] = v
    kk_o[...] = kk
    nb_o[...] = -(kk * a)
    g_o[...] = g
    bonus_o[...] = _head_sums(r * k2 * rk_ref[...], bo) * v


def _rwkv_prep(z, first, row0, rows, tm, seq_rows, mu, w0, a0, k_k, k_a, r_k, w2p, a2p, g2p, bo01):
    blk0 = row0 // tm
    vec = lambda a: a.reshape(1, -1)
    cvec = pl.BlockSpec((1, RWKV_WIDTH), lambda i: (0, 0))
    if seq_rows >= tm:
        first_spec = pl.BlockSpec((8, Z_PAD), lambda i: (jnp.maximum((blk0 + i) * (tm // 8) - 1, 0), 0))
    else:
        first_spec = pl.BlockSpec((tm, Z_PAD), lambda i: (i, 0))
    out = jax.ShapeDtypeStruct((rows, RWKV_WIDTH), F32)
    ospec = pl.BlockSpec((tm, RWKV_WIDTH), lambda i: (i, 0))
    return pl.pallas_call(
        functools.partial(_rwkv_prep_kernel, seq_rows=seq_rows),
        grid=(rows // tm,),
        in_specs=[pl.BlockSpec((tm, Z_PAD), lambda i: (blk0 + i, 0)), first_spec,
                  pl.BlockSpec((1, Z_PAD), lambda i: (0, 0)), cvec, cvec, cvec, cvec, cvec,
                  pl.BlockSpec((LANES, RWKV_WIDTH), lambda i: (0, 0)),
                  pl.BlockSpec((LANES, RWKV_WIDTH), lambda i: (0, 0)),
                  pl.BlockSpec((2 * LANES, RWKV_WIDTH), lambda i: (0, 0)),
                  pl.BlockSpec((LANES, LANES), lambda i: (0, 0))],
        out_specs=[ospec] * 8,
        out_shape=[out] * 8,
        compiler_params=_cparams("parallel"),
        name="rwkv_prep",
    )(z, first, vec(mu), vec(w0), vec(a0), vec(k_k), vec(k_a), vec(r_k), w2p, a2p, g2p, bo01)


def _rwkv_scan_kernel(*refs, nbatch, tc, has_init):
    ops = refs[:6]
    pos = 6
    s0_ref = refs[pos] if has_init else None
    pos += 1 if has_init else 0
    pat_ref, bo_ref, y_ref, s_out, st = refs[pos:pos + 5]
    c = pl.program_id(1)
    nq = RWKV_HEADS // 4
    tw = 4 * HEAD_DIM

    @pl.when(c == 0)
    def _():
        if has_init:
            st[...] = s0_ref[...]
        else:
            st[...] = jnp.zeros_like(st)

    pat = pat_ref[...]
    bo = bo_ref[...]

    def step(t, carry):
        tiles = [(b, q) for b in range(nbatch) for q in range(nq)]
        rowv = lambda k, b, q: ops[k][b, pl.ds(t, 1), q * tw:(q + 1) * tw]
        lhs = []
        for b, q in tiles:
            s = st[b, q]
            lhs.append((s * rowv(4, b, q)).astype(BF16))
            lhs.append((pat * rowv(3, b, q)).astype(BF16))
        red = _dot(jnp.concatenate(lhs, axis=0), bo)
        outs = []
        for n, (b, q) in enumerate(tiles):
            sa = red[n * 2 * HEAD_DIM:n * 2 * HEAD_DIM + HEAD_DIM]
            vb = red[n * 2 * HEAD_DIM + HEAD_DIM:(n + 1) * 2 * HEAD_DIM]
            s2 = st[b, q] * rowv(1, b, q) + sa * rowv(5, b, q) + vb * rowv(2, b, q)
            st[b, q] = s2
            outs.append((s2 * rowv(0, b, q)).astype(BF16))
        yb = _dot(jnp.concatenate(outs, axis=0), bo)
        for n, (b, q) in enumerate(tiles):
            yrow = jnp.sum(yb[n * HEAD_DIM:(n + 1) * HEAD_DIM] * pat, axis=0, keepdims=True)
            y_ref[b, pl.ds(t, 1), q * tw:(q + 1) * tw] = yrow
        return carry

    lax.fori_loop(0, tc, step, 0, unroll=4)

    @pl.when(c == pl.num_programs(1) - 1)
    def _():
        s_out[...] = st[...]


def _rwkv_scan(ops, s0, nseq, seq_rows, nbatch, tc, pat, bo01):
    nq = RWKV_HEADS // 4
    tok_spec = pl.BlockSpec((nbatch, tc, RWKV_WIDTH), lambda gi, c: (gi, c, 0))
    st_spec = pl.BlockSpec((nbatch, nq, HEAD_DIM, 4 * HEAD_DIM), lambda gi, c: (gi, 0, 0, 0))
    in_specs = [tok_spec] * 6
    args = list(ops)
    if s0 is not None:
        in_specs.append(st_spec)
        args.append(s0)
    in_specs += [pl.BlockSpec((HEAD_DIM, 4 * HEAD_DIM), lambda gi, c: (0, 0)),
                 pl.BlockSpec((4 * HEAD_DIM, 4 * HEAD_DIM), lambda gi, c: (0, 0))]
    args += [pat, bo01]
    return pl.pallas_call(
        functools.partial(_rwkv_scan_kernel, nbatch=nbatch, tc=tc, has_init=s0 is not None),
        grid=(nseq // nbatch, seq_rows // tc),
        in_specs=in_specs,
        out_specs=[tok_spec, st_spec],
        out_shape=[jax.ShapeDtypeStruct((nseq, seq_rows, RWKV_WIDTH), F32),
                   jax.ShapeDtypeStruct((nseq, nq, HEAD_DIM, 4 * HEAD_DIM), F32)],
        scratch_shapes=[pltpu.VMEM((nbatch, nq, HEAD_DIM, 4 * HEAD_DIM), F32)],
        compiler_params=_cparams("parallel", "arbitrary"),
        name="rwkv_scan",
    )(*args)


def _rwkv_post_kernel(y_ref, g_ref, bonus_ref, lnw_ref, lnb_ref, bo_ref, o_ref):
    y = y_ref[...]
    bo = bo_ref[...]
    d = y - _head_sums(y, bo) * (1.0 / HEAD_DIM)
    var = _head_sums(d * d, bo) * (1.0 / HEAD_DIM)
    o = (d * lax.rsqrt(var + GN_EPS) * lnw_ref[...] + lnb_ref[...] + bonus_ref[...]) * g_ref[...]
    o_ref[...] = o.astype(o_ref.dtype)


def _rwkv_post(y, g, bonus, ln_w, ln_b, bo01, tm):
    rows = y.shape[0]
    tok = pl.BlockSpec((tm, RWKV_WIDTH), lambda i: (i, 0))
    cvec = pl.BlockSpec((1, RWKV_WIDTH), lambda i: (0, 0))
    return pl.pallas_call(
        _rwkv_post_kernel,
        grid=(rows // tm,),
        in_specs=[tok, tok, tok, cvec, cvec, pl.BlockSpec((LANES, LANES), lambda i: (0, 0))],
        out_specs=tok,
        out_shape=jax.ShapeDtypeStruct((rows, RWKV_WIDTH), BF16),
        compiler_params=_cparams("parallel"),
        name="rwkv_post",
    )(y, g, bonus, ln_w.reshape(1, -1), ln_b.reshape(1, -1), bo01)


def _rwkv_consts(w2, a2, g2):
    w2p = jnp.concatenate([w2, jnp.zeros((LANES - DECAY_LORA, RWKV_WIDTH), w2.dtype)], axis=0)
    a2p = jnp.concatenate([jnp.zeros((DECAY_LORA, RWKV_WIDTH), a2.dtype), a2], axis=0)
    g2p = jnp.concatenate([g2, jnp.zeros((2 * LANES - GATE_LORA, RWKV_WIDTH), g2.dtype)], axis=0)
    v = jnp.arange(HEAD_DIM)
    pat = (v[:, None] == (jnp.arange(4 * HEAD_DIM)[None, :] % HEAD_DIM)).astype(F32)
    return w2p.astype(BF16), a2p.astype(BF16), g2p.astype(BF16), pat


def _state_to_tiles(s):
    n = s.shape[0]
    return s.reshape(n, 4, 4, HEAD_DIM, HEAD_DIM).transpose(0, 1, 3, 2, 4).reshape(n, 4, HEAD_DIM, 4 * HEAD_DIM)


def _tiles_to_state(t):
    n = t.shape[0]
    return t.reshape(n, 4, HEAD_DIM, 4, HEAD_DIM).transpose(0, 1, 3, 2, 4).reshape(n, RWKV_HEADS, HEAD_DIM, HEAD_DIM)


def _outproj_kernel(x_ref, on_ref, or_ref, wa_ref, wb_ref, g_ref, rwh_ref, rwl_ref, rb_ref,
                    x1_ref, h2_ref, te_ref, tg_ref):
    x1 = x_ref[...] + _dot(on_ref[...], wa_ref[...]) + _dot(or_ref[...], wb_ref[...])
    x1_ref[...] = x1
    ms = jnp.mean(x1 * x1, axis=-1, keepdims=True)
    hf = x1 * lax.rsqrt(ms + NORM_EPS) * g_ref[...]
    h2_ref[...] = hf
    hh = hf.astype(BF16)
    hl = (hf - hh.astype(F32)).astype(BF16)
    logits = _dot(hh, rwh_ref[...]) + _dot(hl, rwh_ref[...]) + _dot(hh, rwl_ref[...]) + rb_ref[...]
    lane = lax.broadcasted_iota(jnp.int32, logits.shape, 1).astype(F32)
    vals, idxs = [], []
    for _ in range(MOE_TOPK):
        m = jnp.max(logits, axis=-1, keepdims=True)
        idx = jnp.min(jnp.where(logits == m, lane, float(LANES)), axis=-1, keepdims=True)
        vals.append(m)
        idxs.append(idx)
        logits = jnp.where(lane == idx, -jnp.inf, logits)
    es = [jnp.exp(v - vals[0]) for v in vals]
    denom = es[0] + es[1] + es[2] + es[3]
    te = jnp.zeros(logits.shape, F32)
    tg = jnp.zeros(logits.shape, F32)
    for k in range(MOE_TOPK):
        te = jnp.where(lane == float(k), idxs[k], te)
        tg = jnp.where(lane == float(k), es[k] / denom, tg)
    te_ref[...] = te.astype(jnp.int32)
    tg_ref[...] = tg


def _outproj_router(x, o_nsa, o_rwkv, wa, wb, g, rw_hi, rw_lo, rb, tm=256):
    t = x.shape[0]
    tok = lambda w: pl.BlockSpec((tm, w), lambda i: (i, 0))
    full = lambda a: pl.BlockSpec(a.shape, lambda i: (0,) * a.ndim)
    return pl.pallas_call(
        _outproj_kernel,
        grid=(t // tm,),
        in_specs=[tok(D_MODEL), tok(NSA_WIDTH), tok(RWKV_WIDTH), full(wa), full(wb), full(g), full(rw_hi),
                  full(rw_lo), full(rb)],
        out_specs=[tok(D_MODEL), tok(D_MODEL), tok(LANES), tok(LANES)],
        out_shape=[jax.ShapeDtypeStruct((t, D_MODEL), F32), jax.ShapeDtypeStruct((t, D_MODEL), F32),
                   jax.ShapeDtypeStruct((t, LANES), jnp.int32), jax.ShapeDtypeStruct((t, LANES), F32)],
        compiler_params=_cparams("parallel"),
        name="outproj_router",
    )(x, o_nsa, o_rwkv, wa, wb, g, rw_hi, rw_lo, rb)


MOE_BM = 256


def _route(top_e, bm):
    n_tok = top_e.shape[0]
    flat_e = top_e.reshape(-1)
    n_assign = flat_e.shape[0]
    onehot = (flat_e[:, None] == jnp.arange(N_EXPERTS, dtype=jnp.int32)[None, :]).astype(jnp.int32)
    cum = jnp.cumsum(onehot, axis=0)
    counts = cum[-1]
    pos = jnp.take_along_axis(cum, flat_e[:, None], axis=1)[:, 0] - 1
    padded = (counts + bm - 1) // bm * bm
    pad_end = jnp.cumsum(padded)
    dest = (pad_end - padded)[flat_e] + pos
    n_blocks = -(-n_assign // bm) + N_EXPERTS
    slot_tok = jnp.zeros((n_blocks * bm,), jnp.int32).at[dest].set(jnp.arange(n_assign, dtype=jnp.int32) // MOE_TOPK)
    n_used = (pad_end[-1] // bm).astype(jnp.int32)
    blk = jnp.minimum(jnp.arange(n_blocks, dtype=jnp.int32), n_used - 1)
    blk_e = jnp.minimum(jnp.sum((pad_end[None, :] <= (blk * bm)[:, None]).astype(jnp.int32), axis=1), N_EXPERTS - 1)
    return slot_tok, dest.reshape(n_tok, MOE_TOPK).astype(jnp.int32), blk_e, n_used.reshape(1), n_blocks


def _row_copy(src_ref, dst_ref, sem, src_row, dst_row):
    return pltpu.make_async_copy(src_ref.at[pl.ds(src_row, 1)], dst_ref.at[pl.ds(dst_row, 1)], sem)


def _gather_issue(idx_ref, src_ref, buf, sem):
    n = buf.shape[0]

    def issue(r2, c):
        for par in range(2):
            r = 2 * r2 + par
            _row_copy(src_ref, buf, sem, idx_ref[0, 0, r], r).start(priority=par)
        return c

    lax.fori_loop(0, n // 2, issue, 0, unroll=4)


def _gather_wait(src_ref, buf, sem):
    pltpu.make_async_copy(src_ref.at[pl.ds(0, buf.shape[0])], buf, sem).wait()


def _gather_pipelined(cur_ref, nxt_ref, src_ref, buf, sem, nsteps):
    b = pl.program_id(0)
    slot = b & 1

    @pl.when(b == 0)
    def _():
        _gather_issue(cur_ref, src_ref, buf.at[0], sem.at[0])

    @pl.when(b + 1 < nsteps)
    def _():
        _gather_issue(nxt_ref, src_ref, buf.at[1 - slot], sem.at[1 - slot])

    _gather_wait(src_ref, buf.at[slot], sem.at[slot])
    return slot


def _moe_gather_kernel(cur_ref, nxt_ref, h_ref, o_ref, buf, sem, *, nsteps):
    slot = _gather_pipelined(cur_ref, nxt_ref, h_ref, buf, sem, nsteps)
    o_ref[...] = buf[slot].astype(o_ref.dtype)


def _moe_gather(slot_tok, h2, bm):
    n_slots = slot_tok.shape[0]
    nblk = n_slots // bm
    return pl.pallas_call(
        functools.partial(_moe_gather_kernel, nsteps=nblk),
        grid=(nblk,),
        in_specs=[pl.BlockSpec((1, 1, bm), lambda b: (b, 0, 0), memory_space=pltpu.SMEM),
                  pl.BlockSpec((1, 1, bm), lambda b: (jnp.minimum(b + 1, nblk - 1), 0, 0), memory_space=pltpu.SMEM),
                  pl.BlockSpec(memory_space=pl.ANY)],
        out_specs=pl.BlockSpec((bm, D_MODEL), lambda b: (b, 0)),
        out_shape=jax.ShapeDtypeStruct((n_slots, D_MODEL), BF16),
        scratch_shapes=[pltpu.VMEM((2, bm, D_MODEL), F32), pltpu.SemaphoreType.DMA((2,))],
        compiler_params=_cparams("arbitrary"),
        name="moe_gather",
    )(slot_tok.reshape(nblk, 1, bm), slot_tok.reshape(nblk, 1, bm), h2)


MOE_FT = 1024
MOE_NF = D_FF // MOE_FT


def _expert_changed(be_ref, b):
    prev = be_ref[jnp.maximum(b - 1, 0)]
    return (b == 0) | (be_ref[b] != prev)


def _moe_up_kernel(be_ref, nu_ref, x_ref, wg_ref, wl_ref, bg_ref, bl_ref, o_ref, wg_s, wl_s):
    b = pl.program_id(1)

    @pl.when(b < nu_ref[0])
    def _():
        @pl.when(_expert_changed(be_ref, b))
        def _():
            wg_s[...] = wg_ref[0].astype(BF16)
            wl_s[...] = wl_ref[0].astype(BF16)

        x = x_ref[...]
        glu = jnp.minimum(_dot(x, wg_s[...]) + bg_ref[0], SWIGLU_LIMIT)
        lin = jnp.clip(_dot(x, wl_s[...]) + bl_ref[0], -SWIGLU_LIMIT, SWIGLU_LIMIT)
        o_ref[...] = (glu * jax.nn.sigmoid(SWIGLU_ALPHA * glu) * (lin + 1.0)).astype(o_ref.dtype)

    @pl.when(b >= nu_ref[0])
    def _():
        o_ref[...] = jnp.zeros_like(o_ref)


def _moe_up(blk_e, n_used, xs, w1, b1, bm):
    n_slots = xs.shape[0]
    nblk = n_slots // bm
    live = lambda b, nu: jnp.minimum(b, nu[0] - 1)
    grid_spec = pltpu.PrefetchScalarGridSpec(
        num_scalar_prefetch=2,
        grid=(MOE_NF, nblk),
        in_specs=[
            pl.BlockSpec((bm, D_MODEL), lambda f, b, be, nu: (live(b, nu), 0)),
            pl.BlockSpec((1, D_MODEL, MOE_FT), lambda f, b, be, nu: (be[b], 0, f)),
            pl.BlockSpec((1, D_MODEL, MOE_FT), lambda f, b, be, nu: (be[b], 0, MOE_NF + f)),
            pl.BlockSpec((1, 1, MOE_FT), lambda f, b, be, nu: (be[b], 0, f)),
            pl.BlockSpec((1, 1, MOE_FT), lambda f, b, be, nu: (be[b], 0, MOE_NF + f)),
        ],
        out_specs=pl.BlockSpec((bm, MOE_FT), lambda f, b, be, nu: (b, f)),
        scratch_shapes=[pltpu.VMEM((D_MODEL, MOE_FT), BF16), pltpu.VMEM((D_MODEL, MOE_FT), BF16)],
    )
    return pl.pallas_call(
        _moe_up_kernel,
        grid_spec=grid_spec,
        out_shape=jax.ShapeDtypeStruct((n_slots, D_FF), BF16),
        compiler_params=_cparams("arbitrary", "arbitrary"),
        name="moe_up",
    )(blk_e, n_used, xs, w1, w1, b1, b1)


def _moe_down_kernel(be_ref, nu_ref, a_ref, w_ref, b_ref, o_ref, w_s):
    b = pl.program_id(0)

    @pl.when(b < nu_ref[0])
    def _():
        @pl.when(_expert_changed(be_ref, b))
        def _():
            w_s[...] = w_ref[0].astype(BF16)

        o_ref[...] = _dot(a_ref[...], w_s[...]) + b_ref[0]

    @pl.when(b >= nu_ref[0])
    def _():
        o_ref[...] = jnp.zeros_like(o_ref)


def _moe_down(blk_e, n_used, act, w2, b2, bm):
    n_slots = act.shape[0]
    nblk = n_slots // bm
    live = lambda b, nu: jnp.minimum(b, nu[0] - 1)
    grid_spec = pltpu.PrefetchScalarGridSpec(
        num_scalar_prefetch=2,
        grid=(nblk,),
        in_specs=[
            pl.BlockSpec((bm, D_FF), lambda b, be, nu: (live(b, nu), 0)),
            pl.BlockSpec((1, D_FF, D_MODEL), lambda b, be, nu: (be[b], 0, 0)),
            pl.BlockSpec((1, 1, D_MODEL), lambda b, be, nu: (be[b], 0, 0)),
        ],
        out_specs=pl.BlockSpec((bm, D_MODEL), lambda b, be, nu: (b, 0)),
        scratch_shapes=[pltpu.VMEM((D_FF, D_MODEL), BF16)],
    )
    return pl.pallas_call(
        _moe_down_kernel,
        grid_spec=grid_spec,
        out_shape=jax.ShapeDtypeStruct((n_slots, D_MODEL), F32),
        compiler_params=_cparams("arbitrary"),
        name="moe_down",
    )(blk_e, n_used, act, w2, b2)


def _combine_kernel(cur_ref, nxt_ref, x_ref, tg_ref, g_ref, oh_ref, yb_ref, o_ref, buf, sem, *, nsteps):
    tm = x_ref.shape[0]
    slot = _gather_pipelined(cur_ref, nxt_ref, yb_ref, buf, sem, nsteps)
    x = x_ref[...]
    gates = tg_ref[...]
    for k in range(MOE_TOPK):
        gk = _dot_exact01(gates, oh_ref[k])
        x = x + jnp.concatenate([gk] * (D_MODEL // LANES), axis=1) * buf[slot, k * tm:(k + 1) * tm]
    ms = jnp.mean(x * x, axis=-1, keepdims=True)
    o_ref[...] = x * lax.rsqrt(ms + NORM_EPS) * g_ref[...]


def _moe_combine(slot_of, x1, tg, g, yb, row0, t, tm=128):
    nt = t // tm
    blk0 = row0 // tm
    slots = slot_of[row0:row0 + t].reshape(nt, tm, MOE_TOPK).transpose(0, 2, 1).reshape(nt, 1, MOE_TOPK * tm)
    onehot = (jnp.arange(LANES)[None, :, None] == jnp.arange(MOE_TOPK)[:, None, None]).astype(BF16)
    onehot = jnp.broadcast_to(onehot, (MOE_TOPK, LANES, LANES))
    return pl.pallas_call(
        functools.partial(_combine_kernel, nsteps=nt),
        grid=(nt,),
        in_specs=[pl.BlockSpec((1, 1, MOE_TOPK * tm), lambda i: (i, 0, 0), memory_space=pltpu.SMEM),
                  pl.BlockSpec((1, 1, MOE_TOPK * tm), lambda i: (jnp.minimum(i + 1, nt - 1), 0, 0),
                               memory_space=pltpu.SMEM),
                  pl.BlockSpec((tm, D_MODEL), lambda i: (blk0 + i, 0)),
                  pl.BlockSpec((tm, LANES), lambda i: (blk0 + i, 0)),
                  pl.BlockSpec((1, D_MODEL), lambda i: (0, 0)),
                  pl.BlockSpec((MOE_TOPK, LANES, LANES), lambda i: (0, 0, 0)),
                  pl.BlockSpec(memory_space=pl.ANY)],
        out_specs=pl.BlockSpec((tm, D_MODEL), lambda i: (i, 0)),
        out_shape=jax.ShapeDtypeStruct((t, D_MODEL), F32),
        scratch_shapes=[pltpu.VMEM((2, MOE_TOPK * tm, D_MODEL), F32), pltpu.SemaphoreType.DMA((2,))],
        compiler_params=_cparams("arbitrary"),
        name="moe_combine",
    )(slots, slots, x1, tg, g.reshape(1, -1), onehot, yb)


def _head_perm():
    p, g, e, d = jnp.meshgrid(jnp.arange(2), jnp.arange(NSA_GROUP), jnp.arange(2), jnp.arange(HEAD_DIM),
                              indexing="ij")
    return (((2 * p + e) * NSA_GROUP + g) * HEAD_DIM + d).reshape(-1)


def _alibi_slopes():
    return 2.0 ** (-8.0 * jnp.arange(1, NSA_HEADS + 1, dtype=F32) / NSA_HEADS)


def _block_ones(width):
    i = jnp.arange(width)
    return (i[:, None] // HEAD_DIM == i[None, :] // HEAD_DIM).astype(BF16)


def _cmp_weights_bd(wk, wv):
    eye = jnp.eye(2, dtype=wk.dtype)
    bd = lambda w: jnp.einsum("hg,bde->bhdge", eye, w).reshape(CMP_BLOCK, LANES, LANES)
    return jnp.stack([bd(wk), bd(wv)]).astype(BF16)


def _pad_cols(a, width):
    return jnp.pad(a, ((0, 0), (0, width - a.shape[1])))


def kernel(x_prompt, x_sample, cache_nsa_kv, state_win_kv, state_rwkv, state_shift, page_table, norm_attn, w_in,
           nsa_w_cmp_k, nsa_w_cmp_v, nsa_out_gain, rwkv_mu, rwkv_w0, rwkv_w2, rwkv_a0, rwkv_a2, rwkv_g2, rwkv_k_k,
           rwkv_k_a, rwkv_r_k, rwkv_ln_w, rwkv_ln_b, w_out, norm_ffn, router_w, router_b, moe_w1, moe_b1, moe_w2,
           moe_b2, norm_final):
    nb_p, seq_p, _ = x_prompt.shape
    nb_s, seq_s, _ = x_sample.shape
    depth = norm_attn.shape[0]
    assert depth == 1 and seq_s == 8 and seq_p % (2 * Q_BLOCK) == 0
    tp, ts = nb_p * seq_p, nb_s * seq_s
    n_phys = cache_nsa_kv.shape[1]
    w_len = state_win_kv.shape[2]
    l = 0
    x = jnp.concatenate([x_prompt.reshape(tp, D_MODEL), x_sample.reshape(ts, D_MODEL)], axis=0)

    perm = _head_perm()
    w = w_in[l]
    c0, c1, c2 = NSA_WIDTH, NSA_WIDTH + 4 * KV_WIDTH, NSA_WIDTH + 6 * KV_WIDTH
    c3 = c2 + 3 * NSA_HEADS
    h = _rmsnorm(x, norm_attn[l], BF16)
    q = _matmul(h, (w[:, :c0] * (HEAD_DIM ** -0.5))[:, perm].astype(BF16), F32, 512, 512, "proj_q")
    kvn = _matmul(h, w[:, c0:c1].astype(BF16), F32, 512, 512, "proj_kv")
    winn = _matmul(h, w[:, c1:c2].astype(BF16), F32, 512, 512, "proj_win")
    gl = _matmul(h, _pad_cols(w[:, c2:c3], LANES).astype(BF16), F32, 512, LANES, "proj_gate")
    z = _matmul(h, _pad_cols(w[:, c3:], Z_PAD).astype(BF16), F32, 512, Z_PAD // 3, "proj_z")

    w_bd = _cmp_weights_bd(nsa_w_cmp_k[l], nsa_w_cmp_v[l])
    cache_rows = cache_nsa_kv[l].reshape(n_phys * PAGE_SIZE, 4 * KV_WIDTH)
    kvc_new = _compress(kvn, w_bd, tp // CMP_BLOCK, 256)
    blocks_phys = n_phys * (PAGE_SIZE // CMP_BLOCK)
    kvc_phys = _compress(cache_rows, w_bd, blocks_phys, 512)
    o_p = _nsa_prompt(q, gl, kvn, winn, kvc_new, jnp.stack([_slope_cols(2, Q_BLOCK, p) for p in range(2)]),
                      _gain_cols(nsa_out_gain[l]), _pair01_t(seq_p // CMP_BLOCK), nb_p, seq_p)
    n_cmp_pad = -(-(page_table.shape[1] * PAGE_SIZE // CMP_BLOCK) // LANES) * LANES
    o_s, new_win_s = _nsa_sample(
        page_table, q, gl, kvn, winn, state_win_kv[l].reshape(nb_s, w_len, 2 * KV_WIDTH), cache_rows,
        kvc_phys.reshape(n_phys, 1, (PAGE_SIZE // CMP_BLOCK) * 2 * KV_WIDTH),
        _sample_consts(nsa_out_gain[l]), _pair01_t(n_cmp_pad), tp)
    o_nsa = jnp.concatenate([o_p, o_s.astype(BF16)], axis=0)

    w2p, a2p, g2p, pat = _rwkv_consts(rwkv_w2[l], rwkv_a2[l], rwkv_g2[l])
    bo128 = _block_ones(LANES)
    mu = jnp.pad(rwkv_mu[l], (0, Z_PAD - SHIFT_WIDTH))
    vecs = (mu, rwkv_w0[l], rwkv_a0[l], rwkv_k_k[l], rwkv_k_a[l], rwkv_r_k[l], w2p, a2p, g2p, bo128)
    first_s = jnp.zeros((nb_s, seq_s, Z_PAD), F32).at[:, 0, :SHIFT_WIDTH].set(state_shift[l]).reshape(ts, Z_PAD)
    prep_p = _rwkv_prep(z, z, 0, tp, 512, seq_p, *vecs)
    prep_s = _rwkv_prep(z, first_s, tp, ts, 256, seq_s, *vecs)
    bo256 = _block_ones(2 * LANES)
    y_p, st_p = _rwkv_scan([a.reshape(nb_p, seq_p, RWKV_WIDTH) for a in prep_p[:6]], None, nb_p, seq_p, nb_p, 128,
                           pat, bo256)
    y_s, st_s = _rwkv_scan([a.reshape(nb_s, seq_s, RWKV_WIDTH) for a in prep_s[:6]], _state_to_tiles(state_rwkv[l]),
                           nb_s, seq_s, 4, seq_s, pat, bo256)
    o_rw_p = _rwkv_post(y_p.reshape(tp, RWKV_WIDTH), prep_p[6], prep_p[7], rwkv_ln_w[l], rwkv_ln_b[l], bo128, 512)
    o_rw_s = _rwkv_post(y_s.reshape(ts, RWKV_WIDTH), prep_s[6], prep_s[7], rwkv_ln_w[l], rwkv_ln_b[l], bo128, 512)
    o_rwkv = jnp.concatenate([o_rw_p, o_rw_s], axis=0)

    wo = w_out[l]
    rw = _pad_cols(router_w[l], LANES)
    rw_hi = rw.astype(BF16)
    rw_lo = (rw - rw_hi.astype(F32)).astype(BF16)
    rb = jnp.concatenate([router_b[l].astype(F32), jnp.full((LANES - N_EXPERTS,), -1e30, F32)]).reshape(1, LANES)
    x1, h2, top_e, top_g = _outproj_router(x, o_nsa, o_rwkv, wo[:NSA_WIDTH][perm].astype(BF16),
                                           wo[NSA_WIDTH:].astype(BF16), norm_ffn[l].reshape(1, -1), rw_hi, rw_lo, rb)
    slot_tok, slot_of, blk_e, n_used, _ = _route(top_e[:, :MOE_TOPK], MOE_BM)
    xs = _moe_gather(slot_tok, h2, MOE_BM)
    act = _moe_up(blk_e, n_used, xs, moe_w1[l], moe_b1[l].reshape(N_EXPERTS, 1, 2 * D_FF), MOE_BM)
    yb = _moe_down(blk_e, n_used, act, moe_w2[l], moe_b2[l].reshape(N_EXPERTS, 1, D_MODEL), MOE_BM)
    y_p = _moe_combine(slot_of, x1, top_g, norm_final, yb, 0, tp)
    y_s = _moe_combine(slot_of, x1, top_g, norm_final, yb, tp, ts)

    hd = (NSA_KV_HEADS, HEAD_DIM)
    kv_p = kvn[:tp].reshape(1, nb_p, seq_p, 4, *hd)
    kv_s = kvn[tp:].reshape(1, nb_s, seq_s, 4, *hd)
    win_keep = min(WINDOW, seq_p)
    win_p = winn[:tp].reshape(nb_p, seq_p, 2, *hd)[None, :, seq_p - win_keep:]
    win_s = new_win_s.reshape(1, nb_s, w_len, 2, *hd)
    sh_p = z[seq_p - 1:tp:seq_p, :SHIFT_WIDTH][None]
    sh_s = z[tp + seq_s - 1::seq_s, :SHIFT_WIDTH][None]
    return (y_p.reshape(nb_p, seq_p, D_MODEL), y_s.reshape(nb_s, seq_s, D_MODEL), kv_p, kv_s, win_p, win_s,
            _tiles_to_state(st_p)[None], _tiles_to_state(st_s)[None], sh_p, sh_s)
```

```python
import functools

import jax
import jax.numpy as jnp
from jax import lax
from jax.experimental import pallas as pl
from jax.experimental.pallas import tpu as pltpu

F32 = jnp.float32
BF16 = jnp.bfloat16

D_MODEL = 2048
HEAD_DIM = 64
NSA_HEADS = 16
NSA_KV_HEADS = 4
NSA_GROUP = 4
NSA_WIDTH = 1024
KV_WIDTH = 256
CMP_BLOCK = 32
SEL_BLOCK = 64
SEL_TOPK = 16
WINDOW = 512
Q_BLOCK = 128
PAGE_SIZE = 128
RWKV_HEADS = 16
RWKV_WIDTH = 1024
DECAY_LORA = 64
ICLR_LORA = 64
GATE_LORA = 160
SHIFT_WIDTH = 3 * RWKV_WIDTH + DECAY_LORA + ICLR_LORA + GATE_LORA
Z_PAD = 3456
N_EXPERTS = 32
MOE_TOPK = 4
D_FF = 2048
SWIGLU_LIMIT = 7.0
SWIGLU_ALPHA = 1.702
NORM_EPS = 1e-5
GN_EPS = 64e-5
LANES = 128
VMEM_LIMIT = 56 * 1024 * 1024
NEG = -1e30


def _cparams(*sem):
    return pltpu.CompilerParams(dimension_semantics=sem, vmem_limit_bytes=VMEM_LIMIT)


def _dot(a, b):
    return jnp.dot(a, b, preferred_element_type=F32)


def _split3(x):
    hi = x.astype(BF16)
    r1 = x - hi.astype(F32)
    mid = r1.astype(BF16)
    lo = (r1 - mid.astype(F32)).astype(BF16)
    return hi, mid, lo


def _dot_exact01(x, m01):
    hi, mid, lo = _split3(x)
    return _dot(hi, m01) + _dot(mid, m01) + _dot(lo, m01)


def _rows_of(xp_ref, xs_ref, n_p):
    return jnp.where(pl.program_id(0) < n_p, xp_ref[...], xs_ref[...])


def _row_specs(tm, n_p, width):
    return [pl.BlockSpec((tm, width), lambda i: (jnp.minimum(i, n_p - 1), 0)),
            pl.BlockSpec((tm, width), lambda i: (jnp.maximum(i - n_p, 0), 0))]


def _proj_kernel(xp_ref, xs_ref, g_ref, *refs, n_p):
    n = len(refs) // 2
    x = _rows_of(xp_ref, xs_ref, n_p)
    ms = jnp.mean(x * x, axis=-1, keepdims=True)
    h = (x * lax.rsqrt(ms + NORM_EPS) * g_ref[...]).astype(BF16)
    for w_ref, o_ref in zip(refs[:n], refs[n:]):
        o_ref[...] = _dot(h, w_ref[...])


def _in_proj(xp, xs, g, weights, tm=256):
    t = xp.shape[0] + xs.shape[0]
    n_p = xp.shape[0] // tm
    once = functools.partial(pl.BlockSpec, pipeline_mode=pl.Buffered(1))
    return pl.pallas_call(
        functools.partial(_proj_kernel, n_p=n_p),
        grid=(t // tm,),
        in_specs=_row_specs(tm, n_p, D_MODEL) + [once((1, D_MODEL), lambda i: (0, 0))]
        + [once(w.shape, lambda i: (0, 0)) for w in weights],
        out_specs=[pl.BlockSpec((tm, w.shape[1]), lambda i: (i, 0)) for w in weights],
        out_shape=[jax.ShapeDtypeStruct((t, w.shape[1]), F32) for w in weights],
        compiler_params=_cparams("parallel"),
        name="in_proj",
    )(xp, xs, g.reshape(1, -1), *weights)


def _compress_kernel(x_ref, w_ref, o_ref):
    nb = o_ref.shape[0]
    acc = jnp.zeros((nb, LANES), F32)
    for b in range(CMP_BLOCK):
        a = x_ref[pl.ds(b, nb, stride=CMP_BLOCK), :].astype(BF16)
        acc = acc + _dot(a, w_ref[0, b])
    o_ref[...] = acc


def _compress(rows, w_bd, n_blocks, nb):
    slabs = 2 * KV_WIDTH // LANES
    return pl.pallas_call(
        _compress_kernel,
        grid=(n_blocks // nb, slabs),
        in_specs=[
            pl.BlockSpec((nb * CMP_BLOCK, LANES), lambda i, s: (i, s)),
            pl.BlockSpec((1, CMP_BLOCK, LANES, LANES), lambda i, s: (s // 2, 0, 0, 0)),
        ],
        out_specs=pl.BlockSpec((nb, LANES), lambda i, s: (i, s)),
        out_shape=jax.ShapeDtypeStruct((n_blocks, 2 * KV_WIDTH), F32),
        compiler_params=_cparams("parallel", "arbitrary"),
        name="nsa_compress",
    )(rows, w_bd)


def _dot01_left(m01, x):
    hi, mid, lo = _split3(x)
    return _dot(m01, hi) + _dot(m01, mid) + _dot(m01, lo)


def _softmax0(s, mask):
    s = jnp.where(mask, s, -jnp.inf)
    m = jnp.max(s, axis=0, keepdims=True)
    m = jnp.where(jnp.isfinite(m), m, 0.0)
    e = jnp.exp(s - m)
    return e / jnp.maximum(jnp.sum(e, axis=0, keepdims=True), 1e-30)


def _select_blocks_t(imp, cur, n_rank):
    j = lax.broadcasted_iota(jnp.int32, (imp.shape[0], 1), 0)
    forced = (j == cur) | (j == 0)
    key = jnp.where(forced, jnp.inf, jnp.where(j < cur, imp, -jnp.inf))
    rank = jnp.zeros(key.shape, F32)
    for i in range(n_rank):
        ri = key[i:i + 1, :]
        ahead = (ri > key) | ((ri == key) & (j > i))
        rank = rank + jnp.where(ahead, 1.0, 0.0)
    return jnp.where(rank < SEL_TOPK, 1.0, 0.0)


def _dyn_row(x, r):
    rows = lax.broadcasted_iota(jnp.int32, (x.shape[0], 1), 0)
    return jnp.sum(jnp.where(rows == r, x, 0.0), axis=0, keepdims=True)


def _nsa_prompt_kernel(q_ref, gl_ref, ks_ref, vs_ref, kw_ref, vw_ref, kc_ref, vc_ref, slope_ref, gain_ref,
                       pair_ref, o_ref, m_ref, l_ref, acc_ref, b0_ref, *, seq):
    ne, tq = 2, Q_BLOCK
    ncol = NSA_GROUP * ne * tq
    cet = ne * tq
    n_cmp = seq // CMP_BLOCK
    tk = 2 * Q_BLOCK
    i = pl.program_id(1)
    p = pl.program_id(2)
    p0 = i * tq

    qt = q_ref[...].T
    sub = lax.broadcasted_iota(jnp.int32, (ne * HEAD_DIM, 1), 0)
    cols = []
    for g in range(NSA_GROUP):
        qg = qt[g * LANES:(g + 1) * LANES]
        for e in range(ne):
            cols.append(jnp.where((sub >= e * HEAD_DIM) & (sub < (e + 1) * HEAD_DIM), qg, 0.0))
    qbd = jnp.concatenate(cols, axis=1).astype(BF16)

    lane = lax.broadcasted_iota(jnp.int32, (1, ncol), 1)
    tloc = lane & (tq - 1)
    tpos = p0 + tloc
    slope = slope_ref[0]

    s = _dot(kc_ref[...].astype(BF16), qbd)
    cmp_end = (lax.broadcasted_iota(jnp.int32, (n_cmp, 1), 0) + 1) * CMP_BLOCK - 1
    dist = (tpos - cmp_end).astype(F32)
    pc = _softmax0(s - slope * dist, dist >= 0)
    o_c = _dot(vc_ref[...].T.astype(BF16), pc.astype(BF16))

    psum = pc[:, 0:cet]
    for g in range(1, NSA_GROUP):
        psum = psum + pc[:, g * cet:(g + 1) * cet]
    n_sel = seq // SEL_BLOCK
    imp = _dot01_left(pair_ref[...], psum)
    sel = _select_blocks_t(imp, jnp.right_shift(tpos[:, 0:cet], 6), n_sel).astype(BF16)

    ksub = lax.broadcasted_iota(jnp.int32, (tk, 1), 0)
    b0_ref[...] = slope * (tloc - ksub).astype(F32)
    m_ref[...] = jnp.full(m_ref.shape, NEG, F32)
    l_ref[...] = jnp.zeros(l_ref.shape, F32)
    acc_ref[...] = jnp.zeros(acc_ref.shape, F32)
    jrow = lax.broadcasted_iota(jnp.int32, (1, n_sel), 1)

    def tile(k0, causal):
        kt = ks_ref[pl.ds(k0, tk), :].astype(BF16)
        vt = vs_ref[pl.ds(k0, tk), :].T.astype(BF16)
        st = _dot(kt, qbd)
        expand = jnp.where(jnp.right_shift(k0 + ksub, 6) == jrow, 1.0, 0.0).astype(BF16)
        chosen = _dot(expand, sel) > 0.5
        off = (p0 - k0).astype(F32)
        if causal:
            chosen = chosen & ((tloc[:, 0:cet] - ksub + (p0 - k0)) >= 0)
        cb = jnp.where(chosen, 0.0, -jnp.inf)
        cb = jnp.concatenate([cb] * NSA_GROUP, axis=1)
        s2 = st - b0_ref[...] - slope * off + cb
        m_old = m_ref[...]
        m_new = jnp.maximum(m_old, jnp.max(s2, axis=0, keepdims=True))
        alpha = jnp.exp(m_old - m_new)
        pt = jnp.exp(s2 - m_new)
        l_ref[...] = alpha * l_ref[...] + jnp.sum(pt, axis=0, keepdims=True)
        acc_ref[...] = alpha * acc_ref[...] + _dot(vt, pt.astype(BF16))
        m_ref[...] = m_new

    n_past = lax.shift_right_logical(i, 1)

    def pair(t, carry):
        tile(pl.multiple_of(2 * t * tk, tk), False)
        tile(pl.multiple_of((2 * t + 1) * tk, tk), False)
        return carry

    def single(t, carry):
        tile(pl.multiple_of((n_past - 1) * tk, tk), False)
        return carry

    lax.fori_loop(0, lax.shift_right_logical(n_past, 1), pair, 0)
    lax.fori_loop(0, n_past & 1, single, 0)
    tile(pl.multiple_of(n_past * tk, tk), True)
    o_s = acc_ref[...] / jnp.maximum(l_ref[...], 1e-30)

    start = pl.multiple_of(jnp.maximum(p0 - WINDOW, 0), Q_BLOCK)
    wk = WINDOW + tq
    sw = _dot(kw_ref[pl.ds(start, wk), :].astype(BF16), qbd)
    kpos = start + lax.broadcasted_iota(jnp.int32, (wk, 1), 0)
    dw = tpos - kpos
    pw = _softmax0(sw - slope * dw.astype(F32), (dw >= 0) & (dw <= WINDOW))
    o_w = _dot(vw_ref[pl.ds(start, wk), :].T.astype(BF16), pw.astype(BF16))

    gate = jax.nn.sigmoid(gl_ref[...].T)
    for g in range(NSA_GROUP):
        halves = []
        for e in range(ne):
            c0 = (g * ne + e) * tq
            r0 = e * HEAD_DIM
            x = None
            for br, o in enumerate((o_c, o_s, o_w)):
                hrow = ((2 * p + e) * NSA_GROUP + g) * 3 + br
                gt = _dyn_row(gate, hrow)
                term = gt * o[r0:r0 + HEAD_DIM, c0:c0 + tq]
                x = term if x is None else x + term
            ms = jnp.mean(x * x, axis=0, keepdims=True)
            halves.append(x * lax.rsqrt(ms + NORM_EPS) * gain_ref[0, g * ne + e])
        o_ref[:, g * LANES:(g + 1) * LANES] = jnp.concatenate(halves, axis=0).T.astype(o_ref.dtype)


def _nsa_prompt(q, gl, kvn, winn, kvc, slopes, gain, pair01, n, seq):
    nq = seq // Q_BLOCK
    n_cmp = seq // CMP_BLOCK
    n_sel = seq // SEL_BLOCK
    ncol = NSA_GROUP * 2 * Q_BLOCK
    kern = functools.partial(_nsa_prompt_kernel, seq=seq)
    return pl.pallas_call(
        kern,
        grid=(n, nq, 2),
        in_specs=[
            pl.BlockSpec((Q_BLOCK, 512), lambda b, i, p: (b * nq + i, p)),
            pl.BlockSpec((Q_BLOCK, LANES), lambda b, i, p: (b * nq + i, 0)),
            pl.BlockSpec((seq, LANES), lambda b, i, p: (b, 4 + p)),
            pl.BlockSpec((seq, LANES), lambda b, i, p: (b, 6 + p)),
            pl.BlockSpec((seq, LANES), lambda b, i, p: (b, p)),
            pl.BlockSpec((seq, LANES), lambda b, i, p: (b, 2 + p)),
            pl.BlockSpec((n_cmp, LANES), lambda b, i, p: (b, p)),
            pl.BlockSpec((n_cmp, LANES), lambda b, i, p: (b, 2 + p)),
            pl.BlockSpec((1, 1, ncol), lambda b, i, p: (p, 0, 0)),
            pl.BlockSpec((1, 8, HEAD_DIM, LANES), lambda b, i, p: (p, 0, 0, 0)),
            pl.BlockSpec((n_sel, n_cmp), lambda b, i, p: (0, 0)),
        ],
        out_specs=pl.BlockSpec((Q_BLOCK, 512), lambda b, i, p: (b * nq + i, p)),
        out_shape=jax.ShapeDtypeStruct((n * seq, NSA_WIDTH), BF16),
        scratch_shapes=[pltpu.VMEM((1, ncol), F32), pltpu.VMEM((1, ncol), F32), pltpu.VMEM((LANES, ncol), F32),
                        pltpu.VMEM((2 * Q_BLOCK, ncol), F32)],
        compiler_params=_cparams("parallel", "parallel", "parallel"),
        name="nsa_prompt",
    )(q, gl, kvn, kvn, winn, winn, kvc, kvc, slopes, gain, pair01)


def _slope_cols(ne, tq, pair):
    g, e, t = jnp.meshgrid(jnp.arange(NSA_GROUP), jnp.arange(ne), jnp.arange(tq), indexing="ij")
    head = (pair * ne + e) * NSA_GROUP + g
    return _alibi_slopes()[head].reshape(1, -1)


def _gain_cols(gain):
    gh = gain.reshape(2, 2, NSA_GROUP, HEAD_DIM).transpose(0, 2, 1, 3).reshape(2, 8, HEAD_DIM)
    return jnp.broadcast_to(gh[..., None], (2, 8, HEAD_DIM, LANES))


def _pair01_t(n_cmp):
    return (jnp.arange(n_cmp // 2)[:, None] == jnp.arange(n_cmp)[None, :] // 2).astype(BF16)


def _pad_rows(x, rows):
    return jnp.concatenate([x, jnp.zeros((rows - x.shape[0], x.shape[1]), x.dtype)], axis=0)


def _nsa_sample_kernel(pt_ref, q_ref, gl_ref, kvn_ref, winn_ref, wst_ref, *rest, past_len):
    del pt_ref
    n_pages = past_len // PAGE_SIZE
    page_refs = rest[:n_pages]
    kc_refs = rest[n_pages:2 * n_pages]
    (slope_ref, gain_ref, valid_ref, pair_ref, sumg_ref, tile_ref, gsel_ref,
     o_ref, nw_ref, kc_s, vc_s) = rest[2 * n_pages:]
    ne, tq = NSA_KV_HEADS, 8
    width = ne * HEAD_DIM
    ncol = NSA_GROUP * ne * tq
    blocks_per_page = PAGE_SIZE // CMP_BLOCK

    qf = q_ref[...]
    lane_w = lax.broadcasted_iota(jnp.int32, (1, width), 1)
    parts = []
    for g in range(NSA_GROUP):
        slab = jnp.concatenate([qf[:, g * LANES:(g + 1) * LANES],
                                qf[:, 4 * LANES + g * LANES:4 * LANES + (g + 1) * LANES]], axis=1)
        for e in range(ne):
            parts.append(jnp.where((lane_w >= e * HEAD_DIM) & (lane_w < (e + 1) * HEAD_DIM), slab, 0.0))
    qbd = jnp.concatenate(parts, axis=0).T.astype(BF16)

    lane = lax.broadcasted_iota(jnp.int32, (1, ncol), 1)
    tloc = lane & (tq - 1)
    tpos = past_len + tloc
    slope = slope_ref[...]

    kc_s[...] = jnp.zeros_like(kc_s)
    vc_s[...] = jnp.zeros_like(vc_s)
    for j in range(n_pages):
        for b in range(blocks_per_page):
            c = j * blocks_per_page + b
            kc_s[c:c + 1, :] = kc_refs[j][0, :, b * 2 * KV_WIDTH:b * 2 * KV_WIDTH + KV_WIDTH]
            vc_s[c:c + 1, :] = kc_refs[j][0, :, b * 2 * KV_WIDTH + KV_WIDTH:(b + 1) * 2 * KV_WIDTH]
    ncp = kc_s.shape[0]
    s = _dot(kc_s[...].astype(BF16), qbd)
    cmp_end = (lax.broadcasted_iota(jnp.int32, (ncp, 1), 0) + 1) * CMP_BLOCK - 1
    dist = (tpos - cmp_end).astype(F32)
    pc = _softmax0(s - slope * dist, dist >= 0)
    o_c = _dot(vc_s[...].T.astype(BF16), pc.astype(BF16))

    psum = _dot_exact01(pc, sumg_ref[...])
    imp = _dot01_left(pair_ref[...], psum)
    n_sel_past = past_len // SEL_BLOCK
    sel = _select_blocks_t(imp, jnp.right_shift(tpos, 6), n_sel_past + 1).astype(BF16)
    sel = _dot(sel, tile_ref[...]).astype(BF16)

    ksub = lax.broadcasted_iota(jnp.int32, (PAGE_SIZE, 1), 0)
    jrow = lax.broadcasted_iota(jnp.int32, (1, sel.shape[0]), 1)
    b0 = slope * (tloc - ksub).astype(F32)

    def scores(kt, k0, causal):
        st = _dot(kt.astype(BF16), qbd)
        expand = jnp.where(jnp.right_shift(k0 + ksub, 6) == jrow, 1.0, 0.0).astype(BF16)
        chosen = _dot(expand, sel) > 0.5
        if causal:
            chosen = chosen & ((tloc - ksub + (past_len - k0)) >= 0)
        return st - b0 - slope * float(past_len - k0) + jnp.where(chosen, 0.0, -jnp.inf)

    knew = _pad_rows(kvn_ref[:, 2 * KV_WIDTH:3 * KV_WIDTH], PAGE_SIZE)
    vnew = _pad_rows(kvn_ref[:, 3 * KV_WIDTH:4 * KV_WIDTH], PAGE_SIZE)
    s_all = [scores(page_refs[j][:, 0:KV_WIDTH], j * PAGE_SIZE, False) for j in range(n_pages)]
    s_all.append(scores(knew, past_len, True))
    m = s_all[0]
    for sj in s_all[1:]:
        m = jnp.maximum(m, sj)
    m = jnp.maximum(jnp.max(m, axis=0, keepdims=True), NEG)
    l = jnp.zeros((1, ncol), F32)
    acc = jnp.zeros((width, ncol), F32)
    for j, sj in enumerate(s_all):
        pt = jnp.exp(sj - m)
        vt = page_refs[j][:, KV_WIDTH:2 * KV_WIDTH] if j < n_pages else vnew
        l = l + jnp.sum(pt, axis=0, keepdims=True)
        acc = acc + _dot(vt.T.astype(BF16), pt.astype(BF16))
    o_s = acc / jnp.maximum(l, 1e-30)

    kw = jnp.concatenate([wst_ref[0, :, 0:KV_WIDTH], _pad_rows(winn_ref[:, 0:KV_WIDTH], PAGE_SIZE)], axis=0)
    vw = jnp.concatenate([wst_ref[0, :, KV_WIDTH:], _pad_rows(winn_ref[:, KV_WIDTH:], PAGE_SIZE)], axis=0)
    start = past_len - WINDOW
    sw = _dot(kw.astype(BF16), qbd)
    kpos = start + lax.broadcasted_iota(jnp.int32, (kw.shape[0], 1), 0)
    dw = tpos - kpos
    pw = _softmax0(sw - slope * dw.astype(F32), (dw >= 0) & (dw <= WINDOW))
    o_w = _dot(vw.T.astype(BF16), pw.astype(BF16))

    sg = jax.nn.sigmoid(gl_ref[...])
    trow = lax.broadcasted_iota(jnp.int32, (tq, 1), 0)
    x = None
    for br, o in enumerate((o_c, o_s, o_w)):
        spread = _dot_exact01(sg, gsel_ref[br])
        grow = jnp.sum(jnp.where(trow == tloc, spread, 0.0), axis=0, keepdims=True)
        x = grow * o if x is None else x + grow * o
    x = jnp.where(valid_ref[...] > 0.5, x, 0.0)
    ms = jnp.sum(x * x, axis=0, keepdims=True) * (1.0 / HEAD_DIM)
    y = (x * lax.rsqrt(ms + NORM_EPS) * gain_ref[...]).T
    for g in range(NSA_GROUP):
        og = y[g * ne * tq:g * ne * tq + tq]
        for e in range(1, ne):
            og = og + y[(g * ne + e) * tq:(g * ne + e + 1) * tq]
        o_ref[:, g * LANES:(g + 1) * LANES] = og[:, 0:LANES]
        o_ref[:, (NSA_GROUP + g) * LANES:(NSA_GROUP + g + 1) * LANES] = og[:, LANES:]

    w_len = wst_ref.shape[1]
    nw_ref[0, 0:w_len - tq, :] = wst_ref[0, tq:w_len, :]
    nw_ref[0, w_len - tq:w_len, :] = winn_ref[...]


def _sample_consts(gain):
    ne, tq = NSA_KV_HEADS, 8
    c = jnp.arange(NSA_GROUP * ne * tq)
    cg, ck = c // (ne * tq), (c // tq) % ne
    chead = ck * NSA_GROUP + cg
    r = jnp.arange(ne * HEAD_DIM)
    rk = r // HEAD_DIM
    slope = _alibi_slopes()[chead].reshape(1, -1)
    valid = (rk[:, None] == ck[None, :]).astype(F32)
    gain_t = gain.reshape(ne, NSA_GROUP, HEAD_DIM).transpose(0, 2, 1).reshape(ne * HEAD_DIM, NSA_GROUP)
    gain_t = jnp.repeat(gain_t, ne * tq, axis=1) * valid
    lanes = jnp.arange(LANES)
    sumg = ((lanes[None, :] < ne * tq) & (c[:, None] % (ne * tq) == lanes[None, :])).astype(BF16)
    tile = ((lanes[:, None] < ne * tq) & (lanes[:, None] == c[None, :] % (ne * tq))).astype(BF16)
    gsel = jnp.stack([(lanes[:, None] == (chead * 3 + br)[None, :]).astype(BF16) for br in range(3)])
    return slope, gain_t, valid, sumg, tile, gsel


def _nsa_sample(page_table, q, gl, kvn, winn, win_state, cache_rows, kvc_pages, consts, pair01, row0):
    slope, gain_t, valid, sumg, tile, gsel = consts
    n_seq, n_pages = page_table.shape
    past_len = n_pages * PAGE_SIZE
    w_len = win_state.shape[1]
    blk0 = row0 // 8
    tok = lambda s, pt: (blk0 + s, 0)
    full = lambda a: pl.BlockSpec(a.shape, lambda s, pt: (0,) * a.ndim)
    in_specs = [
        pl.BlockSpec((8, NSA_WIDTH), tok),
        pl.BlockSpec((8, LANES), tok),
        pl.BlockSpec((8, 4 * KV_WIDTH), tok),
        pl.BlockSpec((8, 2 * KV_WIDTH), tok),
        pl.BlockSpec((1, w_len, 2 * KV_WIDTH), lambda s, pt: (s, 0, 0)),
    ]
    in_specs += [pl.BlockSpec((PAGE_SIZE, 2 * KV_WIDTH), functools.partial(lambda s, pt, j: (pt[s, j], 1), j=j))
                 for j in range(n_pages)]
    in_specs += [pl.BlockSpec((1, 1, 8 * KV_WIDTH), functools.partial(lambda s, pt, j: (pt[s, j], 0, 0), j=j))
                 for j in range(n_pages)]
    in_specs += [full(slope), full(gain_t), full(valid), full(pair01), full(sumg), full(tile), full(gsel)]
    ncp = -(-(past_len // CMP_BLOCK) // LANES) * LANES
    grid_spec = pltpu.PrefetchScalarGridSpec(
        num_scalar_prefetch=1,
        grid=(n_seq,),
        in_specs=in_specs,
        out_specs=[pl.BlockSpec((8, NSA_WIDTH), lambda s, pt: (s, 0)),
                   pl.BlockSpec((1, w_len, 2 * KV_WIDTH), lambda s, pt: (s, 0, 0))],
        scratch_shapes=[pltpu.VMEM((ncp, KV_WIDTH), F32), pltpu.VMEM((ncp, KV_WIDTH), F32)],
    )
    return pl.pallas_call(
        functools.partial(_nsa_sample_kernel, past_len=past_len),
        grid_spec=grid_spec,
        out_shape=[jax.ShapeDtypeStruct((n_seq * 8, NSA_WIDTH), F32),
                   jax.ShapeDtypeStruct((n_seq, w_len, 2 * KV_WIDTH), F32)],
        compiler_params=_cparams("parallel"),
        name="nsa_sample",
    )(page_table, q, gl, kvn, winn, win_state, *([cache_rows] * n_pages), *([kvc_pages] * n_pages),
      slope, gain_t, valid, pair01, sumg, tile, gsel)


def _head_sums(x, bo01):
    return jnp.concatenate([_dot_exact01(x[:, s * LANES:(s + 1) * LANES], bo01)
                            for s in range(x.shape[1] // LANES)], axis=1)


def _rwkv_prep_kernel(z_ref, first_ref, mu_ref, w0_ref, a0_ref, kk_ref, ka_ref, rk_ref, w2_ref, a2_ref, g2_ref,
                      bo_ref, r_o, w_o, k_o, v_o, kk_o, nb_o, g_o, bonus_o, *, seq_rows):
    z = z_ref[...]
    tm = z.shape[0]
    row = lax.broadcasted_iota(jnp.int32, (tm, 1), 0)
    if seq_rows >= tm:
        at_start = pl.program_id(0) % (seq_rows // tm) == 0
        before = jnp.where(at_start, 0.0, first_ref[7:8, :])
        prev = jnp.where(row == 0, before, pltpu.roll(z, 1, 0))
    else:
        prev = jnp.where((row & (seq_rows - 1)) == 0, first_ref[...], pltpu.roll(z, 1, 0))
    zm = z + (prev - z) * mu_ref[...]
    w = RWKV_WIDTH
    r, k, v = zm[:, 0:w], zm[:, w:2 * w], zm[:, 2 * w:3 * w]
    wa = zm[:, 3 * w:3 * w + LANES]
    gd = zm[:, 3 * w + LANES:]
    wl = w0_ref[...] + _dot(jnp.tanh(wa).astype(BF16), w2_ref[...])
    neg = -wl
    softplus = jnp.maximum(neg, 0.0) + jnp.log1p(jnp.exp(-jnp.abs(neg)))
    decay = jnp.exp(-jnp.exp(-softplus - 0.5))
    a = jax.nn.sigmoid(a0_ref[...] + _dot(wa.astype(BF16), a2_ref[...]))
    g = _dot(jax.nn.sigmoid(gd).astype(BF16), g2_ref[...])
    bo = bo_ref[...]
    kk = k * kk_ref[...]
    kk = kk / jnp.maximum(jnp.sqrt(_head_sums(kk * kk, bo)), 1e-12)
    k2 = k * (1.0 + (a - 1.0) * ka_ref[...])
    r_o[...] = r
    w_o[...] = decay
    k_o[...] = k2
    v_o[...] = v
    kk_o[...] = kk
    nb_o[...] = -(kk * a)
    g_o[...] = g
    bonus_o[...] = _head_sums(r * k2 * rk_ref[...], bo) * v


def _rwkv_prep(z, first, row0, rows, tm, seq_rows, mu, w0, a0, k_k, k_a, r_k, w2p, a2p, g2p, bo01):
    blk0 = row0 // tm
    vec = lambda a: a.reshape(1, -1)
    cvec = pl.BlockSpec((1, RWKV_WIDTH), lambda i: (0, 0))
    if seq_rows >= tm:
        first_spec = pl.BlockSpec((8, Z_PAD), lambda i: (jnp.maximum((blk0 + i) * (tm // 8) - 1, 0), 0))
    else:
        first_spec = pl.BlockSpec((tm, Z_PAD), lambda i: (i, 0))
    out = jax.ShapeDtypeStruct((rows, RWKV_WIDTH), F32)
    ospec = pl.BlockSpec((tm, RWKV_WIDTH), lambda i: (i, 0))
    return pl.pallas_call(
        functools.partial(_rwkv_prep_kernel, seq_rows=seq_rows),
        grid=(rows // tm,),
        in_specs=[pl.BlockSpec((tm, Z_PAD), lambda i: (blk0 + i, 0)), first_spec,
                  pl.BlockSpec((1, Z_PAD), lambda i: (0, 0)), cvec, cvec, cvec, cvec, cvec,
                  pl.BlockSpec((LANES, RWKV_WIDTH), lambda i: (0, 0)),
                  pl.BlockSpec((LANES, RWKV_WIDTH), lambda i: (0, 0)),
                  pl.BlockSpec((2 * LANES, RWKV_WIDTH), lambda i: (0, 0)),
                  pl.BlockSpec((LANES, LANES), lambda i: (0, 0))],
        out_specs=[ospec] * 8,
        out_shape=[out] * 8,
        compiler_params=_cparams("parallel"),
        name="rwkv_prep",
    )(z, first, vec(mu), vec(w0), vec(a0), vec(k_k), vec(k_a), vec(r_k), w2p, a2p, g2p, bo01)


def _rwkv_scan_kernel(*refs, nbatch, tc, has_init):
    ops = refs[:6]
    pos = 6
    s0_ref = refs[pos] if has_init else None
    pos += 1 if has_init else 0
    pat_ref, bo_ref, y_ref, s_out, st = refs[pos:pos + 5]
    c = pl.program_id(1)
    nq = RWKV_HEADS // 4
    tw = 4 * HEAD_DIM

    @pl.when(c == 0)
    def _():
        if has_init:
            st[...] = s0_ref[...]
        else:
            st[...] = jnp.zeros_like(st)

    pat = pat_ref[...]
    bo = bo_ref[...]

    def step(t, carry):
        tiles = [(b, q) for b in range(nbatch) for q in range(nq)]
        rowv = lambda k, b, q: ops[k][b, pl.ds(t, 1), q * tw:(q + 1) * tw]
        lhs = []
        for b, q in tiles:
            s = st[b, q]
            lhs.append((s * rowv(4, b, q)).astype(BF16))
            lhs.append((pat * rowv(3, b, q)).astype(BF16))
        red = _dot(jnp.concatenate(lhs, axis=0), bo)
        outs = []
        for n, (b, q) in enumerate(tiles):
            sa = red[n * 2 * HEAD_DIM:n * 2 * HEAD_DIM + HEAD_DIM]
            vb = red[n * 2 * HEAD_DIM + HEAD_DIM:(n + 1) * 2 * HEAD_DIM]
            s2 = st[b, q] * rowv(1, b, q) + sa * rowv(5, b, q) + vb * rowv(2, b, q)
            st[b, q] = s2
            outs.append((s2 * rowv(0, b, q)).astype(BF16))
        yb = _dot(jnp.concatenate(outs, axis=0), bo)
        for n, (b, q) in enumerate(tiles):
            yrow = jnp.sum(yb[n * HEAD_DIM:(n + 1) * HEAD_DIM] * pat, axis=0, keepdims=True)
            y_ref[b, pl.ds(t, 1), q * tw:(q + 1) * tw] = yrow
        return carry

    lax.fori_loop(0, tc, step, 0, unroll=4)

    @pl.when(c == pl.num_programs(1) - 1)
    def _():
        s_out[...] = st[...]


def _rwkv_scan(ops, s0, nseq, seq_rows, nbatch, tc, pat, bo01):
    nq = RWKV_HEADS // 4
    tok_spec = pl.BlockSpec((nbatch, tc, RWKV_WIDTH), lambda gi, c: (gi, c, 0))
    st_spec = pl.BlockSpec((nbatch, nq, HEAD_DIM, 4 * HEAD_DIM), lambda gi, c: (gi, 0, 0, 0))
    in_specs = [tok_spec] * 6
    args = list(ops)
    if s0 is not None:
        in_specs.append(st_spec)
        args.append(s0)
    in_specs += [pl.BlockSpec((HEAD_DIM, 4 * HEAD_DIM), lambda gi, c: (0, 0)),
                 pl.BlockSpec((4 * HEAD_DIM, 4 * HEAD_DIM), lambda gi, c: (0, 0))]
    args += [pat, bo01]
    return pl.pallas_call(
        functools.partial(_rwkv_scan_kernel, nbatch=nbatch, tc=tc, has_init=s0 is not None),
        grid=(nseq // nbatch, seq_rows // tc),
        in_specs=in_specs,
        out_specs=[tok_spec, st_spec],
        out_shape=[jax.ShapeDtypeStruct((nseq, seq_rows, RWKV_WIDTH), F32),
                   jax.ShapeDtypeStruct((nseq, nq, HEAD_DIM, 4 * HEAD_DIM), F32)],
        scratch_shapes=[pltpu.VMEM((nbatch, nq, HEAD_DIM, 4 * HEAD_DIM), F32)],
        compiler_params=_cparams("parallel", "arbitrary"),
        name="rwkv_scan",
    )(*args)


def _rwkv_post_kernel(y_ref, g_ref, bonus_ref, lnw_ref, lnb_ref, bo_ref, o_ref):
    y = y_ref[...]
    bo = bo_ref[...]
    d = y - _head_sums(y, bo) * (1.0 / HEAD_DIM)
    var = _head_sums(d * d, bo) * (1.0 / HEAD_DIM)
    o = (d * lax.rsqrt(var + GN_EPS) * lnw_ref[...] + lnb_ref[...] + bonus_ref[...]) * g_ref[...]
    o_ref[...] = o.astype(o_ref.dtype)


def _rwkv_post(y, g, bonus, ln_w, ln_b, bo01, tm):
    rows = y.shape[0]
    tok = pl.BlockSpec((tm, RWKV_WIDTH), lambda i: (i, 0))
    cvec = pl.BlockSpec((1, RWKV_WIDTH), lambda i: (0, 0))
    return pl.pallas_call(
        _rwkv_post_kernel,
        grid=(rows // tm,),
        in_specs=[tok, tok, tok, cvec, cvec, pl.BlockSpec((LANES, LANES), lambda i: (0, 0))],
        out_specs=tok,
        out_shape=jax.ShapeDtypeStruct((rows, RWKV_WIDTH), BF16),
        compiler_params=_cparams("parallel"),
        name="rwkv_post",
    )(y, g, bonus, ln_w.reshape(1, -1), ln_b.reshape(1, -1), bo01)


def _rwkv_consts(w2, a2, g2):
    w2p = jnp.concatenate([w2, jnp.zeros((LANES - DECAY_LORA, RWKV_WIDTH), w2.dtype)], axis=0)
    a2p = jnp.concatenate([jnp.zeros((DECAY_LORA, RWKV_WIDTH), a2.dtype), a2], axis=0)
    g2p = jnp.concatenate([g2, jnp.zeros((2 * LANES - GATE_LORA, RWKV_WIDTH), g2.dtype)], axis=0)
    v = jnp.arange(HEAD_DIM)
    pat = (v[:, None] == (jnp.arange(4 * HEAD_DIM)[None, :] % HEAD_DIM)).astype(F32)
    return w2p.astype(BF16), a2p.astype(BF16), g2p.astype(BF16), pat


def _state_to_tiles(s):
    n = s.shape[0]
    return s.reshape(n, 4, 4, HEAD_DIM, HEAD_DIM).transpose(0, 1, 3, 2, 4).reshape(n, 4, HEAD_DIM, 4 * HEAD_DIM)


def _tiles_to_state(t):
    n = t.shape[0]
    return t.reshape(n, 4, HEAD_DIM, 4, HEAD_DIM).transpose(0, 1, 3, 2, 4).reshape(n, RWKV_HEADS, HEAD_DIM, HEAD_DIM)


def _outproj_kernel(xp_ref, xs_ref, on_ref, or_ref, wa_ref, wb_ref, g_ref, rwh_ref, rwl_ref, rb_ref,
                    x1_ref, h2_ref, te_ref, tg_ref, *, n_p):
    x1 = _rows_of(xp_ref, xs_ref, n_p) + _dot(on_ref[...], wa_ref[...]) + _dot(or_ref[...], wb_ref[...])
    x1_ref[...] = x1
    ms = jnp.mean(x1 * x1, axis=-1, keepdims=True)
    hf = x1 * lax.rsqrt(ms + NORM_EPS) * g_ref[...]
    h2_ref[...] = hf
    hh = hf.astype(BF16)
    hl = (hf - hh.astype(F32)).astype(BF16)
    logits = _dot(hh, rwh_ref[...]) + _dot(hl, rwh_ref[...]) + _dot(hh, rwl_ref[...]) + rb_ref[...]
    lane = lax.broadcasted_iota(jnp.int32, logits.shape, 1).astype(F32)
    vals, idxs = [], []
    for _ in range(MOE_TOPK):
        m = jnp.max(logits, axis=-1, keepdims=True)
        idx = jnp.min(jnp.where(logits == m, lane, float(LANES)), axis=-1, keepdims=True)
        vals.append(m)
        idxs.append(idx)
        logits = jnp.where(lane == idx, -jnp.inf, logits)
    es = [jnp.exp(v - vals[0]) for v in vals]
    denom = es[0] + es[1] + es[2] + es[3]
    te = jnp.zeros(logits.shape, F32)
    tg = jnp.zeros(logits.shape, F32)
    for k in range(MOE_TOPK):
        te = jnp.where(lane == float(k), idxs[k], te)
        tg = jnp.where(lane == float(k), es[k] / denom, tg)
    te_ref[...] = te.astype(jnp.int32)
    tg_ref[...] = tg


def _outproj_router(xp, xs, o_nsa, o_rwkv, wa, wb, g, rw_hi, rw_lo, rb, tm=256):
    t = xp.shape[0] + xs.shape[0]
    n_p = xp.shape[0] // tm
    tok = lambda w: pl.BlockSpec((tm, w), lambda i: (i, 0))
    full = lambda a: pl.BlockSpec(a.shape, lambda i: (0,) * a.ndim)
    return pl.pallas_call(
        functools.partial(_outproj_kernel, n_p=n_p),
        grid=(t // tm,),
        in_specs=_row_specs(tm, n_p, D_MODEL) + [tok(NSA_WIDTH), tok(RWKV_WIDTH), full(wa), full(wb), full(g),
                                                 full(rw_hi), full(rw_lo), full(rb)],
        out_specs=[tok(D_MODEL), tok(D_MODEL), tok(LANES), tok(LANES)],
        out_shape=[jax.ShapeDtypeStruct((t, D_MODEL), F32), jax.ShapeDtypeStruct((t, D_MODEL), F32),
                   jax.ShapeDtypeStruct((t, LANES), jnp.int32), jax.ShapeDtypeStruct((t, LANES), F32)],
        compiler_params=_cparams("parallel"),
        name="outproj_router",
    )(xp, xs, o_nsa, o_rwkv, wa, wb, g, rw_hi, rw_lo, rb)


MOE_BM = 256


def _route(top_e, bm):
    n_tok = top_e.shape[0]
    flat_e = top_e.reshape(-1)
    n_assign = flat_e.shape[0]
    onehot = (flat_e[:, None] == jnp.arange(N_EXPERTS, dtype=jnp.int32)[None, :]).astype(jnp.int32)
    cum = jnp.cumsum(onehot, axis=0)
    counts = cum[-1]
    pos = jnp.take_along_axis(cum, flat_e[:, None], axis=1)[:, 0] - 1
    padded = (counts + bm - 1) // bm * bm
    pad_end = jnp.cumsum(padded)
    dest = (pad_end - padded)[flat_e] + pos
    n_blocks = -(-n_assign // bm) + N_EXPERTS
    slot_tok = jnp.zeros((n_blocks * bm,), jnp.int32).at[dest].set(jnp.arange(n_assign, dtype=jnp.int32) // MOE_TOPK)
    n_used = (pad_end[-1] // bm).astype(jnp.int32)
    blk = jnp.minimum(jnp.arange(n_blocks, dtype=jnp.int32), n_used - 1)
    blk_e = jnp.minimum(jnp.sum((pad_end[None, :] <= (blk * bm)[:, None]).astype(jnp.int32), axis=1), N_EXPERTS - 1)
    return slot_tok, dest.reshape(n_tok, MOE_TOPK).astype(jnp.int32), blk_e, n_used.reshape(1), n_blocks


def _row_copy(src_ref, dst_ref, sem, src_row, dst_row):
    return pltpu.make_async_copy(src_ref.at[pl.ds(src_row, 1)], dst_ref.at[pl.ds(dst_row, 1)], sem)


def _gather_issue(idx_ref, src_ref, buf, sem):
    n = buf.shape[0]

    def issue(r2, c):
        for par in range(2):
            r = 2 * r2 + par
            _row_copy(src_ref, buf, sem, idx_ref[0, 0, r], r).start(priority=par)
        return c

    lax.fori_loop(0, n // 2, issue, 0, unroll=4)


def _gather_wait(src_ref, buf, sem):
    pltpu.make_async_copy(src_ref.at[pl.ds(0, buf.shape[0])], buf, sem).wait()


def _gather_pipelined(cur_ref, nxt_ref, src_ref, buf, sem, b, nsteps):
    slot = b & 1

    @pl.when(b == 0)
    def _():
        _gather_issue(cur_ref, src_ref, buf.at[0], sem.at[0])

    @pl.when(b + 1 < nsteps)
    def _():
        _gather_issue(nxt_ref, src_ref, buf.at[1 - slot], sem.at[1 - slot])

    _gather_wait(src_ref, buf.at[slot], sem.at[slot])
    return slot


MOE_FT = 1024
MOE_NF = D_FF // MOE_FT


def _expert_changed(be_ref, b):
    prev = be_ref[jnp.maximum(b - 1, 0)]
    return (b == 0) | (be_ref[b] != prev)


def _moe_up_kernel(be_ref, nu_ref, cur_ref, nxt_ref, h_ref, wg_ref, wl_ref, bg_ref, bl_ref, o_ref, wg_s, wl_s,
                   buf, sem, *, nblk):
    f = pl.program_id(0)
    b = pl.program_id(1)
    slot = _gather_pipelined(cur_ref, nxt_ref, h_ref, buf, sem, f * nblk + b, MOE_NF * nblk)

    @pl.when(b < nu_ref[0])
    def _():
        @pl.when(_expert_changed(be_ref, b))
        def _():
            wg_s[...] = wg_ref[0].astype(BF16)
            wl_s[...] = wl_ref[0].astype(BF16)

        x = buf[slot].astype(BF16)
        glu = jnp.minimum(_dot(x, wg_s[...]) + bg_ref[0], SWIGLU_LIMIT)
        lin = jnp.clip(_dot(x, wl_s[...]) + bl_ref[0], -SWIGLU_LIMIT, SWIGLU_LIMIT)
        o_ref[...] = (glu * jax.nn.sigmoid(SWIGLU_ALPHA * glu) * (lin + 1.0)).astype(o_ref.dtype)

    @pl.when(b >= nu_ref[0])
    def _():
        o_ref[...] = jnp.zeros_like(o_ref)


def _moe_up(blk_e, n_used, slot_tok, h2, w1, b1, bm):
    n_slots = slot_tok.shape[0]
    nblk = n_slots // bm
    idx = slot_tok.reshape(nblk, 1, bm)
    grid_spec = pltpu.PrefetchScalarGridSpec(
        num_scalar_prefetch=2,
        grid=(MOE_NF, nblk),
        in_specs=[
            pl.BlockSpec((1, 1, bm), lambda f, b, be, nu: (b, 0, 0), memory_space=pltpu.SMEM),
            pl.BlockSpec((1, 1, bm), lambda f, b, be, nu: (jnp.where(b + 1 < nblk, b + 1, 0), 0, 0),
                         memory_space=pltpu.SMEM),
            pl.BlockSpec(memory_space=pl.ANY),
            pl.BlockSpec((1, D_MODEL, MOE_FT), lambda f, b, be, nu: (be[b], 0, f)),
            pl.BlockSpec((1, D_MODEL, MOE_FT), lambda f, b, be, nu: (be[b], 0, MOE_NF + f)),
            pl.BlockSpec((1, 1, MOE_FT), lambda f, b, be, nu: (be[b], 0, f)),
            pl.BlockSpec((1, 1, MOE_FT), lambda f, b, be, nu: (be[b], 0, MOE_NF + f)),
        ],
        out_specs=pl.BlockSpec((bm, MOE_FT), lambda f, b, be, nu: (b, f)),
        scratch_shapes=[pltpu.VMEM((D_MODEL, MOE_FT), BF16), pltpu.VMEM((D_MODEL, MOE_FT), BF16),
                        pltpu.VMEM((2, bm, D_MODEL), F32), pltpu.SemaphoreType.DMA((2,))],
    )
    return pl.pallas_call(
        functools.partial(_moe_up_kernel, nblk=nblk),
        grid_spec=grid_spec,
        out_shape=jax.ShapeDtypeStruct((n_slots, D_FF), BF16),
        compiler_params=_cparams("arbitrary", "arbitrary"),
        name="moe_up",
    )(blk_e, n_used, idx, idx, h2, w1, w1, b1, b1)


def _moe_down_kernel(be_ref, nu_ref, a_ref, w_ref, b_ref, o_ref, w_s):
    b = pl.program_id(0)

    @pl.when(b < nu_ref[0])
    def _():
        @pl.when(_expert_changed(be_ref, b))
        def _():
            w_s[...] = w_ref[0].astype(BF16)

        o_ref[...] = _dot(a_ref[...], w_s[...]) + b_ref[0]

    @pl.when(b >= nu_ref[0])
    def _():
        o_ref[...] = jnp.zeros_like(o_ref)


def _moe_down(blk_e, n_used, act, w2, b2, bm):
    n_slots = act.shape[0]
    nblk = n_slots // bm
    live = lambda b, nu: jnp.minimum(b, nu[0] - 1)
    grid_spec = pltpu.PrefetchScalarGridSpec(
        num_scalar_prefetch=2,
        grid=(nblk,),
        in_specs=[
            pl.BlockSpec((bm, D_FF), lambda b, be, nu: (live(b, nu), 0)),
            pl.BlockSpec((1, D_FF, D_MODEL), lambda b, be, nu: (be[b], 0, 0)),
            pl.BlockSpec((1, 1, D_MODEL), lambda b, be, nu: (be[b], 0, 0)),
        ],
        out_specs=pl.BlockSpec((bm, D_MODEL), lambda b, be, nu: (b, 0)),
        scratch_shapes=[pltpu.VMEM((D_FF, D_MODEL), BF16)],
    )
    return pl.pallas_call(
        _moe_down_kernel,
        grid_spec=grid_spec,
        out_shape=jax.ShapeDtypeStruct((n_slots, D_MODEL), F32),
        compiler_params=_cparams("arbitrary"),
        name="moe_down",
    )(blk_e, n_used, act, w2, b2)


def _combine_kernel(cur_ref, nxt_ref, x_ref, tg_ref, g_ref, oh_ref, yb_ref, o_ref, buf, sem, *, nsteps):
    tm = x_ref.shape[0]
    slot = _gather_pipelined(cur_ref, nxt_ref, yb_ref, buf, sem, pl.program_id(0), nsteps)
    x = x_ref[...]
    gates = tg_ref[...]
    for k in range(MOE_TOPK):
        gk = _dot_exact01(gates, oh_ref[k])
        x = x + jnp.concatenate([gk] * (D_MODEL // LANES), axis=1) * buf[slot, k * tm:(k + 1) * tm]
    ms = jnp.mean(x * x, axis=-1, keepdims=True)
    o_ref[...] = x * lax.rsqrt(ms + NORM_EPS) * g_ref[...]


def _moe_combine(slot_of, x1, tg, g, yb, row0, t, tm=128):
    nt = t // tm
    blk0 = row0 // tm
    slots = slot_of[row0:row0 + t].reshape(nt, tm, MOE_TOPK).transpose(0, 2, 1).reshape(nt, 1, MOE_TOPK * tm)
    onehot = (jnp.arange(LANES)[None, :, None] == jnp.arange(MOE_TOPK)[:, None, None]).astype(BF16)
    onehot = jnp.broadcast_to(onehot, (MOE_TOPK, LANES, LANES))
    return pl.pallas_call(
        functools.partial(_combine_kernel, nsteps=nt),
        grid=(nt,),
        in_specs=[pl.BlockSpec((1, 1, MOE_TOPK * tm), lambda i: (i, 0, 0), memory_space=pltpu.SMEM),
                  pl.BlockSpec((1, 1, MOE_TOPK * tm), lambda i: (jnp.minimum(i + 1, nt - 1), 0, 0),
                               memory_space=pltpu.SMEM),
                  pl.BlockSpec((tm, D_MODEL), lambda i: (blk0 + i, 0)),
                  pl.BlockSpec((tm, LANES), lambda i: (blk0 + i, 0)),
                  pl.BlockSpec((1, D_MODEL), lambda i: (0, 0)),
                  pl.BlockSpec((MOE_TOPK, LANES, LANES), lambda i: (0, 0, 0)),
                  pl.BlockSpec(memory_space=pl.ANY)],
        out_specs=pl.BlockSpec((tm, D_MODEL), lambda i: (i, 0)),
        out_shape=jax.ShapeDtypeStruct((t, D_MODEL), F32),
        scratch_shapes=[pltpu.VMEM((2, MOE_TOPK * tm, D_MODEL), F32), pltpu.SemaphoreType.DMA((2,))],
        compiler_params=_cparams("arbitrary"),
        name="moe_combine",
    )(slots, slots, x1, tg, g.reshape(1, -1), onehot, yb)


def _head_perm():
    p, g, e, d = jnp.meshgrid(jnp.arange(2), jnp.arange(NSA_GROUP), jnp.arange(2), jnp.arange(HEAD_DIM),
                              indexing="ij")
    return (((2 * p + e) * NSA_GROUP + g) * HEAD_DIM + d).reshape(-1)


def _alibi_slopes():
    return 2.0 ** (-8.0 * jnp.arange(1, NSA_HEADS + 1, dtype=F32) / NSA_HEADS)


def _block_ones(width):
    i = jnp.arange(width)
    return (i[:, None] // HEAD_DIM == i[None, :] // HEAD_DIM).astype(BF16)


def _cmp_weights_bd(wk, wv):
    eye = jnp.eye(2, dtype=wk.dtype)
    bd = lambda w: jnp.einsum("hg,bde->bhdge", eye, w).reshape(CMP_BLOCK, LANES, LANES)
    return jnp.stack([bd(wk), bd(wv)]).astype(BF16)


def _pad_cols(a, width):
    return jnp.pad(a, ((0, 0), (0, width - a.shape[1])))


def kernel(x_prompt, x_sample, cache_nsa_kv, state_win_kv, state_rwkv, state_shift, page_table, norm_attn, w_in,
           nsa_w_cmp_k, nsa_w_cmp_v, nsa_out_gain, rwkv_mu, rwkv_w0, rwkv_w2, rwkv_a0, rwkv_a2, rwkv_g2, rwkv_k_k,
           rwkv_k_a, rwkv_r_k, rwkv_ln_w, rwkv_ln_b, w_out, norm_ffn, router_w, router_b, moe_w1, moe_b1, moe_w2,
           moe_b2, norm_final):
    nb_p, seq_p, _ = x_prompt.shape
    nb_s, seq_s, _ = x_sample.shape
    depth = norm_attn.shape[0]
    assert depth == 1 and seq_s == 8 and seq_p % (2 * Q_BLOCK) == 0
    tp, ts = nb_p * seq_p, nb_s * seq_s
    n_phys = cache_nsa_kv.shape[1]
    w_len = state_win_kv.shape[2]
    l = 0
    xp, xs = x_prompt.reshape(tp, D_MODEL), x_sample.reshape(ts, D_MODEL)

    perm = _head_perm()
    w = w_in[l]
    c0, c1, c2 = NSA_WIDTH, NSA_WIDTH + 4 * KV_WIDTH, NSA_WIDTH + 6 * KV_WIDTH
    c3 = c2 + 3 * NSA_HEADS
    q, kvn, winn, gl, z = _in_proj(xp, xs, norm_attn[l], [
        (w[:, :c0] * (HEAD_DIM ** -0.5))[:, perm].astype(BF16), w[:, c0:c1].astype(BF16), w[:, c1:c2].astype(BF16),
        _pad_cols(w[:, c2:c3], LANES).astype(BF16), _pad_cols(w[:, c3:], Z_PAD).astype(BF16)])

    w_bd = _cmp_weights_bd(nsa_w_cmp_k[l], nsa_w_cmp_v[l])
    cache_rows = cache_nsa_kv[l].reshape(n_phys * PAGE_SIZE, 4 * KV_WIDTH)
    kvc_new = _compress(kvn, w_bd, tp // CMP_BLOCK, 256)
    blocks_phys = n_phys * (PAGE_SIZE // CMP_BLOCK)
    kvc_phys = _compress(cache_rows, w_bd, blocks_phys, 512)
    o_p = _nsa_prompt(q, gl, kvn, winn, kvc_new, jnp.stack([_slope_cols(2, Q_BLOCK, p) for p in range(2)]),
                      _gain_cols(nsa_out_gain[l]), _pair01_t(seq_p // CMP_BLOCK), nb_p, seq_p)
    n_cmp_pad = -(-(page_table.shape[1] * PAGE_SIZE // CMP_BLOCK) // LANES) * LANES
    o_s, new_win_s = _nsa_sample(
        page_table, q, gl, kvn, winn, state_win_kv[l].reshape(nb_s, w_len, 2 * KV_WIDTH), cache_rows,
        kvc_phys.reshape(n_phys, 1, (PAGE_SIZE // CMP_BLOCK) * 2 * KV_WIDTH),
        _sample_consts(nsa_out_gain[l]), _pair01_t(n_cmp_pad), tp)
    o_nsa = jnp.concatenate([o_p, o_s.astype(BF16)], axis=0)

    w2p, a2p, g2p, pat = _rwkv_consts(rwkv_w2[l], rwkv_a2[l], rwkv_g2[l])
    bo128 = _block_ones(LANES)
    mu = jnp.pad(rwkv_mu[l], (0, Z_PAD - SHIFT_WIDTH))
    vecs = (mu, rwkv_w0[l], rwkv_a0[l], rwkv_k_k[l], rwkv_k_a[l], rwkv_r_k[l], w2p, a2p, g2p, bo128)
    first_s = jnp.zeros((nb_s, seq_s, Z_PAD), F32).at[:, 0, :SHIFT_WIDTH].set(state_shift[l]).reshape(ts, Z_PAD)
    prep_p = _rwkv_prep(z, z, 0, tp, 512, seq_p, *vecs)
    prep_s = _rwkv_prep(z, first_s, tp, ts, 256, seq_s, *vecs)
    bo256 = _block_ones(2 * LANES)
    y_p, st_p = _rwkv_scan([a.reshape(nb_p, seq_p, RWKV_WIDTH) for a in prep_p[:6]], None, nb_p, seq_p, nb_p, 128,
                           pat, bo256)
    y_s, st_s = _rwkv_scan([a.reshape(nb_s, seq_s, RWKV_WIDTH) for a in prep_s[:6]], _state_to_tiles(state_rwkv[l]),
                           nb_s, seq_s, 4, seq_s, pat, bo256)
    o_rw_p = _rwkv_post(y_p.reshape(tp, RWKV_WIDTH), prep_p[6], prep_p[7], rwkv_ln_w[l], rwkv_ln_b[l], bo128, 512)
    o_rw_s = _rwkv_post(y_s.reshape(ts, RWKV_WIDTH), prep_s[6], prep_s[7], rwkv_ln_w[l], rwkv_ln_b[l], bo128, 512)
    o_rwkv = jnp.concatenate([o_rw_p, o_rw_s], axis=0)

    wo = w_out[l]
    rw = _pad_cols(router_w[l], LANES)
    rw_hi = rw.astype(BF16)
    rw_lo = (rw - rw_hi.astype(F32)).astype(BF16)
    rb = jnp.concatenate([router_b[l].astype(F32), jnp.full((LANES - N_EXPERTS,), -1e30, F32)]).reshape(1, LANES)
    x1, h2, top_e, top_g = _outproj_router(xp, xs, o_nsa, o_rwkv, wo[:NSA_WIDTH][perm].astype(BF16),
                                           wo[NSA_WIDTH:].astype(BF16), norm_ffn[l].reshape(1, -1), rw_hi, rw_lo, rb)
    slot_tok, slot_of, blk_e, n_used, _ = _route(top_e[:, :MOE_TOPK], MOE_BM)
    act = _moe_up(blk_e, n_used, slot_tok, h2, moe_w1[l], moe_b1[l].reshape(N_EXPERTS, 1, 2 * D_FF), MOE_BM)
    yb = _moe_down(blk_e, n_used, act, moe_w2[l], moe_b2[l].reshape(N_EXPERTS, 1, D_MODEL), MOE_BM)
    y_p = _moe_combine(slot_of, x1, top_g, norm_final, yb, 0, tp)
    y_s = _moe_combine(slot_of, x1, top_g, norm_final, yb, tp, ts)

    hd = (NSA_KV_HEADS, HEAD_DIM)
    kv_p = kvn[:tp].reshape(1, nb_p, seq_p, 4, *hd)
    kv_s = kvn[tp:].reshape(1, nb_s, seq_s, 4, *hd)
    win_keep = min(WINDOW, seq_p)
    win_p = winn[:tp].reshape(nb_p, seq_p, 2, *hd)[None, :, seq_p - win_keep:]
    win_s = new_win_s.reshape(1, nb_s, w_len, 2, *hd)
    sh_p = z[seq_p - 1:tp:seq_p, :SHIFT_WIDTH][None]
    sh_s = z[tp + seq_s - 1::seq_s, :SHIFT_WIDTH][None]
    return (y_p.reshape(nb_p, seq_p, D_MODEL), y_s.reshape(nb_s, seq_s, D_MODEL), kv_p, kv_s, win_p, win_s,
            _tiles_to_state(st_p)[None], _tiles_to_state(st_s)[None], sh_p, sh_s)
```

```python
import functools

import jax
import jax.numpy as jnp
from jax import lax
from jax.experimental import pallas as pl
from jax.experimental.pallas import tpu as pltpu

F32 = jnp.float32
BF16 = jnp.bfloat16

D_MODEL = 2048
HEAD_DIM = 64
NSA_HEADS = 16
NSA_KV_HEADS = 4
NSA_GROUP = 4
NSA_WIDTH = 1024
KV_WIDTH = 256
CMP_BLOCK = 32
SEL_BLOCK = 64
SEL_TOPK = 16
WINDOW = 512
Q_BLOCK = 128
PAGE_SIZE = 128
RWKV_HEADS = 16
RWKV_WIDTH = 1024
DECAY_LORA = 64
ICLR_LORA = 64
GATE_LORA = 160
SHIFT_WIDTH = 3 * RWKV_WIDTH + DECAY_LORA + ICLR_LORA + GATE_LORA
Z_PAD = 3456
N_EXPERTS = 32
MOE_TOPK = 4
D_FF = 2048
SWIGLU_LIMIT = 7.0
SWIGLU_ALPHA = 1.702
NORM_EPS = 1e-5
GN_EPS = 64e-5
LANES = 128
VMEM_LIMIT = 56 * 1024 * 1024
NEG = -1e30


def _cparams(*sem):
    return pltpu.CompilerParams(dimension_semantics=sem, vmem_limit_bytes=VMEM_LIMIT)


def _dot(a, b):
    return jnp.dot(a, b, preferred_element_type=F32)


def _split3(x):
    hi = x.astype(BF16)
    r1 = x - hi.astype(F32)
    mid = r1.astype(BF16)
    lo = (r1 - mid.astype(F32)).astype(BF16)
    return hi, mid, lo


def _dot_exact01(x, m01):
    hi, mid, lo = _split3(x)
    return _dot(hi, m01) + _dot(mid, m01) + _dot(lo, m01)


def _rows_of(xp_ref, xs_ref, n_p):
    return jnp.where(pl.program_id(0) < n_p, xp_ref[...], xs_ref[...])


def _row_specs(tm, n_p, width):
    return [pl.BlockSpec((tm, width), lambda i: (jnp.minimum(i, n_p - 1), 0)),
            pl.BlockSpec((tm, width), lambda i: (jnp.maximum(i - n_p, 0), 0))]


def _proj_kernel(xp_ref, xs_ref, g_ref, *refs, n_p):
    n = len(refs) // 2
    x = _rows_of(xp_ref, xs_ref, n_p)
    ms = jnp.mean(x * x, axis=-1, keepdims=True)
    h = (x * lax.rsqrt(ms + NORM_EPS) * g_ref[...]).astype(BF16)
    for w_ref, o_ref in zip(refs[:n], refs[n:]):
        o_ref[...] = _dot(h, w_ref[...])


def _in_proj(xp, xs, g, weights, tm=256):
    t = xp.shape[0] + xs.shape[0]
    n_p = xp.shape[0] // tm
    once = functools.partial(pl.BlockSpec, pipeline_mode=pl.Buffered(1))
    return pl.pallas_call(
        functools.partial(_proj_kernel, n_p=n_p),
        grid=(t // tm,),
        in_specs=_row_specs(tm, n_p, D_MODEL) + [once((1, D_MODEL), lambda i: (0, 0))]
        + [once(w.shape, lambda i: (0, 0)) for w in weights],
        out_specs=[pl.BlockSpec((tm, w.shape[1]), lambda i: (i, 0)) for w in weights],
        out_shape=[jax.ShapeDtypeStruct((t, w.shape[1]), F32) for w in weights],
        compiler_params=_cparams("parallel"),
        name="in_proj",
    )(xp, xs, g.reshape(1, -1), *weights)


def _compress_kernel(x_ref, w_ref, o_ref):
    nb = o_ref.shape[0]
    acc = jnp.zeros((nb, LANES), F32)
    for b in range(CMP_BLOCK):
        a = x_ref[pl.ds(b, nb, stride=CMP_BLOCK), :].astype(BF16)
        acc = acc + _dot(a, w_ref[0, b])
    o_ref[...] = acc


def _compress(rows, w_bd, n_blocks, nb):
    slabs = 2 * KV_WIDTH // LANES
    return pl.pallas_call(
        _compress_kernel,
        grid=(n_blocks // nb, slabs),
        in_specs=[
            pl.BlockSpec((nb * CMP_BLOCK, LANES), lambda i, s: (i, s)),
            pl.BlockSpec((1, CMP_BLOCK, LANES, LANES), lambda i, s: (s // 2, 0, 0, 0)),
        ],
        out_specs=pl.BlockSpec((nb, LANES), lambda i, s: (i, s)),
        out_shape=jax.ShapeDtypeStruct((n_blocks, 2 * KV_WIDTH), F32),
        compiler_params=_cparams("parallel", "arbitrary"),
        name="nsa_compress",
    )(rows, w_bd)


def _dot01_left(m01, x):
    hi, mid, lo = _split3(x)
    return _dot(m01, hi) + _dot(m01, mid) + _dot(m01, lo)


def _softmax0(s, mask):
    s = jnp.where(mask, s, -jnp.inf)
    m = jnp.max(s, axis=0, keepdims=True)
    m = jnp.where(jnp.isfinite(m), m, 0.0)
    e = jnp.exp(s - m)
    return e / jnp.maximum(jnp.sum(e, axis=0, keepdims=True), 1e-30)


def _select_blocks_t(imp, cur, n_rank):
    j = lax.broadcasted_iota(jnp.int32, (imp.shape[0], 1), 0)
    forced = (j == cur) | (j == 0)
    key = jnp.where(forced, jnp.inf, jnp.where(j < cur, imp, -jnp.inf))
    rank = jnp.zeros(key.shape, F32)
    for i in range(n_rank):
        ri = key[i:i + 1, :]
        ahead = (ri > key) | ((ri == key) & (j > i))
        rank = rank + jnp.where(ahead, 1.0, 0.0)
    return jnp.where(rank < SEL_TOPK, 1.0, 0.0)


def _dyn_row(x, r):
    rows = lax.broadcasted_iota(jnp.int32, (x.shape[0], 1), 0)
    return jnp.sum(jnp.where(rows == r, x, 0.0), axis=0, keepdims=True)


def _nsa_prompt_kernel(q_ref, gl_ref, ks_ref, vs_ref, kw_ref, vw_ref, kc_ref, vc_ref, slope_ref, gain_ref,
                       pair_ref, o_ref, m_ref, l_ref, acc_ref, b0_ref, *, seq):
    ne, tq = 2, Q_BLOCK
    ncol = NSA_GROUP * ne * tq
    cet = ne * tq
    n_cmp = seq // CMP_BLOCK
    tk = 2 * Q_BLOCK
    i = pl.program_id(1)
    p = pl.program_id(2)
    p0 = i * tq

    qt = q_ref[...].T
    sub = lax.broadcasted_iota(jnp.int32, (ne * HEAD_DIM, 1), 0)
    cols = []
    for g in range(NSA_GROUP):
        qg = qt[g * LANES:(g + 1) * LANES]
        for e in range(ne):
            cols.append(jnp.where((sub >= e * HEAD_DIM) & (sub < (e + 1) * HEAD_DIM), qg, 0.0))
    qbd = jnp.concatenate(cols, axis=1).astype(BF16)

    lane = lax.broadcasted_iota(jnp.int32, (1, ncol), 1)
    tloc = lane & (tq - 1)
    tpos = p0 + tloc
    slope = slope_ref[0]

    s = _dot(kc_ref[...].astype(BF16), qbd)
    cmp_end = (lax.broadcasted_iota(jnp.int32, (n_cmp, 1), 0) + 1) * CMP_BLOCK - 1
    dist = (tpos - cmp_end).astype(F32)
    pc = _softmax0(s - slope * dist, dist >= 0)
    o_c = _dot(vc_ref[...].T.astype(BF16), pc.astype(BF16))

    psum = pc[:, 0:cet]
    for g in range(1, NSA_GROUP):
        psum = psum + pc[:, g * cet:(g + 1) * cet]
    n_sel = seq // SEL_BLOCK
    imp = _dot01_left(pair_ref[...], psum)
    sel = _select_blocks_t(imp, jnp.right_shift(tpos[:, 0:cet], 6), n_sel).astype(BF16)

    ksub = lax.broadcasted_iota(jnp.int32, (tk, 1), 0)
    b0_ref[...] = slope * (tloc - ksub).astype(F32)
    m_ref[...] = jnp.full(m_ref.shape, NEG, F32)
    l_ref[...] = jnp.zeros(l_ref.shape, F32)
    acc_ref[...] = jnp.zeros(acc_ref.shape, F32)
    jrow = lax.broadcasted_iota(jnp.int32, (1, n_sel), 1)

    def tile(k0, causal):
        kt = ks_ref[pl.ds(k0, tk), :].astype(BF16)
        vt = vs_ref[pl.ds(k0, tk), :].T.astype(BF16)
        st = _dot(kt, qbd)
        expand = jnp.where(jnp.right_shift(k0 + ksub, 6) == jrow, 1.0, 0.0).astype(BF16)
        chosen = _dot(expand, sel) > 0.5
        off = (p0 - k0).astype(F32)
        if causal:
            chosen = chosen & ((tloc[:, 0:cet] - ksub + (p0 - k0)) >= 0)
        cb = jnp.where(chosen, 0.0, -jnp.inf)
        cb = jnp.concatenate([cb] * NSA_GROUP, axis=1)
        s2 = st - b0_ref[...] - slope * off + cb
        m_old = m_ref[...]
        m_new = jnp.maximum(m_old, jnp.max(s2, axis=0, keepdims=True))
        alpha = jnp.exp(m_old - m_new)
        pt = jnp.exp(s2 - m_new)
        l_ref[...] = alpha * l_ref[...] + jnp.sum(pt, axis=0, keepdims=True)
        acc_ref[...] = alpha * acc_ref[...] + _dot(vt, pt.astype(BF16))
        m_ref[...] = m_new

    n_past = lax.shift_right_logical(i, 1)

    def pair(t, carry):
        tile(pl.multiple_of(2 * t * tk, tk), False)
        tile(pl.multiple_of((2 * t + 1) * tk, tk), False)
        return carry

    def single(t, carry):
        tile(pl.multiple_of((n_past - 1) * tk, tk), False)
        return carry

    lax.fori_loop(0, lax.shift_right_logical(n_past, 1), pair, 0)
    lax.fori_loop(0, n_past & 1, single, 0)
    tile(pl.multiple_of(n_past * tk, tk), True)
    o_s = acc_ref[...] / jnp.maximum(l_ref[...], 1e-30)

    start = pl.multiple_of(jnp.maximum(p0 - WINDOW, 0), Q_BLOCK)
    wk = WINDOW + tq
    sw = _dot(kw_ref[pl.ds(start, wk), :].astype(BF16), qbd)
    kpos = start + lax.broadcasted_iota(jnp.int32, (wk, 1), 0)
    dw = tpos - kpos
    pw = _softmax0(sw - slope * dw.astype(F32), (dw >= 0) & (dw <= WINDOW))
    o_w = _dot(vw_ref[pl.ds(start, wk), :].T.astype(BF16), pw.astype(BF16))

    gate = jax.nn.sigmoid(gl_ref[...].T)
    for g in range(NSA_GROUP):
        halves = []
        for e in range(ne):
            c0 = (g * ne + e) * tq
            r0 = e * HEAD_DIM
            x = None
            for br, o in enumerate((o_c, o_s, o_w)):
                hrow = ((2 * p + e) * NSA_GROUP + g) * 3 + br
                gt = _dyn_row(gate, hrow)
                term = gt * o[r0:r0 + HEAD_DIM, c0:c0 + tq]
                x = term if x is None else x + term
            ms = jnp.mean(x * x, axis=0, keepdims=True)
            halves.append(x * lax.rsqrt(ms + NORM_EPS) * gain_ref[0, g * ne + e])
        o_ref[:, g * LANES:(g + 1) * LANES] = jnp.concatenate(halves, axis=0).T.astype(o_ref.dtype)


def _nsa_prompt(q, gl, kvn, winn, kvc, slopes, gain, pair01, n, seq):
    nq = seq // Q_BLOCK
    n_cmp = seq // CMP_BLOCK
    n_sel = seq // SEL_BLOCK
    ncol = NSA_GROUP * 2 * Q_BLOCK
    kern = functools.partial(_nsa_prompt_kernel, seq=seq)
    return pl.pallas_call(
        kern,
        grid=(n, nq, 2),
        in_specs=[
            pl.BlockSpec((Q_BLOCK, 512), lambda b, i, p: (b * nq + i, p)),
            pl.BlockSpec((Q_BLOCK, LANES), lambda b, i, p: (b * nq + i, 0)),
            pl.BlockSpec((seq, LANES), lambda b, i, p: (b, 4 + p)),
            pl.BlockSpec((seq, LANES), lambda b, i, p: (b, 6 + p)),
            pl.BlockSpec((seq, LANES), lambda b, i, p: (b, p)),
            pl.BlockSpec((seq, LANES), lambda b, i, p: (b, 2 + p)),
            pl.BlockSpec((n_cmp, LANES), lambda b, i, p: (b, p)),
            pl.BlockSpec((n_cmp, LANES), lambda b, i, p: (b, 2 + p)),
            pl.BlockSpec((1, 1, ncol), lambda b, i, p: (p, 0, 0)),
            pl.BlockSpec((1, 8, HEAD_DIM, LANES), lambda b, i, p: (p, 0, 0, 0)),
            pl.BlockSpec((n_sel, n_cmp), lambda b, i, p: (0, 0)),
        ],
        out_specs=pl.BlockSpec((Q_BLOCK, 512), lambda b, i, p: (b * nq + i, p)),
        out_shape=jax.ShapeDtypeStruct((n * seq, NSA_WIDTH), BF16),
        scratch_shapes=[pltpu.VMEM((1, ncol), F32), pltpu.VMEM((1, ncol), F32), pltpu.VMEM((LANES, ncol), F32),
                        pltpu.VMEM((2 * Q_BLOCK, ncol), F32)],
        compiler_params=_cparams("parallel", "parallel", "parallel"),
        name="nsa_prompt",
    )(q, gl, kvn, kvn, winn, winn, kvc, kvc, slopes, gain, pair01)


def _slope_cols(ne, tq, pair):
    g, e, t = jnp.meshgrid(jnp.arange(NSA_GROUP), jnp.arange(ne), jnp.arange(tq), indexing="ij")
    head = (pair * ne + e) * NSA_GROUP + g
    return _alibi_slopes()[head].reshape(1, -1)


def _gain_cols(gain):
    gh = gain.reshape(2, 2, NSA_GROUP, HEAD_DIM).transpose(0, 2, 1, 3).reshape(2, 8, HEAD_DIM)
    return jnp.broadcast_to(gh[..., None], (2, 8, HEAD_DIM, LANES))


def _pair01_t(n_cmp):
    return (jnp.arange(n_cmp // 2)[:, None] == jnp.arange(n_cmp)[None, :] // 2).astype(BF16)


def _pad_rows(x, rows):
    return jnp.concatenate([x, jnp.zeros((rows - x.shape[0], x.shape[1]), x.dtype)], axis=0)


def _nsa_sample_kernel(pt_ref, q_ref, gl_ref, kvn_ref, winn_ref, wst_ref, *rest, past_len):
    del pt_ref
    n_pages = past_len // PAGE_SIZE
    page_refs = rest[:n_pages]
    kc_refs = rest[n_pages:2 * n_pages]
    (slope_ref, gain_ref, valid_ref, pair_ref, sumg_ref, tile_ref, gsel_ref,
     o_ref, nw_ref, kc_s, vc_s) = rest[2 * n_pages:]
    ne, tq = NSA_KV_HEADS, 8
    width = ne * HEAD_DIM
    ncol = NSA_GROUP * ne * tq
    blocks_per_page = PAGE_SIZE // CMP_BLOCK

    qf = q_ref[...]
    lane_w = lax.broadcasted_iota(jnp.int32, (1, width), 1)
    parts = []
    for g in range(NSA_GROUP):
        slab = jnp.concatenate([qf[:, g * LANES:(g + 1) * LANES],
                                qf[:, 4 * LANES + g * LANES:4 * LANES + (g + 1) * LANES]], axis=1)
        for e in range(ne):
            parts.append(jnp.where((lane_w >= e * HEAD_DIM) & (lane_w < (e + 1) * HEAD_DIM), slab, 0.0))
    qbd = jnp.concatenate(parts, axis=0).T.astype(BF16)

    lane = lax.broadcasted_iota(jnp.int32, (1, ncol), 1)
    tloc = lane & (tq - 1)
    tpos = past_len + tloc
    slope = slope_ref[...]

    kc_s[...] = jnp.zeros_like(kc_s)
    vc_s[...] = jnp.zeros_like(vc_s)
    for j in range(n_pages):
        for b in range(blocks_per_page):
            c = j * blocks_per_page + b
            kc_s[c:c + 1, :] = kc_refs[j][0, :, b * 2 * KV_WIDTH:b * 2 * KV_WIDTH + KV_WIDTH]
            vc_s[c:c + 1, :] = kc_refs[j][0, :, b * 2 * KV_WIDTH + KV_WIDTH:(b + 1) * 2 * KV_WIDTH]
    ncp = kc_s.shape[0]
    s = _dot(kc_s[...].astype(BF16), qbd)
    cmp_end = (lax.broadcasted_iota(jnp.int32, (ncp, 1), 0) + 1) * CMP_BLOCK - 1
    dist = (tpos - cmp_end).astype(F32)
    pc = _softmax0(s - slope * dist, dist >= 0)
    o_c = _dot(vc_s[...].T.astype(BF16), pc.astype(BF16))

    psum = _dot_exact01(pc, sumg_ref[...])
    imp = _dot01_left(pair_ref[...], psum)
    n_sel_past = past_len // SEL_BLOCK
    sel = _select_blocks_t(imp, jnp.right_shift(tpos, 6), n_sel_past + 1).astype(BF16)
    sel = _dot(sel, tile_ref[...]).astype(BF16)

    ksub = lax.broadcasted_iota(jnp.int32, (PAGE_SIZE, 1), 0)
    jrow = lax.broadcasted_iota(jnp.int32, (1, sel.shape[0]), 1)
    b0 = slope * (tloc - ksub).astype(F32)

    def scores(kt, k0, causal):
        st = _dot(kt.astype(BF16), qbd)
        expand = jnp.where(jnp.right_shift(k0 + ksub, 6) == jrow, 1.0, 0.0).astype(BF16)
        chosen = _dot(expand, sel) > 0.5
        if causal:
            chosen = chosen & ((tloc - ksub + (past_len - k0)) >= 0)
        return st - b0 - slope * float(past_len - k0) + jnp.where(chosen, 0.0, -jnp.inf)

    knew = _pad_rows(kvn_ref[:, 2 * KV_WIDTH:3 * KV_WIDTH], PAGE_SIZE)
    vnew = _pad_rows(kvn_ref[:, 3 * KV_WIDTH:4 * KV_WIDTH], PAGE_SIZE)
    s_all = [scores(page_refs[j][:, 0:KV_WIDTH], j * PAGE_SIZE, False) for j in range(n_pages)]
    s_all.append(scores(knew, past_len, True))
    m = s_all[0]
    for sj in s_all[1:]:
        m = jnp.maximum(m, sj)
    m = jnp.maximum(jnp.max(m, axis=0, keepdims=True), NEG)
    l = jnp.zeros((1, ncol), F32)
    acc = jnp.zeros((width, ncol), F32)
    for j, sj in enumerate(s_all):
        pt = jnp.exp(sj - m)
        vt = page_refs[j][:, KV_WIDTH:2 * KV_WIDTH] if j < n_pages else vnew
        l = l + jnp.sum(pt, axis=0, keepdims=True)
        acc = acc + _dot(vt.T.astype(BF16), pt.astype(BF16))
    o_s = acc / jnp.maximum(l, 1e-30)

    kw = jnp.concatenate([wst_ref[0, :, 0:KV_WIDTH], _pad_rows(winn_ref[:, 0:KV_WIDTH], PAGE_SIZE)], axis=0)
    vw = jnp.concatenate([wst_ref[0, :, KV_WIDTH:], _pad_rows(winn_ref[:, KV_WIDTH:], PAGE_SIZE)], axis=0)
    start = past_len - WINDOW
    sw = _dot(kw.astype(BF16), qbd)
    kpos = start + lax.broadcasted_iota(jnp.int32, (kw.shape[0], 1), 0)
    dw = tpos - kpos
    pw = _softmax0(sw - slope * dw.astype(F32), (dw >= 0) & (dw <= WINDOW))
    o_w = _dot(vw.T.astype(BF16), pw.astype(BF16))

    sg = jax.nn.sigmoid(gl_ref[...])
    trow = lax.broadcasted_iota(jnp.int32, (tq, 1), 0)
    x = None
    for br, o in enumerate((o_c, o_s, o_w)):
        spread = _dot_exact01(sg, gsel_ref[br])
        grow = jnp.sum(jnp.where(trow == tloc, spread, 0.0), axis=0, keepdims=True)
        x = grow * o if x is None else x + grow * o
    x = jnp.where(valid_ref[...] > 0.5, x, 0.0)
    ms = jnp.sum(x * x, axis=0, keepdims=True) * (1.0 / HEAD_DIM)
    y = (x * lax.rsqrt(ms + NORM_EPS) * gain_ref[...]).T
    for g in range(NSA_GROUP):
        og = y[g * ne * tq:g * ne * tq + tq]
        for e in range(1, ne):
            og = og + y[(g * ne + e) * tq:(g * ne + e + 1) * tq]
        o_ref[:, g * LANES:(g + 1) * LANES] = og[:, 0:LANES]
        o_ref[:, (NSA_GROUP + g) * LANES:(NSA_GROUP + g + 1) * LANES] = og[:, LANES:]

    w_len = wst_ref.shape[1]
    nw_ref[0, 0:w_len - tq, :] = wst_ref[0, tq:w_len, :]
    nw_ref[0, w_len - tq:w_len, :] = winn_ref[...]


def _sample_consts(gain):
    ne, tq = NSA_KV_HEADS, 8
    c = jnp.arange(NSA_GROUP * ne * tq)
    cg, ck = c // (ne * tq), (c // tq) % ne
    chead = ck * NSA_GROUP + cg
    r = jnp.arange(ne * HEAD_DIM)
    rk = r // HEAD_DIM
    slope = _alibi_slopes()[chead].reshape(1, -1)
    valid = (rk[:, None] == ck[None, :]).astype(F32)
    gain_t = gain.reshape(ne, NSA_GROUP, HEAD_DIM).transpose(0, 2, 1).reshape(ne * HEAD_DIM, NSA_GROUP)
    gain_t = jnp.repeat(gain_t, ne * tq, axis=1) * valid
    lanes = jnp.arange(LANES)
    sumg = ((lanes[None, :] < ne * tq) & (c[:, None] % (ne * tq) == lanes[None, :])).astype(BF16)
    tile = ((lanes[:, None] < ne * tq) & (lanes[:, None] == c[None, :] % (ne * tq))).astype(BF16)
    gsel = jnp.stack([(lanes[:, None] == (chead * 3 + br)[None, :]).astype(BF16) for br in range(3)])
    return slope, gain_t, valid, sumg, tile, gsel


def _nsa_sample(page_table, q, gl, kvn, winn, win_state, cache_rows, kvc_pages, consts, pair01, row0):
    slope, gain_t, valid, sumg, tile, gsel = consts
    n_seq, n_pages = page_table.shape
    past_len = n_pages * PAGE_SIZE
    w_len = win_state.shape[1]
    blk0 = row0 // 8
    tok = lambda s, pt: (blk0 + s, 0)
    full = lambda a: pl.BlockSpec(a.shape, lambda s, pt: (0,) * a.ndim)
    in_specs = [
        pl.BlockSpec((8, NSA_WIDTH), tok),
        pl.BlockSpec((8, LANES), tok),
        pl.BlockSpec((8, 4 * KV_WIDTH), tok),
        pl.BlockSpec((8, 2 * KV_WIDTH), tok),
        pl.BlockSpec((1, w_len, 2 * KV_WIDTH), lambda s, pt: (s, 0, 0)),
    ]
    in_specs += [pl.BlockSpec((PAGE_SIZE, 2 * KV_WIDTH), functools.partial(lambda s, pt, j: (pt[s, j], 1), j=j))
                 for j in range(n_pages)]
    in_specs += [pl.BlockSpec((1, 1, 8 * KV_WIDTH), functools.partial(lambda s, pt, j: (pt[s, j], 0, 0), j=j))
                 for j in range(n_pages)]
    in_specs += [full(slope), full(gain_t), full(valid), full(pair01), full(sumg), full(tile), full(gsel)]
    ncp = -(-(past_len // CMP_BLOCK) // LANES) * LANES
    grid_spec = pltpu.PrefetchScalarGridSpec(
        num_scalar_prefetch=1,
        grid=(n_seq,),
        in_specs=in_specs,
        out_specs=[pl.BlockSpec((8, NSA_WIDTH), lambda s, pt: (s, 0)),
                   pl.BlockSpec((1, w_len, 2 * KV_WIDTH), lambda s, pt: (s, 0, 0))],
        scratch_shapes=[pltpu.VMEM((ncp, KV_WIDTH), F32), pltpu.VMEM((ncp, KV_WIDTH), F32)],
    )
    return pl.pallas_call(
        functools.partial(_nsa_sample_kernel, past_len=past_len),
        grid_spec=grid_spec,
        out_shape=[jax.ShapeDtypeStruct((n_seq * 8, NSA_WIDTH), F32),
                   jax.ShapeDtypeStruct((n_seq, w_len, 2 * KV_WIDTH), F32)],
        compiler_params=_cparams("parallel"),
        name="nsa_sample",
    )(page_table, q, gl, kvn, winn, win_state, *([cache_rows] * n_pages), *([kvc_pages] * n_pages),
      slope, gain_t, valid, pair01, sumg, tile, gsel)


def _head_sums(x, bo01):
    return jnp.concatenate([_dot_exact01(x[:, s * LANES:(s + 1) * LANES], bo01)
                            for s in range(x.shape[1] // LANES)], axis=1)


def _rwkv_project(z, prev_row, mu, w0, a0, k_k, k_a, r_k, w2p, a2p, g2p, bo):
    row = lax.broadcasted_iota(jnp.int32, (z.shape[0], 1), 0)
    prev = jnp.where(row == 0, prev_row, pltpu.roll(z, 1, 0))
    zm = z + (prev - z) * mu
    w = RWKV_WIDTH
    r, k, v = zm[:, 0:w], zm[:, w:2 * w], zm[:, 2 * w:3 * w]
    wa = zm[:, 3 * w:3 * w + LANES]
    gd = zm[:, 3 * w + LANES:]
    neg = -(w0 + _dot(jnp.tanh(wa).astype(BF16), w2p))
    softplus = jnp.maximum(neg, 0.0) + jnp.log1p(jnp.exp(-jnp.abs(neg)))
    decay = jnp.exp(-jnp.exp(-softplus - 0.5))
    a = jax.nn.sigmoid(a0 + _dot(wa.astype(BF16), a2p))
    g = _dot(jax.nn.sigmoid(gd).astype(BF16), g2p)
    kk = k * k_k
    kk = kk / jnp.maximum(jnp.sqrt(_head_sums(kk * kk, bo)), 1e-12)
    k2 = k * (1.0 + (a - 1.0) * k_a)
    bonus = _head_sums(r * k2 * r_k, bo) * v
    return (r, decay, k2, v, kk, -(kk * a)), g, bonus


def _rwkv_kernel(*refs, nbatch, tc, has_init):
    z_refs = refs[:nbatch]
    pos = nbatch
    first_ref = s0_ref = None
    if has_init:
        first_ref, s0_ref = refs[pos], refs[pos + 1]
        pos += 2
    (mu_ref, w0_ref, a0_ref, kk_ref, ka_ref, rk_ref, lnw_ref, lnb_ref, w2_ref, a2_ref, g2_ref, bo128_ref, pat_ref,
     bo_ref) = refs[pos:pos + 14]
    pos += 14
    o_ref, s_out, st, ops_s, y_s, zlast = refs[pos:pos + 6]
    c = pl.program_id(1)
    nq = RWKV_HEADS // 4
    tw = 4 * HEAD_DIM

    @pl.when(c == 0)
    def _():
        if has_init:
            st[...] = s0_ref[...]
            zlast[...] = first_ref[0]
        else:
            st[...] = jnp.zeros_like(st)
            zlast[...] = jnp.zeros_like(zlast)

    gates, bonuses = [], []
    for b in range(nbatch):
        z = z_refs[b][...]
        ops, g, bonus = _rwkv_project(z, zlast[b:b + 1, :], mu_ref[...], w0_ref[...], a0_ref[...], kk_ref[...],
                                      ka_ref[...], rk_ref[...], w2_ref[...], a2_ref[...], g2_ref[...], bo128_ref[...])
        zlast[b:b + 1, :] = z[tc - 1:tc, :]
        for k, op in enumerate(ops):
            ops_s[k, b] = op
        gates.append(g)
        bonuses.append(bonus)

    pat = pat_ref[...]
    bo = bo_ref[...]

    def step(t, carry):
        tiles = [(b, q) for b in range(nbatch) for q in range(nq)]
        rowv = lambda k, b, q: ops_s[k, b, pl.ds(t, 1), q * tw:(q + 1) * tw]
        lhs = []
        for b, q in tiles:
            lhs.append((st[b, q] * rowv(4, b, q)).astype(BF16))
            lhs.append((pat * rowv(3, b, q)).astype(BF16))
        red = _dot(jnp.concatenate(lhs, axis=0), bo)
        outs = []
        for n, (b, q) in enumerate(tiles):
            sa = red[n * 2 * HEAD_DIM:n * 2 * HEAD_DIM + HEAD_DIM]
            vb = red[n * 2 * HEAD_DIM + HEAD_DIM:(n + 1) * 2 * HEAD_DIM]
            s2 = st[b, q] * rowv(1, b, q) + sa * rowv(5, b, q) + vb * rowv(2, b, q)
            st[b, q] = s2
            outs.append((s2 * rowv(0, b, q)).astype(BF16))
        yb = _dot(jnp.concatenate(outs, axis=0), bo)
        for n, (b, q) in enumerate(tiles):
            yrow = jnp.sum(yb[n * HEAD_DIM:(n + 1) * HEAD_DIM] * pat, axis=0, keepdims=True)
            y_s[b, pl.ds(t, 1), q * tw:(q + 1) * tw] = yrow
        return carry

    lax.fori_loop(0, tc, step, 0, unroll=4)

    for b in range(nbatch):
        y = y_s[b]
        d = y - _head_sums(y, bo128_ref[...]) * (1.0 / HEAD_DIM)
        var = _head_sums(d * d, bo128_ref[...]) * (1.0 / HEAD_DIM)
        o = (d * lax.rsqrt(var + GN_EPS) * lnw_ref[...] + lnb_ref[...] + bonuses[b]) * gates[b]
        o_ref[b] = o.astype(o_ref.dtype)

    @pl.when(c == pl.num_programs(1) - 1)
    def _():
        s_out[...] = st[...]


def _rwkv_mixer(z, row0, nseq, seq_rows, nbatch, tc, first, s0, vecs, mats):
    nchunk = seq_rows // tc
    nq = RWKV_HEADS // 4
    blk0 = row0 // tc
    rows = lambda b: pl.BlockSpec((tc, Z_PAD), lambda gi, c: (blk0 + (gi * nbatch + b) * nchunk + c, 0))
    st_spec = pl.BlockSpec((nbatch, nq, HEAD_DIM, 4 * HEAD_DIM), lambda gi, c: (gi, 0, 0, 0))
    const = lambda a: pl.BlockSpec(a.shape, lambda gi, c: (0,) * a.ndim)
    in_specs = [rows(b) for b in range(nbatch)]
    args = [z] * nbatch
    if s0 is not None:
        in_specs += [pl.BlockSpec((1, nbatch, Z_PAD), lambda gi, c: (gi, 0, 0)), st_spec]
        args += [first.reshape(nseq // nbatch, nbatch, Z_PAD), s0]
    consts = [v.reshape(1, -1) for v in vecs] + list(mats)
    in_specs += [const(a) for a in consts]
    args += consts
    return pl.pallas_call(
        functools.partial(_rwkv_kernel, nbatch=nbatch, tc=tc, has_init=s0 is not None),
        grid=(nseq // nbatch, nchunk),
        in_specs=in_specs,
        out_specs=[pl.BlockSpec((nbatch, tc, RWKV_WIDTH), lambda gi, c: (gi, c, 0)), st_spec],
        out_shape=[jax.ShapeDtypeStruct((nseq, seq_rows, RWKV_WIDTH), F32),
                   jax.ShapeDtypeStruct((nseq, nq, HEAD_DIM, 4 * HEAD_DIM), F32)],
        scratch_shapes=[pltpu.VMEM((nbatch, nq, HEAD_DIM, 4 * HEAD_DIM), F32),
                        pltpu.VMEM((6, nbatch, tc, RWKV_WIDTH), F32), pltpu.VMEM((nbatch, tc, RWKV_WIDTH), F32),
                        pltpu.VMEM((nbatch, Z_PAD), F32)],
        compiler_params=_cparams("parallel", "arbitrary"),
        name="rwkv_mixer",
    )(*args)


def _rwkv_consts(w2, a2, g2):
    w2p = jnp.concatenate([w2, jnp.zeros((LANES - DECAY_LORA, RWKV_WIDTH), w2.dtype)], axis=0)
    a2p = jnp.concatenate([jnp.zeros((DECAY_LORA, RWKV_WIDTH), a2.dtype), a2], axis=0)
    g2p = jnp.concatenate([g2, jnp.zeros((2 * LANES - GATE_LORA, RWKV_WIDTH), g2.dtype)], axis=0)
    v = jnp.arange(HEAD_DIM)
    pat = (v[:, None] == (jnp.arange(4 * HEAD_DIM)[None, :] % HEAD_DIM)).astype(F32)
    return w2p.astype(BF16), a2p.astype(BF16), g2p.astype(BF16), pat


def _state_to_tiles(s):
    n = s.shape[0]
    return s.reshape(n, 4, 4, HEAD_DIM, HEAD_DIM).transpose(0, 1, 3, 2, 4).reshape(n, 4, HEAD_DIM, 4 * HEAD_DIM)


def _tiles_to_state(t):
    n = t.shape[0]
    return t.reshape(n, 4, HEAD_DIM, 4, HEAD_DIM).transpose(0, 1, 3, 2, 4).reshape(n, RWKV_HEADS, HEAD_DIM, HEAD_DIM)


def _outproj_kernel(xp_ref, xs_ref, on_ref, or_ref, wa_ref, wb_ref, g_ref, rwh_ref, rwl_ref, rb_ref,
                    x1_ref, h2_ref, te_ref, tg_ref, *, n_p):
    x1 = _rows_of(xp_ref, xs_ref, n_p) + _dot(on_ref[...], wa_ref[...]) + _dot(or_ref[...], wb_ref[...])
    x1_ref[...] = x1
    ms = jnp.mean(x1 * x1, axis=-1, keepdims=True)
    hf = x1 * lax.rsqrt(ms + NORM_EPS) * g_ref[...]
    h2_ref[...] = hf
    hh = hf.astype(BF16)
    hl = (hf - hh.astype(F32)).astype(BF16)
    logits = _dot(hh, rwh_ref[...]) + _dot(hl, rwh_ref[...]) + _dot(hh, rwl_ref[...]) + rb_ref[...]
    lane = lax.broadcasted_iota(jnp.int32, logits.shape, 1).astype(F32)
    vals, idxs = [], []
    for _ in range(MOE_TOPK):
        m = jnp.max(logits, axis=-1, keepdims=True)
        idx = jnp.min(jnp.where(logits == m, lane, float(LANES)), axis=-1, keepdims=True)
        vals.append(m)
        idxs.append(idx)
        logits = jnp.where(lane == idx, -jnp.inf, logits)
    es = [jnp.exp(v - vals[0]) for v in vals]
    denom = es[0] + es[1] + es[2] + es[3]
    te = jnp.zeros(logits.shape, F32)
    tg = jnp.zeros(logits.shape, F32)
    for k in range(MOE_TOPK):
        te = jnp.where(lane == float(k), idxs[k], te)
        tg = jnp.where(lane == float(k), es[k] / denom, tg)
    te_ref[...] = te.astype(jnp.int32)
    tg_ref[...] = tg


def _outproj_router(xp, xs, o_nsa, o_rwkv, wa, wb, g, rw_hi, rw_lo, rb, tm=256):
    t = xp.shape[0] + xs.shape[0]
    n_p = xp.shape[0] // tm
    tok = lambda w: pl.BlockSpec((tm, w), lambda i: (i, 0))
    full = lambda a: pl.BlockSpec(a.shape, lambda i: (0,) * a.ndim)
    return pl.pallas_call(
        functools.partial(_outproj_kernel, n_p=n_p),
        grid=(t // tm,),
        in_specs=_row_specs(tm, n_p, D_MODEL) + [tok(NSA_WIDTH), tok(RWKV_WIDTH), full(wa), full(wb), full(g),
                                                 full(rw_hi), full(rw_lo), full(rb)],
        out_specs=[tok(D_MODEL), tok(D_MODEL), tok(LANES), tok(LANES)],
        out_shape=[jax.ShapeDtypeStruct((t, D_MODEL), F32), jax.ShapeDtypeStruct((t, D_MODEL), F32),
                   jax.ShapeDtypeStruct((t, LANES), jnp.int32), jax.ShapeDtypeStruct((t, LANES), F32)],
        compiler_params=_cparams("parallel"),
        name="outproj_router",
    )(xp, xs, o_nsa, o_rwkv, wa, wb, g, rw_hi, rw_lo, rb)


MOE_BM = 256


def _route(top_e, bm):
    n_tok = top_e.shape[0]
    flat_e = top_e.reshape(-1)
    n_assign = flat_e.shape[0]
    onehot = (flat_e[:, None] == jnp.arange(N_EXPERTS, dtype=jnp.int32)[None, :]).astype(jnp.int32)
    cum = jnp.cumsum(onehot, axis=0)
    counts = cum[-1]
    pos = jnp.take_along_axis(cum, flat_e[:, None], axis=1)[:, 0] - 1
    padded = (counts + bm - 1) // bm * bm
    pad_end = jnp.cumsum(padded)
    dest = (pad_end - padded)[flat_e] + pos
    n_blocks = -(-n_assign // bm) + N_EXPERTS
    slot_tok = jnp.zeros((n_blocks * bm,), jnp.int32).at[dest].set(jnp.arange(n_assign, dtype=jnp.int32) // MOE_TOPK)
    n_used = (pad_end[-1] // bm).astype(jnp.int32)
    blk = jnp.minimum(jnp.arange(n_blocks, dtype=jnp.int32), n_used - 1)
    blk_e = jnp.minimum(jnp.sum((pad_end[None, :] <= (blk * bm)[:, None]).astype(jnp.int32), axis=1), N_EXPERTS - 1)
    return slot_tok, dest.reshape(n_tok, MOE_TOPK).astype(jnp.int32), blk_e, n_used.reshape(1), n_blocks


def _row_copy(src_ref, dst_ref, sem, src_row, dst_row):
    return pltpu.make_async_copy(src_ref.at[pl.ds(src_row, 1)], dst_ref.at[pl.ds(dst_row, 1)], sem)


def _gather_issue(idx_ref, src_ref, buf, sem):
    n = buf.shape[0]

    def issue(r2, c):
        for par in range(2):
            r = 2 * r2 + par
            _row_copy(src_ref, buf, sem, idx_ref[0, 0, r], r).start(priority=par)
        return c

    lax.fori_loop(0, n // 2, issue, 0, unroll=4)


def _gather_wait(src_ref, buf, sem):
    pltpu.make_async_copy(src_ref.at[pl.ds(0, buf.shape[0])], buf, sem).wait()


def _gather_pipelined(cur_ref, nxt_ref, src_ref, buf, sem, b, nsteps):
    slot = b & 1

    @pl.when(b == 0)
    def _():
        _gather_issue(cur_ref, src_ref, buf.at[0], sem.at[0])

    @pl.when(b + 1 < nsteps)
    def _():
        _gather_issue(nxt_ref, src_ref, buf.at[1 - slot], sem.at[1 - slot])

    _gather_wait(src_ref, buf.at[slot], sem.at[slot])
    return slot


def _moe_gather_kernel(cur_ref, nxt_ref, h_ref, o_ref, buf, sem, *, nsteps):
    slot = _gather_pipelined(cur_ref, nxt_ref, h_ref, buf, sem, pl.program_id(0), nsteps)
    o_ref[...] = buf[slot].astype(o_ref.dtype)


def _moe_gather(slot_tok, h2, bm):
    n_slots = slot_tok.shape[0]
    nblk = n_slots // bm
    return pl.pallas_call(
        functools.partial(_moe_gather_kernel, nsteps=nblk),
        grid=(nblk,),
        in_specs=[pl.BlockSpec((1, 1, bm), lambda b: (b, 0, 0), memory_space=pltpu.SMEM),
                  pl.BlockSpec((1, 1, bm), lambda b: (jnp.minimum(b + 1, nblk - 1), 0, 0), memory_space=pltpu.SMEM),
                  pl.BlockSpec(memory_space=pl.ANY)],
        out_specs=pl.BlockSpec((bm, D_MODEL), lambda b: (b, 0)),
        out_shape=jax.ShapeDtypeStruct((n_slots, D_MODEL), BF16),
        scratch_shapes=[pltpu.VMEM((2, bm, D_MODEL), F32), pltpu.SemaphoreType.DMA((2,))],
        compiler_params=_cparams("arbitrary"),
        name="moe_gather",
    )(slot_tok.reshape(nblk, 1, bm), slot_tok.reshape(nblk, 1, bm), h2)


MOE_FT = 1024
MOE_NF = D_FF // MOE_FT


def _expert_changed(be_ref, b):
    prev = be_ref[jnp.maximum(b - 1, 0)]
    return (b == 0) | (be_ref[b] != prev)


def _moe_up_kernel(be_ref, nu_ref, x_ref, wg_ref, wl_ref, bg_ref, bl_ref, o_ref, wg_s, wl_s):
    b = pl.program_id(1)

    @pl.when(b < nu_ref[0])
    def _():
        @pl.when(_expert_changed(be_ref, b))
        def _():
            wg_s[...] = wg_ref[0].astype(BF16)
            wl_s[...] = wl_ref[0].astype(BF16)

        x = x_ref[...]
        glu = jnp.minimum(_dot(x, wg_s[...]) + bg_ref[0], SWIGLU_LIMIT)
        lin = jnp.clip(_dot(x, wl_s[...]) + bl_ref[0], -SWIGLU_LIMIT, SWIGLU_LIMIT)
        o_ref[...] = (glu * jax.nn.sigmoid(SWIGLU_ALPHA * glu) * (lin + 1.0)).astype(o_ref.dtype)

    @pl.when(b >= nu_ref[0])
    def _():
        o_ref[...] = jnp.zeros_like(o_ref)


def _moe_up(blk_e, n_used, xs, w1, b1, bm):
    n_slots = xs.shape[0]
    nblk = n_slots // bm
    live = lambda b, nu: jnp.minimum(b, nu[0] - 1)
    grid_spec = pltpu.PrefetchScalarGridSpec(
        num_scalar_prefetch=2,
        grid=(MOE_NF, nblk),
        in_specs=[
            pl.BlockSpec((bm, D_MODEL), lambda f, b, be, nu: (live(b, nu), 0)),
            pl.BlockSpec((1, D_MODEL, MOE_FT), lambda f, b, be, nu: (be[b], 0, f)),
            pl.BlockSpec((1, D_MODEL, MOE_FT), lambda f, b, be, nu: (be[b], 0, MOE_NF + f)),
            pl.BlockSpec((1, 1, MOE_FT), lambda f, b, be, nu: (be[b], 0, f)),
            pl.BlockSpec((1, 1, MOE_FT), lambda f, b, be, nu: (be[b], 0, MOE_NF + f)),
        ],
        out_specs=pl.BlockSpec((bm, MOE_FT), lambda f, b, be, nu: (b, f)),
        scratch_shapes=[pltpu.VMEM((D_MODEL, MOE_FT), BF16), pltpu.VMEM((D_MODEL, MOE_FT), BF16)],
    )
    return pl.pallas_call(
        _moe_up_kernel,
        grid_spec=grid_spec,
        out_shape=jax.ShapeDtypeStruct((n_slots, D_FF), BF16),
        compiler_params=_cparams("arbitrary", "arbitrary"),
        name="moe_up",
    )(blk_e, n_used, xs, w1, w1, b1, b1)


def _moe_down_kernel(be_ref, nu_ref, a_ref, w_ref, b_ref, o_ref, w_s):
    b = pl.program_id(0)

    @pl.when(b < nu_ref[0])
    def _():
        @pl.when(_expert_changed(be_ref, b))
        def _():
            w_s[...] = w_ref[0].astype(BF16)

        o_ref[...] = _dot(a_ref[...], w_s[...]) + b_ref[0]

    @pl.when(b >= nu_ref[0])
    def _():
        o_ref[...] = jnp.zeros_like(o_ref)


def _moe_down(blk_e, n_used, act, w2, b2, bm):
    n_slots = act.shape[0]
    nblk = n_slots // bm
    live = lambda b, nu: jnp.minimum(b, nu[0] - 1)
    grid_spec = pltpu.PrefetchScalarGridSpec(
        num_scalar_prefetch=2,
        grid=(nblk,),
        in_specs=[
            pl.BlockSpec((bm, D_FF), lambda b, be, nu: (live(b, nu), 0)),
            pl.BlockSpec((1, D_FF, D_MODEL), lambda b, be, nu: (be[b], 0, 0)),
            pl.BlockSpec((1, 1, D_MODEL), lambda b, be, nu: (be[b], 0, 0)),
        ],
        out_specs=pl.BlockSpec((bm, D_MODEL), lambda b, be, nu: (b, 0)),
        scratch_shapes=[pltpu.VMEM((D_FF, D_MODEL), BF16)],
    )
    return pl.pallas_call(
        _moe_down_kernel,
        grid_spec=grid_spec,
        out_shape=jax.ShapeDtypeStruct((n_slots, D_MODEL), F32),
        compiler_params=_cparams("arbitrary"),
        name="moe_down",
    )(blk_e, n_used, act, w2, b2)


def _combine_kernel(cur_ref, nxt_ref, x_ref, tg_ref, g_ref, oh_ref, yb_ref, o_ref, buf, sem, *, nsteps):
    tm = x_ref.shape[0]
    slot = _gather_pipelined(cur_ref, nxt_ref, yb_ref, buf, sem, pl.program_id(0), nsteps)
    x = x_ref[...]
    gates = tg_ref[...]
    for k in range(MOE_TOPK):
        gk = _dot_exact01(gates, oh_ref[k])
        x = x + jnp.concatenate([gk] * (D_MODEL // LANES), axis=1) * buf[slot, k * tm:(k + 1) * tm]
    ms = jnp.mean(x * x, axis=-1, keepdims=True)
    o_ref[...] = x * lax.rsqrt(ms + NORM_EPS) * g_ref[...]


def _moe_combine(slot_of, x1, tg, g, yb, row0, t, tm=128):
    nt = t // tm
    blk0 = row0 // tm
    slots = slot_of[row0:row0 + t].reshape(nt, tm, MOE_TOPK).transpose(0, 2, 1).reshape(nt, 1, MOE_TOPK * tm)
    onehot = (jnp.arange(LANES)[None, :, None] == jnp.arange(MOE_TOPK)[:, None, None]).astype(BF16)
    onehot = jnp.broadcast_to(onehot, (MOE_TOPK, LANES, LANES))
    return pl.pallas_call(
        functools.partial(_combine_kernel, nsteps=nt),
        grid=(nt,),
        in_specs=[pl.BlockSpec((1, 1, MOE_TOPK * tm), lambda i: (i, 0, 0), memory_space=pltpu.SMEM),
                  pl.BlockSpec((1, 1, MOE_TOPK * tm), lambda i: (jnp.minimum(i + 1, nt - 1), 0, 0),
                               memory_space=pltpu.SMEM),
                  pl.BlockSpec((tm, D_MODEL), lambda i: (blk0 + i, 0)),
                  pl.BlockSpec((tm, LANES), lambda i: (blk0 + i, 0)),
                  pl.BlockSpec((1, D_MODEL), lambda i: (0, 0)),
                  pl.BlockSpec((MOE_TOPK, LANES, LANES), lambda i: (0, 0, 0)),
                  pl.BlockSpec(memory_space=pl.ANY)],
        out_specs=pl.BlockSpec((tm, D_MODEL), lambda i: (i, 0)),
        out_shape=jax.ShapeDtypeStruct((t, D_MODEL), F32),
        scratch_shapes=[pltpu.VMEM((2, MOE_TOPK * tm, D_MODEL), F32), pltpu.SemaphoreType.DMA((2,))],
        compiler_params=_cparams("arbitrary"),
        name="moe_combine",
    )(slots, slots, x1, tg, g.reshape(1, -1), onehot, yb)


def _head_perm():
    p, g, e, d = jnp.meshgrid(jnp.arange(2), jnp.arange(NSA_GROUP), jnp.arange(2), jnp.arange(HEAD_DIM),
                              indexing="ij")
    return (((2 * p + e) * NSA_GROUP + g) * HEAD_DIM + d).reshape(-1)


def _alibi_slopes():
    return 2.0 ** (-8.0 * jnp.arange(1, NSA_HEADS + 1, dtype=F32) / NSA_HEADS)


def _block_ones(width):
    i = jnp.arange(width)
    return (i[:, None] // HEAD_DIM == i[None, :] // HEAD_DIM).astype(BF16)


def _cmp_weights_bd(wk, wv):
    eye = jnp.eye(2, dtype=wk.dtype)
    bd = lambda w: jnp.einsum("hg,bde->bhdge", eye, w).reshape(CMP_BLOCK, LANES, LANES)
    return jnp.stack([bd(wk), bd(wv)]).astype(BF16)


def _pad_cols(a, width):
    return jnp.pad(a, ((0, 0), (0, width - a.shape[1])))


def kernel(x_prompt, x_sample, cache_nsa_kv, state_win_kv, state_rwkv, state_shift, page_table, norm_attn, w_in,
           nsa_w_cmp_k, nsa_w_cmp_v, nsa_out_gain, rwkv_mu, rwkv_w0, rwkv_w2, rwkv_a0, rwkv_a2, rwkv_g2, rwkv_k_k,
           rwkv_k_a, rwkv_r_k, rwkv_ln_w, rwkv_ln_b, w_out, norm_ffn, router_w, router_b, moe_w1, moe_b1, moe_w2,
           moe_b2, norm_final):
    nb_p, seq_p, _ = x_prompt.shape
    nb_s, seq_s, _ = x_sample.shape
    depth = norm_attn.shape[0]
    assert depth == 1 and seq_s == 8 and seq_p % (2 * Q_BLOCK) == 0
    tp, ts = nb_p * seq_p, nb_s * seq_s
    n_phys = cache_nsa_kv.shape[1]
    w_len = state_win_kv.shape[2]
    l = 0
    xp, xs = x_prompt.reshape(tp, D_MODEL), x_sample.reshape(ts, D_MODEL)

    perm = _head_perm()
    w = w_in[l]
    c0, c1, c2 = NSA_WIDTH, NSA_WIDTH + 4 * KV_WIDTH, NSA_WIDTH + 6 * KV_WIDTH
    c3 = c2 + 3 * NSA_HEADS
    q, kvn, winn, gl, z = _in_proj(xp, xs, norm_attn[l], [
        (w[:, :c0] * (HEAD_DIM ** -0.5))[:, perm].astype(BF16), w[:, c0:c1].astype(BF16), w[:, c1:c2].astype(BF16),
        _pad_cols(w[:, c2:c3], LANES).astype(BF16), _pad_cols(w[:, c3:], Z_PAD).astype(BF16)])

    w_bd = _cmp_weights_bd(nsa_w_cmp_k[l], nsa_w_cmp_v[l])
    cache_rows = cache_nsa_kv[l].reshape(n_phys * PAGE_SIZE, 4 * KV_WIDTH)
    kvc_new = _compress(kvn, w_bd, tp // CMP_BLOCK, 256)
    blocks_phys = n_phys * (PAGE_SIZE // CMP_BLOCK)
    kvc_phys = _compress(cache_rows, w_bd, blocks_phys, 512)
    o_p = _nsa_prompt(q, gl, kvn, winn, kvc_new, jnp.stack([_slope_cols(2, Q_BLOCK, p) for p in range(2)]),
                      _gain_cols(nsa_out_gain[l]), _pair01_t(seq_p // CMP_BLOCK), nb_p, seq_p)
    n_cmp_pad = -(-(page_table.shape[1] * PAGE_SIZE // CMP_BLOCK) // LANES) * LANES
    o_s, new_win_s = _nsa_sample(
        page_table, q, gl, kvn, winn, state_win_kv[l].reshape(nb_s, w_len, 2 * KV_WIDTH), cache_rows,
        kvc_phys.reshape(n_phys, 1, (PAGE_SIZE // CMP_BLOCK) * 2 * KV_WIDTH),
        _sample_consts(nsa_out_gain[l]), _pair01_t(n_cmp_pad), tp)
    o_nsa = jnp.concatenate([o_p, o_s.astype(BF16)], axis=0)

    w2p, a2p, g2p, pat = _rwkv_consts(rwkv_w2[l], rwkv_a2[l], rwkv_g2[l])
    mu = jnp.pad(rwkv_mu[l], (0, Z_PAD - SHIFT_WIDTH))
    vecs = (mu, rwkv_w0[l], rwkv_a0[l], rwkv_k_k[l], rwkv_k_a[l], rwkv_r_k[l], rwkv_ln_w[l], rwkv_ln_b[l])
    mats = (w2p, a2p, g2p, _block_ones(LANES), pat, _block_ones(2 * LANES))
    o_rw_p, st_p = _rwkv_mixer(z, 0, nb_p, seq_p, nb_p, 128, None, None, vecs, mats)
    o_rw_s, st_s = _rwkv_mixer(z, tp, nb_s, seq_s, 4, seq_s, _pad_cols(state_shift[l], Z_PAD),
                               _state_to_tiles(state_rwkv[l]), vecs, mats)
    o_rwkv = jnp.concatenate([o_rw_p.reshape(tp, RWKV_WIDTH), o_rw_s.reshape(ts, RWKV_WIDTH)], axis=0).astype(BF16)

    wo = w_out[l]
    rw = _pad_cols(router_w[l], LANES)
    rw_hi = rw.astype(BF16)
    rw_lo = (rw - rw_hi.astype(F32)).astype(BF16)
    rb = jnp.concatenate([router_b[l].astype(F32), jnp.full((LANES - N_EXPERTS,), -1e30, F32)]).reshape(1, LANES)
    x1, h2, top_e, top_g = _outproj_router(xp, xs, o_nsa, o_rwkv, wo[:NSA_WIDTH][perm].astype(BF16),
                                           wo[NSA_WIDTH:].astype(BF16), norm_ffn[l].reshape(1, -1), rw_hi, rw_lo, rb)
    slot_tok, slot_of, blk_e, n_used, _ = _route(top_e[:, :MOE_TOPK], MOE_BM)
    xs_sorted = _moe_gather(slot_tok, h2, MOE_BM)
    act = _moe_up(blk_e, n_used, xs_sorted, moe_w1[l], moe_b1[l].reshape(N_EXPERTS, 1, 2 * D_FF), MOE_BM)
    yb = _moe_down(blk_e, n_used, act, moe_w2[l], moe_b2[l].reshape(N_EXPERTS, 1, D_MODEL), MOE_BM)
    y_p = _moe_combine(slot_of, x1, top_g, norm_final, yb, 0, tp)
    y_s = _moe_combine(slot_of, x1, top_g, norm_final, yb, tp, ts)

    hd = (NSA_KV_HEADS, HEAD_DIM)
    kv_p = kvn[:tp].reshape(1, nb_p, seq_p, 4, *hd)
    kv_s = kvn[tp:].reshape(1, nb_s, seq_s, 4, *hd)
    win_keep = min(WINDOW, seq_p)
    win_p = winn[:tp].reshape(nb_p, seq_p, 2, *hd)[None, :, seq_p - win_keep:]
    win_s = new_win_s.reshape(1, nb_s, w_len, 2, *hd)
    sh_p = z[seq_p - 1:tp:seq_p, :SHIFT_WIDTH][None]
    sh_s = z[tp + seq_s - 1::seq_s, :SHIFT_WIDTH][None]
    return (y_p.reshape(nb_p, seq_p, D_MODEL), y_s.reshape(nb_s, seq_s, D_MODEL), kv_p, kv_s, win_p, win_s,
            _tiles_to_state(st_p)[None], _tiles_to_state(st_s)[None], sh_p, sh_s)
```

```python
import functools

import jax
import jax.numpy as jnp
from jax import lax
from jax.experimental import pallas as pl
from jax.experimental.pallas import tpu as pltpu

F32 = jnp.float32
BF16 = jnp.bfloat16

D_MODEL = 2048
HEAD_DIM = 64
NSA_HEADS = 16
NSA_KV_HEADS = 4
NSA_GROUP = 4
NSA_WIDTH = 1024
KV_WIDTH = 256
CMP_BLOCK = 32
SEL_BLOCK = 64
SEL_TOPK = 16
WINDOW = 512
Q_BLOCK = 128
PAGE_SIZE = 128
RWKV_HEADS = 16
RWKV_WIDTH = 1024
DECAY_LORA = 64
ICLR_LORA = 64
GATE_LORA = 160
SHIFT_WIDTH = 3 * RWKV_WIDTH + DECAY_LORA + ICLR_LORA + GATE_LORA
Z_PAD = 3456
N_EXPERTS = 32
MOE_TOPK = 4
D_FF = 2048
SWIGLU_LIMIT = 7.0
SWIGLU_ALPHA = 1.702
NORM_EPS = 1e-5
GN_EPS = 64e-5
LANES = 128
VMEM_LIMIT = 56 * 1024 * 1024
NEG = -1e30


def _cparams(*sem):
    return pltpu.CompilerParams(dimension_semantics=sem, vmem_limit_bytes=VMEM_LIMIT)


def _dot(a, b):
    return jnp.dot(a, b, preferred_element_type=F32)


def _split3(x):
    hi = x.astype(BF16)
    r1 = x - hi.astype(F32)
    mid = r1.astype(BF16)
    lo = (r1 - mid.astype(F32)).astype(BF16)
    return hi, mid, lo


def _dot_exact01(x, m01):
    hi, mid, lo = _split3(x)
    return _dot(hi, m01) + _dot(mid, m01) + _dot(lo, m01)


def _rows_of(xp_ref, xs_ref, n_p):
    return jnp.where(pl.program_id(0) < n_p, xp_ref[...], xs_ref[...])


def _row_specs(tm, n_p, width):
    return [pl.BlockSpec((tm, width), lambda i: (jnp.minimum(i, n_p - 1), 0)),
            pl.BlockSpec((tm, width), lambda i: (jnp.maximum(i - n_p, 0), 0))]


def _proj_kernel(xp_ref, xs_ref, g_ref, *refs, n_p):
    n = len(refs) // 2
    x = _rows_of(xp_ref, xs_ref, n_p)
    ms = jnp.mean(x * x, axis=-1, keepdims=True)
    h = (x * lax.rsqrt(ms + NORM_EPS) * g_ref[...]).astype(BF16)
    for w_ref, o_ref in zip(refs[:n], refs[n:]):
        o_ref[...] = _dot(h, w_ref[...])


def _in_proj(xp, xs, g, weights, tm=256):
    t = xp.shape[0] + xs.shape[0]
    n_p = xp.shape[0] // tm
    once = functools.partial(pl.BlockSpec, pipeline_mode=pl.Buffered(1))
    return pl.pallas_call(
        functools.partial(_proj_kernel, n_p=n_p),
        grid=(t // tm,),
        in_specs=_row_specs(tm, n_p, D_MODEL) + [once((1, D_MODEL), lambda i: (0, 0))]
        + [once(w.shape, lambda i: (0, 0)) for w in weights],
        out_specs=[pl.BlockSpec((tm, w.shape[1]), lambda i: (i, 0)) for w in weights],
        out_shape=[jax.ShapeDtypeStruct((t, w.shape[1]), F32) for w in weights],
        compiler_params=_cparams("parallel"),
        name="in_proj",
    )(xp, xs, g.reshape(1, -1), *weights)


def _compress_kernel(x_ref, w_ref, o_ref):
    nb = o_ref.shape[0]
    acc = jnp.zeros((nb, LANES), F32)
    for b in range(CMP_BLOCK):
        a = x_ref[pl.ds(b, nb, stride=CMP_BLOCK), :].astype(BF16)
        acc = acc + _dot(a, w_ref[0, b])
    o_ref[...] = acc


def _compress(rows, w_bd, n_blocks, nb):
    slabs = 2 * KV_WIDTH // LANES
    return pl.pallas_call(
        _compress_kernel,
        grid=(n_blocks // nb, slabs),
        in_specs=[
            pl.BlockSpec((nb * CMP_BLOCK, LANES), lambda i, s: (i, s)),
            pl.BlockSpec((1, CMP_BLOCK, LANES, LANES), lambda i, s: (s // 2, 0, 0, 0)),
        ],
        out_specs=pl.BlockSpec((nb, LANES), lambda i, s: (i, s)),
        out_shape=jax.ShapeDtypeStruct((n_blocks, 2 * KV_WIDTH), F32),
        compiler_params=_cparams("parallel", "arbitrary"),
        name="nsa_compress",
    )(rows, w_bd)


def _dot01_left(m01, x):
    hi, mid, lo = _split3(x)
    return _dot(m01, hi) + _dot(m01, mid) + _dot(m01, lo)


def _softmax0(s, mask):
    s = jnp.where(mask, s, -jnp.inf)
    m = jnp.max(s, axis=0, keepdims=True)
    m = jnp.where(jnp.isfinite(m), m, 0.0)
    e = jnp.exp(s - m)
    return e / jnp.maximum(jnp.sum(e, axis=0, keepdims=True), 1e-30)


def _select_blocks_t(imp, cur, n_rank):
    j = lax.broadcasted_iota(jnp.int32, (imp.shape[0], 1), 0)
    forced = (j == cur) | (j == 0)
    key = jnp.where(forced, jnp.inf, jnp.where(j < cur, imp, -jnp.inf))
    rank = jnp.zeros(key.shape, F32)
    for i in range(n_rank):
        ri = key[i:i + 1, :]
        ahead = (ri > key) | ((ri == key) & (j > i))
        rank = rank + jnp.where(ahead, 1.0, 0.0)
    return jnp.where(rank < SEL_TOPK, 1.0, 0.0)


def _dyn_row(x, r):
    rows = lax.broadcasted_iota(jnp.int32, (x.shape[0], 1), 0)
    return jnp.sum(jnp.where(rows == r, x, 0.0), axis=0, keepdims=True)


def _nsa_prompt_kernel(q_ref, gl_ref, ks_ref, vs_ref, kw_ref, vw_ref, kc_ref, vc_ref, slope_ref, gain_ref,
                       pair_ref, o_ref, m_ref, l_ref, acc_ref, b0_ref, *, seq):
    ne, tq = 2, Q_BLOCK
    ncol = NSA_GROUP * ne * tq
    cet = ne * tq
    n_cmp = seq // CMP_BLOCK
    tk = 2 * Q_BLOCK
    i = pl.program_id(1)
    p = pl.program_id(2)
    p0 = i * tq

    qt = q_ref[...].T
    sub = lax.broadcasted_iota(jnp.int32, (ne * HEAD_DIM, 1), 0)
    cols = []
    for g in range(NSA_GROUP):
        qg = qt[g * LANES:(g + 1) * LANES]
        for e in range(ne):
            cols.append(jnp.where((sub >= e * HEAD_DIM) & (sub < (e + 1) * HEAD_DIM), qg, 0.0))
    qbd = jnp.concatenate(cols, axis=1).astype(BF16)

    lane = lax.broadcasted_iota(jnp.int32, (1, ncol), 1)
    tloc = lane & (tq - 1)
    tpos = p0 + tloc
    slope = slope_ref[0]

    s = _dot(kc_ref[...].astype(BF16), qbd)
    cmp_end = (lax.broadcasted_iota(jnp.int32, (n_cmp, 1), 0) + 1) * CMP_BLOCK - 1
    dist = (tpos - cmp_end).astype(F32)
    pc = _softmax0(s - slope * dist, dist >= 0)
    o_c = _dot(vc_ref[...].T.astype(BF16), pc.astype(BF16))

    psum = pc[:, 0:cet]
    for g in range(1, NSA_GROUP):
        psum = psum + pc[:, g * cet:(g + 1) * cet]
    n_sel = seq // SEL_BLOCK
    imp = _dot01_left(pair_ref[...], psum)
    sel = _select_blocks_t(imp, jnp.right_shift(tpos[:, 0:cet], 6), n_sel).astype(BF16)

    ksub = lax.broadcasted_iota(jnp.int32, (tk, 1), 0)
    b0_ref[...] = slope * (tloc - ksub).astype(F32)
    m_ref[...] = jnp.full(m_ref.shape, NEG, F32)
    l_ref[...] = jnp.zeros(l_ref.shape, F32)
    acc_ref[...] = jnp.zeros(acc_ref.shape, F32)
    jrow = lax.broadcasted_iota(jnp.int32, (1, n_sel), 1)

    def tile(k0, causal):
        kt = ks_ref[pl.ds(k0, tk), :].astype(BF16)
        vt = vs_ref[pl.ds(k0, tk), :].T.astype(BF16)
        st = _dot(kt, qbd)
        expand = jnp.where(jnp.right_shift(k0 + ksub, 6) == jrow, 1.0, 0.0).astype(BF16)
        chosen = _dot(expand, sel) > 0.5
        off = (p0 - k0).astype(F32)
        if causal:
            chosen = chosen & ((tloc[:, 0:cet] - ksub + (p0 - k0)) >= 0)
        cb = jnp.where(chosen, 0.0, -jnp.inf)
        cb = jnp.concatenate([cb] * NSA_GROUP, axis=1)
        s2 = st - b0_ref[...] - slope * off + cb
        m_old = m_ref[...]
        m_new = jnp.maximum(m_old, jnp.max(s2, axis=0, keepdims=True))
        alpha = jnp.exp(m_old - m_new)
        pt = jnp.exp(s2 - m_new)
        l_ref[...] = alpha * l_ref[...] + jnp.sum(pt, axis=0, keepdims=True)
        acc_ref[...] = alpha * acc_ref[...] + _dot(vt, pt.astype(BF16))
        m_ref[...] = m_new

    n_past = lax.shift_right_logical(i, 1)

    def pair(t, carry):
        tile(pl.multiple_of(2 * t * tk, tk), False)
        tile(pl.multiple_of((2 * t + 1) * tk, tk), False)
        return carry

    def single(t, carry):
        tile(pl.multiple_of((n_past - 1) * tk, tk), False)
        return carry

    lax.fori_loop(0, lax.shift_right_logical(n_past, 1), pair, 0)
    lax.fori_loop(0, n_past & 1, single, 0)
    tile(pl.multiple_of(n_past * tk, tk), True)
    o_s = acc_ref[...] / jnp.maximum(l_ref[...], 1e-30)

    start = pl.multiple_of(jnp.maximum(p0 - WINDOW, 0), Q_BLOCK)
    wk = WINDOW + tq
    sw = _dot(kw_ref[pl.ds(start, wk), :].astype(BF16), qbd)
    kpos = start + lax.broadcasted_iota(jnp.int32, (wk, 1), 0)
    dw = tpos - kpos
    pw = _softmax0(sw - slope * dw.astype(F32), (dw >= 0) & (dw <= WINDOW))
    o_w = _dot(vw_ref[pl.ds(start, wk), :].T.astype(BF16), pw.astype(BF16))

    gate = jax.nn.sigmoid(gl_ref[...].T)
    for g in range(NSA_GROUP):
        halves = []
        for e in range(ne):
            c0 = (g * ne + e) * tq
            r0 = e * HEAD_DIM
            x = None
            for br, o in enumerate((o_c, o_s, o_w)):
                hrow = ((2 * p + e) * NSA_GROUP + g) * 3 + br
                gt = _dyn_row(gate, hrow)
                term = gt * o[r0:r0 + HEAD_DIM, c0:c0 + tq]
                x = term if x is None else x + term
            ms = jnp.mean(x * x, axis=0, keepdims=True)
            halves.append(x * lax.rsqrt(ms + NORM_EPS) * gain_ref[0, g * ne + e])
        o_ref[:, g * LANES:(g + 1) * LANES] = jnp.concatenate(halves, axis=0).T.astype(o_ref.dtype)


def _nsa_prompt(q, gl, kvn, winn, kvc, slopes, gain, pair01, n, seq):
    nq = seq // Q_BLOCK
    n_cmp = seq // CMP_BLOCK
    n_sel = seq // SEL_BLOCK
    ncol = NSA_GROUP * 2 * Q_BLOCK
    kern = functools.partial(_nsa_prompt_kernel, seq=seq)
    return pl.pallas_call(
        kern,
        grid=(n, nq, 2),
        in_specs=[
            pl.BlockSpec((Q_BLOCK, 512), lambda b, i, p: (b * nq + i, p)),
            pl.BlockSpec((Q_BLOCK, LANES), lambda b, i, p: (b * nq + i, 0)),
            pl.BlockSpec((seq, LANES), lambda b, i, p: (b, 4 + p)),
            pl.BlockSpec((seq, LANES), lambda b, i, p: (b, 6 + p)),
            pl.BlockSpec((seq, LANES), lambda b, i, p: (b, p)),
            pl.BlockSpec((seq, LANES), lambda b, i, p: (b, 2 + p)),
            pl.BlockSpec((n_cmp, LANES), lambda b, i, p: (b, p)),
            pl.BlockSpec((n_cmp, LANES), lambda b, i, p: (b, 2 + p)),
            pl.BlockSpec((1, 1, ncol), lambda b, i, p: (p, 0, 0)),
            pl.BlockSpec((1, 8, HEAD_DIM, LANES), lambda b, i, p: (p, 0, 0, 0)),
            pl.BlockSpec((n_sel, n_cmp), lambda b, i, p: (0, 0)),
        ],
        out_specs=pl.BlockSpec((Q_BLOCK, 512), lambda b, i, p: (b * nq + i, p)),
        out_shape=jax.ShapeDtypeStruct((n * seq, NSA_WIDTH), BF16),
        scratch_shapes=[pltpu.VMEM((1, ncol), F32), pltpu.VMEM((1, ncol), F32), pltpu.VMEM((LANES, ncol), F32),
                        pltpu.VMEM((2 * Q_BLOCK, ncol), F32)],
        compiler_params=_cparams("parallel", "parallel", "parallel"),
        name="nsa_prompt",
    )(q, gl, kvn, kvn, winn, winn, kvc, kvc, slopes, gain, pair01)


def _slope_cols(ne, tq, pair):
    g, e, t = jnp.meshgrid(jnp.arange(NSA_GROUP), jnp.arange(ne), jnp.arange(tq), indexing="ij")
    head = (pair * ne + e) * NSA_GROUP + g
    return _alibi_slopes()[head].reshape(1, -1)


def _gain_cols(gain):
    gh = gain.reshape(2, 2, NSA_GROUP, HEAD_DIM).transpose(0, 2, 1, 3).reshape(2, 8, HEAD_DIM)
    return jnp.broadcast_to(gh[..., None], (2, 8, HEAD_DIM, LANES))


def _pair01_t(n_cmp):
    return (jnp.arange(n_cmp // 2)[:, None] == jnp.arange(n_cmp)[None, :] // 2).astype(BF16)


def _pad_rows(x, rows):
    return jnp.concatenate([x, jnp.zeros((rows - x.shape[0], x.shape[1]), x.dtype)], axis=0)


def _nsa_sample_kernel(pt_ref, q_ref, gl_ref, kvn_ref, winn_ref, wst_ref, *rest, past_len):
    del pt_ref
    n_pages = past_len // PAGE_SIZE
    page_refs = rest[:n_pages]
    kc_refs = rest[n_pages:2 * n_pages]
    (slope_ref, gain_ref, valid_ref, pair_ref, sumg_ref, tile_ref, gsel_ref,
     o_ref, nw_ref, kc_s, vc_s) = rest[2 * n_pages:]
    ne, tq = NSA_KV_HEADS, 8
    width = ne * HEAD_DIM
    ncol = NSA_GROUP * ne * tq
    blocks_per_page = PAGE_SIZE // CMP_BLOCK

    qf = q_ref[...]
    lane_w = lax.broadcasted_iota(jnp.int32, (1, width), 1)
    parts = []
    for g in range(NSA_GROUP):
        slab = jnp.concatenate([qf[:, g * LANES:(g + 1) * LANES],
                                qf[:, 4 * LANES + g * LANES:4 * LANES + (g + 1) * LANES]], axis=1)
        for e in range(ne):
            parts.append(jnp.where((lane_w >= e * HEAD_DIM) & (lane_w < (e + 1) * HEAD_DIM), slab, 0.0))
    qbd = jnp.concatenate(parts, axis=0).T.astype(BF16)

    lane = lax.broadcasted_iota(jnp.int32, (1, ncol), 1)
    tloc = lane & (tq - 1)
    tpos = past_len + tloc
    slope = slope_ref[...]

    kc_s[...] = jnp.zeros_like(kc_s)
    vc_s[...] = jnp.zeros_like(vc_s)
    for j in range(n_pages):
        for b in range(blocks_per_page):
            c = j * blocks_per_page + b
            kc_s[c:c + 1, :] = kc_refs[j][0, :, b * 2 * KV_WIDTH:b * 2 * KV_WIDTH + KV_WIDTH]
            vc_s[c:c + 1, :] = kc_refs[j][0, :, b * 2 * KV_WIDTH + KV_WIDTH:(b + 1) * 2 * KV_WIDTH]
    ncp = kc_s.shape[0]
    s = _dot(kc_s[...].astype(BF16), qbd)
    cmp_end = (lax.broadcasted_iota(jnp.int32, (ncp, 1), 0) + 1) * CMP_BLOCK - 1
    dist = (tpos - cmp_end).astype(F32)
    pc = _softmax0(s - slope * dist, dist >= 0)
    o_c = _dot(vc_s[...].T.astype(BF16), pc.astype(BF16))

    psum = _dot_exact01(pc, sumg_ref[...])
    imp = _dot01_left(pair_ref[...], psum)
    n_sel_past = past_len // SEL_BLOCK
    sel = _select_blocks_t(imp, jnp.right_shift(tpos, 6), n_sel_past + 1).astype(BF16)
    sel = _dot(sel, tile_ref[...]).astype(BF16)

    ksub = lax.broadcasted_iota(jnp.int32, (PAGE_SIZE, 1), 0)
    jrow = lax.broadcasted_iota(jnp.int32, (1, sel.shape[0]), 1)
    b0 = slope * (tloc - ksub).astype(F32)

    def scores(kt, k0, causal):
        st = _dot(kt.astype(BF16), qbd)
        expand = jnp.where(jnp.right_shift(k0 + ksub, 6) == jrow, 1.0, 0.0).astype(BF16)
        chosen = _dot(expand, sel) > 0.5
        if causal:
            chosen = chosen & ((tloc - ksub + (past_len - k0)) >= 0)
        return st - b0 - slope * float(past_len - k0) + jnp.where(chosen, 0.0, -jnp.inf)

    knew = _pad_rows(kvn_ref[:, 2 * KV_WIDTH:3 * KV_WIDTH], PAGE_SIZE)
    vnew = _pad_rows(kvn_ref[:, 3 * KV_WIDTH:4 * KV_WIDTH], PAGE_SIZE)
    s_all = [scores(page_refs[j][:, 0:KV_WIDTH], j * PAGE_SIZE, False) for j in range(n_pages)]
    s_all.append(scores(knew, past_len, True))
    m = s_all[0]
    for sj in s_all[1:]:
        m = jnp.maximum(m, sj)
    m = jnp.maximum(jnp.max(m, axis=0, keepdims=True), NEG)
    l = jnp.zeros((1, ncol), F32)
    acc = jnp.zeros((width, ncol), F32)
    for j, sj in enumerate(s_all):
        pt = jnp.exp(sj - m)
        vt = page_refs[j][:, KV_WIDTH:2 * KV_WIDTH] if j < n_pages else vnew
        l = l + jnp.sum(pt, axis=0, keepdims=True)
        acc = acc + _dot(vt.T.astype(BF16), pt.astype(BF16))
    o_s = acc / jnp.maximum(l, 1e-30)

    kw = jnp.concatenate([wst_ref[0, :, 0:KV_WIDTH], _pad_rows(winn_ref[:, 0:KV_WIDTH], PAGE_SIZE)], axis=0)
    vw = jnp.concatenate([wst_ref[0, :, KV_WIDTH:], _pad_rows(winn_ref[:, KV_WIDTH:], PAGE_SIZE)], axis=0)
    start = past_len - WINDOW
    sw = _dot(kw.astype(BF16), qbd)
    kpos = start + lax.broadcasted_iota(jnp.int32, (kw.shape[0], 1), 0)
    dw = tpos - kpos
    pw = _softmax0(sw - slope * dw.astype(F32), (dw >= 0) & (dw <= WINDOW))
    o_w = _dot(vw.T.astype(BF16), pw.astype(BF16))

    sg = jax.nn.sigmoid(gl_ref[...])
    trow = lax.broadcasted_iota(jnp.int32, (tq, 1), 0)
    x = None
    for br, o in enumerate((o_c, o_s, o_w)):
        spread = _dot_exact01(sg, gsel_ref[br])
        grow = jnp.sum(jnp.where(trow == tloc, spread, 0.0), axis=0, keepdims=True)
        x = grow * o if x is None else x + grow * o
    x = jnp.where(valid_ref[...] > 0.5, x, 0.0)
    ms = jnp.sum(x * x, axis=0, keepdims=True) * (1.0 / HEAD_DIM)
    y = (x * lax.rsqrt(ms + NORM_EPS) * gain_ref[...]).T
    for g in range(NSA_GROUP):
        og = y[g * ne * tq:g * ne * tq + tq]
        for e in range(1, ne):
            og = og + y[(g * ne + e) * tq:(g * ne + e + 1) * tq]
        o_ref[:, g * LANES:(g + 1) * LANES] = og[:, 0:LANES]
        o_ref[:, (NSA_GROUP + g) * LANES:(NSA_GROUP + g + 1) * LANES] = og[:, LANES:]

    w_len = wst_ref.shape[1]
    nw_ref[0, 0:w_len - tq, :] = wst_ref[0, tq:w_len, :]
    nw_ref[0, w_len - tq:w_len, :] = winn_ref[...]


def _sample_consts(gain):
    ne, tq = NSA_KV_HEADS, 8
    c = jnp.arange(NSA_GROUP * ne * tq)
    cg, ck = c // (ne * tq), (c // tq) % ne
    chead = ck * NSA_GROUP + cg
    r = jnp.arange(ne * HEAD_DIM)
    rk = r // HEAD_DIM
    slope = _alibi_slopes()[chead].reshape(1, -1)
    valid = (rk[:, None] == ck[None, :]).astype(F32)
    gain_t = gain.reshape(ne, NSA_GROUP, HEAD_DIM).transpose(0, 2, 1).reshape(ne * HEAD_DIM, NSA_GROUP)
    gain_t = jnp.repeat(gain_t, ne * tq, axis=1) * valid
    lanes = jnp.arange(LANES)
    sumg = ((lanes[None, :] < ne * tq) & (c[:, None] % (ne * tq) == lanes[None, :])).astype(BF16)
    tile = ((lanes[:, None] < ne * tq) & (lanes[:, None] == c[None, :] % (ne * tq))).astype(BF16)
    gsel = jnp.stack([(lanes[:, None] == (chead * 3 + br)[None, :]).astype(BF16) for br in range(3)])
    return slope, gain_t, valid, sumg, tile, gsel


def _nsa_sample(page_table, q, gl, kvn, winn, win_state, cache_rows, kvc_pages, consts, pair01, row0):
    slope, gain_t, valid, sumg, tile, gsel = consts
    n_seq, n_pages = page_table.shape
    past_len = n_pages * PAGE_SIZE
    w_len = win_state.shape[1]
    blk0 = row0 // 8
    tok = lambda s, pt: (blk0 + s, 0)
    full = lambda a: pl.BlockSpec(a.shape, lambda s, pt: (0,) * a.ndim)
    in_specs = [
        pl.BlockSpec((8, NSA_WIDTH), tok),
        pl.BlockSpec((8, LANES), tok),
        pl.BlockSpec((8, 4 * KV_WIDTH), tok),
        pl.BlockSpec((8, 2 * KV_WIDTH), tok),
        pl.BlockSpec((1, w_len, 2 * KV_WIDTH), lambda s, pt: (s, 0, 0)),
    ]
    in_specs += [pl.BlockSpec((PAGE_SIZE, 2 * KV_WIDTH), functools.partial(lambda s, pt, j: (pt[s, j], 1), j=j))
                 for j in range(n_pages)]
    in_specs += [pl.BlockSpec((1, 1, 8 * KV_WIDTH), functools.partial(lambda s, pt, j: (pt[s, j], 0, 0), j=j))
                 for j in range(n_pages)]
    in_specs += [full(slope), full(gain_t), full(valid), full(pair01), full(sumg), full(tile), full(gsel)]
    ncp = -(-(past_len // CMP_BLOCK) // LANES) * LANES
    grid_spec = pltpu.PrefetchScalarGridSpec(
        num_scalar_prefetch=1,
        grid=(n_seq,),
        in_specs=in_specs,
        out_specs=[pl.BlockSpec((8, NSA_WIDTH), lambda s, pt: (s, 0)),
                   pl.BlockSpec((1, w_len, 2 * KV_WIDTH), lambda s, pt: (s, 0, 0))],
        scratch_shapes=[pltpu.VMEM((ncp, KV_WIDTH), F32), pltpu.VMEM((ncp, KV_WIDTH), F32)],
    )
    return pl.pallas_call(
        functools.partial(_nsa_sample_kernel, past_len=past_len),
        grid_spec=grid_spec,
        out_shape=[jax.ShapeDtypeStruct((n_seq * 8, NSA_WIDTH), F32),
                   jax.ShapeDtypeStruct((n_seq, w_len, 2 * KV_WIDTH), F32)],
        compiler_params=_cparams("parallel"),
        name="nsa_sample",
    )(page_table, q, gl, kvn, winn, win_state, *([cache_rows] * n_pages), *([kvc_pages] * n_pages),
      slope, gain_t, valid, pair01, sumg, tile, gsel)


def _head_sums(x, bo01):
    return jnp.concatenate([_dot_exact01(x[:, s * LANES:(s + 1) * LANES], bo01)
                            for s in range(x.shape[1] // LANES)], axis=1)


def _rwkv_project(z, prev_row, mu, w0, a0, k_k, k_a, r_k, w2p, a2p, g2p, bo):
    row = lax.broadcasted_iota(jnp.int32, (z.shape[0], 1), 0)
    prev = jnp.where(row == 0, prev_row, pltpu.roll(z, 1, 0))
    zm = z + (prev - z) * mu
    w = RWKV_WIDTH
    r, k, v = zm[:, 0:w], zm[:, w:2 * w], zm[:, 2 * w:3 * w]
    wa = zm[:, 3 * w:3 * w + LANES]
    gd = zm[:, 3 * w + LANES:]
    neg = -(w0 + _dot(jnp.tanh(wa).astype(BF16), w2p))
    softplus = jnp.maximum(neg, 0.0) + jnp.log1p(jnp.exp(-jnp.abs(neg)))
    decay = jnp.exp(-jnp.exp(-softplus - 0.5))
    a = jax.nn.sigmoid(a0 + _dot(wa.astype(BF16), a2p))
    g = _dot(jax.nn.sigmoid(gd).astype(BF16), g2p)
    kk = k * k_k
    kk = kk / jnp.maximum(jnp.sqrt(_head_sums(kk * kk, bo)), 1e-12)
    k2 = k * (1.0 + (a - 1.0) * k_a)
    bonus = _head_sums(r * k2 * r_k, bo) * v
    return (r, decay, k2, v, kk, -(kk * a)), g, bonus


def _rwkv_kernel(*refs, nbatch, tc, has_init):
    z_refs = refs[:nbatch]
    pos = nbatch
    first_ref = s0_ref = None
    if has_init:
        first_ref, s0_ref = refs[pos], refs[pos + 1]
        pos += 2
    (mu_ref, w0_ref, a0_ref, kk_ref, ka_ref, rk_ref, lnw_ref, lnb_ref, w2_ref, a2_ref, g2_ref, bo128_ref, pat_ref,
     bo_ref) = refs[pos:pos + 14]
    pos += 14
    o_ref, s_out, st, ops_s, y_s, zlast = refs[pos:pos + 6]
    c = pl.program_id(1)
    nq = RWKV_HEADS // 4
    tw = 4 * HEAD_DIM

    @pl.when(c == 0)
    def _():
        if has_init:
            st[...] = s0_ref[...]
            zlast[...] = first_ref[0]
        else:
            st[...] = jnp.zeros_like(st)
            zlast[...] = jnp.zeros_like(zlast)

    gates, bonuses = [], []
    for b in range(nbatch):
        z = z_refs[b][...]
        ops, g, bonus = _rwkv_project(z, zlast[b:b + 1, :], mu_ref[...], w0_ref[...], a0_ref[...], kk_ref[...],
                                      ka_ref[...], rk_ref[...], w2_ref[...], a2_ref[...], g2_ref[...], bo128_ref[...])
        zlast[b:b + 1, :] = z[tc - 1:tc, :]
        for k, op in enumerate(ops):
            ops_s[k, b] = op
        gates.append(g)
        bonuses.append(bonus)

    pat = pat_ref[...]
    bo = bo_ref[...]

    def step(t, carry):
        tiles = [(b, q) for b in range(nbatch) for q in range(nq)]
        rowv = lambda k, b, q: ops_s[k, b, pl.ds(t, 1), q * tw:(q + 1) * tw]
        lhs = []
        for b, q in tiles:
            lhs.append((st[b, q] * rowv(4, b, q)).astype(BF16))
            lhs.append((pat * rowv(3, b, q)).astype(BF16))
        red = _dot(jnp.concatenate(lhs, axis=0), bo)
        outs = []
        for n, (b, q) in enumerate(tiles):
            sa = red[n * 2 * HEAD_DIM:n * 2 * HEAD_DIM + HEAD_DIM]
            vb = red[n * 2 * HEAD_DIM + HEAD_DIM:(n + 1) * 2 * HEAD_DIM]
            s2 = st[b, q] * rowv(1, b, q) + sa * rowv(5, b, q) + vb * rowv(2, b, q)
            st[b, q] = s2
            outs.append((s2 * rowv(0, b, q)).astype(BF16))
        yb = _dot(jnp.concatenate(outs, axis=0), bo)
        for n, (b, q) in enumerate(tiles):
            yrow = jnp.sum(yb[n * HEAD_DIM:(n + 1) * HEAD_DIM] * pat, axis=0, keepdims=True)
            y_s[b, pl.ds(t, 1), q * tw:(q + 1) * tw] = yrow
        return carry

    lax.fori_loop(0, tc, step, 0, unroll=4)

    for b in range(nbatch):
        y = y_s[b]
        d = y - _head_sums(y, bo128_ref[...]) * (1.0 / HEAD_DIM)
        var = _head_sums(d * d, bo128_ref[...]) * (1.0 / HEAD_DIM)
        o = (d * lax.rsqrt(var + GN_EPS) * lnw_ref[...] + lnb_ref[...] + bonuses[b]) * gates[b]
        o_ref[b] = o.astype(o_ref.dtype)

    @pl.when(c == pl.num_programs(1) - 1)
    def _():
        s_out[...] = st[...]


def _rwkv_mixer(z, row0, nseq, seq_rows, nbatch, tc, first, s0, vecs, mats):
    nchunk = seq_rows // tc
    nq = RWKV_HEADS // 4
    blk0 = row0 // tc
    rows = lambda b: pl.BlockSpec((tc, Z_PAD), lambda gi, c: (blk0 + (gi * nbatch + b) * nchunk + c, 0))
    st_spec = pl.BlockSpec((nbatch, nq, HEAD_DIM, 4 * HEAD_DIM), lambda gi, c: (gi, 0, 0, 0))
    const = lambda a: pl.BlockSpec(a.shape, lambda gi, c: (0,) * a.ndim)
    in_specs = [rows(b) for b in range(nbatch)]
    args = [z] * nbatch
    if s0 is not None:
        in_specs += [pl.BlockSpec((1, nbatch, Z_PAD), lambda gi, c: (gi, 0, 0)), st_spec]
        args += [first.reshape(nseq // nbatch, nbatch, Z_PAD), s0]
    consts = [v.reshape(1, -1) for v in vecs] + list(mats)
    in_specs += [const(a) for a in consts]
    args += consts
    return pl.pallas_call(
        functools.partial(_rwkv_kernel, nbatch=nbatch, tc=tc, has_init=s0 is not None),
        grid=(nseq // nbatch, nchunk),
        in_specs=in_specs,
        out_specs=[pl.BlockSpec((nbatch, tc, RWKV_WIDTH), lambda gi, c: (gi, c, 0)), st_spec],
        out_shape=[jax.ShapeDtypeStruct((nseq, seq_rows, RWKV_WIDTH), F32),
                   jax.ShapeDtypeStruct((nseq, nq, HEAD_DIM, 4 * HEAD_DIM), F32)],
        scratch_shapes=[pltpu.VMEM((nbatch, nq, HEAD_DIM, 4 * HEAD_DIM), F32),
                        pltpu.VMEM((6, nbatch, tc, RWKV_WIDTH), F32), pltpu.VMEM((nbatch, tc, RWKV_WIDTH), F32),
                        pltpu.VMEM((nbatch, Z_PAD), F32)],
        compiler_params=_cparams("parallel", "arbitrary"),
        name="rwkv_mixer",
    )(*args)


def _rwkv_consts(w2, a2, g2):
    w2p = jnp.concatenate([w2, jnp.zeros((LANES - DECAY_LORA, RWKV_WIDTH), w2.dtype)], axis=0)
    a2p = jnp.concatenate([jnp.zeros((DECAY_LORA, RWKV_WIDTH), a2.dtype), a2], axis=0)
    g2p = jnp.concatenate([g2, jnp.zeros((2 * LANES - GATE_LORA, RWKV_WIDTH), g2.dtype)], axis=0)
    v = jnp.arange(HEAD_DIM)
    pat = (v[:, None] == (jnp.arange(4 * HEAD_DIM)[None, :] % HEAD_DIM)).astype(F32)
    return w2p.astype(BF16), a2p.astype(BF16), g2p.astype(BF16), pat


def _state_to_tiles(s):
    n = s.shape[0]
    return s.reshape(n, 4, 4, HEAD_DIM, HEAD_DIM).transpose(0, 1, 3, 2, 4).reshape(n, 4, HEAD_DIM, 4 * HEAD_DIM)


def _tiles_to_state(t):
    n = t.shape[0]
    return t.reshape(n, 4, HEAD_DIM, 4, HEAD_DIM).transpose(0, 1, 3, 2, 4).reshape(n, RWKV_HEADS, HEAD_DIM, HEAD_DIM)


def _outproj_kernel(xp_ref, xs_ref, on_ref, or_ref, wa_ref, wb_ref, g_ref, rwh_ref, rwl_ref, rb_ref,
                    x1_ref, h2_ref, te_ref, tg_ref, *, n_p):
    x1 = _rows_of(xp_ref, xs_ref, n_p) + _dot(on_ref[...], wa_ref[...]) + _dot(or_ref[...], wb_ref[...])
    x1_ref[...] = x1
    ms = jnp.mean(x1 * x1, axis=-1, keepdims=True)
    hf = x1 * lax.rsqrt(ms + NORM_EPS) * g_ref[...]
    h2_ref[...] = hf
    hh = hf.astype(BF16)
    hl = (hf - hh.astype(F32)).astype(BF16)
    logits = _dot(hh, rwh_ref[...]) + _dot(hl, rwh_ref[...]) + _dot(hh, rwl_ref[...]) + rb_ref[...]
    lane = lax.broadcasted_iota(jnp.int32, logits.shape, 1).astype(F32)
    vals, idxs = [], []
    for _ in range(MOE_TOPK):
        m = jnp.max(logits, axis=-1, keepdims=True)
        idx = jnp.min(jnp.where(logits == m, lane, float(LANES)), axis=-1, keepdims=True)
        vals.append(m)
        idxs.append(idx)
        logits = jnp.where(lane == idx, -jnp.inf, logits)
    es = [jnp.exp(v - vals[0]) for v in vals]
    denom = es[0] + es[1] + es[2] + es[3]
    te = jnp.zeros(logits.shape, F32)
    tg = jnp.zeros(logits.shape, F32)
    for k in range(MOE_TOPK):
        te = jnp.where(lane == float(k), idxs[k], te)
        tg = jnp.where(lane == float(k), es[k] / denom, tg)
    te_ref[...] = te.astype(jnp.int32)
    tg_ref[...] = tg


def _outproj_router(xp, xs, o_nsa, o_rwkv, wa, wb, g, rw_hi, rw_lo, rb, tm=256):
    t = xp.shape[0] + xs.shape[0]
    n_p = xp.shape[0] // tm
    tok = lambda w: pl.BlockSpec((tm, w), lambda i: (i, 0))
    full = lambda a: pl.BlockSpec(a.shape, lambda i: (0,) * a.ndim)
    return pl.pallas_call(
        functools.partial(_outproj_kernel, n_p=n_p),
        grid=(t // tm,),
        in_specs=_row_specs(tm, n_p, D_MODEL) + [tok(NSA_WIDTH), tok(RWKV_WIDTH), full(wa), full(wb), full(g),
                                                 full(rw_hi), full(rw_lo), full(rb)],
        out_specs=[tok(D_MODEL), tok(D_MODEL), tok(LANES), tok(LANES)],
        out_shape=[jax.ShapeDtypeStruct((t, D_MODEL), F32), jax.ShapeDtypeStruct((t, D_MODEL), F32),
                   jax.ShapeDtypeStruct((t, LANES), jnp.int32), jax.ShapeDtypeStruct((t, LANES), F32)],
        compiler_params=_cparams("parallel"),
        name="outproj_router",
    )(xp, xs, o_nsa, o_rwkv, wa, wb, g, rw_hi, rw_lo, rb)


MOE_BM = 256


def _route(top_e, bm):
    n_tok = top_e.shape[0]
    flat_e = top_e.reshape(-1)
    n_assign = flat_e.shape[0]
    onehot = (flat_e[:, None] == jnp.arange(N_EXPERTS, dtype=jnp.int32)[None, :]).astype(jnp.int32)
    cum = jnp.cumsum(onehot, axis=0)
    counts = cum[-1]
    pos = jnp.take_along_axis(cum, flat_e[:, None], axis=1)[:, 0] - 1
    padded = (counts + bm - 1) // bm * bm
    pad_end = jnp.cumsum(padded)
    dest = (pad_end - padded)[flat_e] + pos
    n_blocks = -(-n_assign // bm) + N_EXPERTS
    n_used = (pad_end[-1] // bm).astype(jnp.int32)
    blk = jnp.minimum(jnp.arange(n_blocks, dtype=jnp.int32), n_used - 1)
    blk_e = jnp.minimum(jnp.sum((pad_end[None, :] <= (blk * bm)[:, None]).astype(jnp.int32), axis=1), N_EXPERTS - 1)
    return dest.reshape(n_tok, MOE_TOPK).astype(jnp.int32), blk_e, n_used.reshape(1), n_blocks * bm


def _row_copy(src_ref, dst_ref, sem, src_row, dst_row):
    return pltpu.make_async_copy(src_ref.at[pl.ds(src_row, 1)], dst_ref.at[pl.ds(dst_row, 1)], sem)


def _gather_issue(idx_ref, src_ref, buf, sem):
    n = buf.shape[0]

    def issue(r2, c):
        for par in range(2):
            r = 2 * r2 + par
            _row_copy(src_ref, buf, sem, idx_ref[0, 0, r], r).start(priority=par)
        return c

    lax.fori_loop(0, n // 2, issue, 0, unroll=4)


def _gather_wait(src_ref, buf, sem):
    pltpu.make_async_copy(src_ref.at[pl.ds(0, buf.shape[0])], buf, sem).wait()


def _gather_pipelined(cur_ref, nxt_ref, src_ref, buf, sem, b, nsteps):
    slot = b & 1

    @pl.when(b == 0)
    def _():
        _gather_issue(cur_ref, src_ref, buf.at[0], sem.at[0])

    @pl.when(b + 1 < nsteps)
    def _():
        _gather_issue(nxt_ref, src_ref, buf.at[1 - slot], sem.at[1 - slot])

    _gather_wait(src_ref, buf.at[slot], sem.at[slot])
    return slot


def _moe_scatter_kernel(dst_ref, h_ref, init_ref, o_ref, sem):
    del init_ref
    n = dst_ref.shape[2]
    tm = h_ref.shape[0]

    def issue(r2, c):
        for par in range(2):
            r = 2 * r2 + par
            pltpu.make_async_copy(h_ref.at[pl.ds(r & (tm - 1), 1)], o_ref.at[pl.ds(dst_ref[0, 0, r], 1)],
                                  sem).start(priority=par)
        return c

    lax.fori_loop(0, n // 2, issue, 0, unroll=4)
    pltpu.make_async_copy(o_ref.at[pl.ds(0, n)], o_ref.at[pl.ds(0, n)], sem).wait()


def _moe_scatter(slot_of, h2, n_slots, tm=256):
    t = h2.shape[0]
    nt = t // tm
    dst = slot_of.reshape(nt, tm, MOE_TOPK).transpose(0, 2, 1).reshape(nt, 1, MOE_TOPK * tm)
    init = jnp.zeros((n_slots, D_MODEL), F32)
    return pl.pallas_call(
        _moe_scatter_kernel,
        grid=(nt,),
        in_specs=[pl.BlockSpec((1, 1, MOE_TOPK * tm), lambda i: (i, 0, 0), memory_space=pltpu.SMEM),
                  pl.BlockSpec((tm, D_MODEL), lambda i: (i, 0)),
                  pl.BlockSpec(memory_space=pl.ANY)],
        out_specs=pl.BlockSpec(memory_space=pl.ANY),
        out_shape=jax.ShapeDtypeStruct((n_slots, D_MODEL), F32),
        scratch_shapes=[pltpu.SemaphoreType.DMA(())],
        input_output_aliases={2: 0},
        compiler_params=_cparams("arbitrary"),
        name="moe_scatter",
    )(dst, h2, init)


MOE_FT = 1024
MOE_NF = D_FF // MOE_FT


def _expert_changed(be_ref, b):
    prev = be_ref[jnp.maximum(b - 1, 0)]
    return (b == 0) | (be_ref[b] != prev)


def _moe_up_kernel(be_ref, nu_ref, x_ref, wg_ref, wl_ref, bg_ref, bl_ref, o_ref, wg_s, wl_s):
    b = pl.program_id(1)

    @pl.when(b < nu_ref[0])
    def _():
        @pl.when(_expert_changed(be_ref, b))
        def _():
            wg_s[...] = wg_ref[0].astype(BF16)
            wl_s[...] = wl_ref[0].astype(BF16)

        x = x_ref[...].astype(BF16)
        glu = jnp.minimum(_dot(x, wg_s[...]) + bg_ref[0], SWIGLU_LIMIT)
        lin = jnp.clip(_dot(x, wl_s[...]) + bl_ref[0], -SWIGLU_LIMIT, SWIGLU_LIMIT)
        o_ref[...] = (glu * jax.nn.sigmoid(SWIGLU_ALPHA * glu) * (lin + 1.0)).astype(o_ref.dtype)

    @pl.when(b >= nu_ref[0])
    def _():
        o_ref[...] = jnp.zeros_like(o_ref)


def _moe_up(blk_e, n_used, xs, w1, b1, bm):
    n_slots = xs.shape[0]
    nblk = n_slots // bm
    live = lambda b, nu: jnp.minimum(b, nu[0] - 1)
    grid_spec = pltpu.PrefetchScalarGridSpec(
        num_scalar_prefetch=2,
        grid=(MOE_NF, nblk),
        in_specs=[
            pl.BlockSpec((bm, D_MODEL), lambda f, b, be, nu: (live(b, nu), 0)),
            pl.BlockSpec((1, D_MODEL, MOE_FT), lambda f, b, be, nu: (be[b], 0, f)),
            pl.BlockSpec((1, D_MODEL, MOE_FT), lambda f, b, be, nu: (be[b], 0, MOE_NF + f)),
            pl.BlockSpec((1, 1, MOE_FT), lambda f, b, be, nu: (be[b], 0, f)),
            pl.BlockSpec((1, 1, MOE_FT), lambda f, b, be, nu: (be[b], 0, MOE_NF + f)),
        ],
        out_specs=pl.BlockSpec((bm, MOE_FT), lambda f, b, be, nu: (b, f)),
        scratch_shapes=[pltpu.VMEM((D_MODEL, MOE_FT), BF16), pltpu.VMEM((D_MODEL, MOE_FT), BF16)],
    )
    return pl.pallas_call(
        _moe_up_kernel,
        grid_spec=grid_spec,
        out_shape=jax.ShapeDtypeStruct((n_slots, D_FF), BF16),
        compiler_params=_cparams("arbitrary", "arbitrary"),
        name="moe_up",
    )(blk_e, n_used, xs, w1, w1, b1, b1)


def _moe_down_kernel(be_ref, nu_ref, a_ref, w_ref, b_ref, o_ref, w_s):
    b = pl.program_id(0)

    @pl.when(b < nu_ref[0])
    def _():
        @pl.when(_expert_changed(be_ref, b))
        def _():
            w_s[...] = w_ref[0].astype(BF16)

        o_ref[...] = _dot(a_ref[...], w_s[...]) + b_ref[0]

    @pl.when(b >= nu_ref[0])
    def _():
        o_ref[...] = jnp.zeros_like(o_ref)


def _moe_down(blk_e, n_used, act, w2, b2, bm):
    n_slots = act.shape[0]
    nblk = n_slots // bm
    live = lambda b, nu: jnp.minimum(b, nu[0] - 1)
    grid_spec = pltpu.PrefetchScalarGridSpec(
        num_scalar_prefetch=2,
        grid=(nblk,),
        in_specs=[
            pl.BlockSpec((bm, D_FF), lambda b, be, nu: (live(b, nu), 0)),
            pl.BlockSpec((1, D_FF, D_MODEL), lambda b, be, nu: (be[b], 0, 0)),
            pl.BlockSpec((1, 1, D_MODEL), lambda b, be, nu: (be[b], 0, 0)),
        ],
        out_specs=pl.BlockSpec((bm, D_MODEL), lambda b, be, nu: (b, 0)),
        scratch_shapes=[pltpu.VMEM((D_FF, D_MODEL), BF16)],
    )
    return pl.pallas_call(
        _moe_down_kernel,
        grid_spec=grid_spec,
        out_shape=jax.ShapeDtypeStruct((n_slots, D_MODEL), F32),
        compiler_params=_cparams("arbitrary"),
        name="moe_down",
    )(blk_e, n_used, act, w2, b2)


def _combine_kernel(cur_ref, nxt_ref, x_ref, tg_ref, g_ref, oh_ref, yb_ref, o_ref, buf, sem, *, nsteps):
    tm = x_ref.shape[0]
    slot = _gather_pipelined(cur_ref, nxt_ref, yb_ref, buf, sem, pl.program_id(0), nsteps)
    x = x_ref[...]
    gates = tg_ref[...]
    for k in range(MOE_TOPK):
        gk = _dot_exact01(gates, oh_ref[k])
        x = x + jnp.concatenate([gk] * (D_MODEL // LANES), axis=1) * buf[slot, k * tm:(k + 1) * tm]
    ms = jnp.mean(x * x, axis=-1, keepdims=True)
    o_ref[...] = x * lax.rsqrt(ms + NORM_EPS) * g_ref[...]


def _moe_combine(slot_of, x1, tg, g, yb, row0, t, tm=128):
    nt = t // tm
    blk0 = row0 // tm
    slots = slot_of[row0:row0 + t].reshape(nt, tm, MOE_TOPK).transpose(0, 2, 1).reshape(nt, 1, MOE_TOPK * tm)
    onehot = (jnp.arange(LANES)[None, :, None] == jnp.arange(MOE_TOPK)[:, None, None]).astype(BF16)
    onehot = jnp.broadcast_to(onehot, (MOE_TOPK, LANES, LANES))
    return pl.pallas_call(
        functools.partial(_combine_kernel, nsteps=nt),
        grid=(nt,),
        in_specs=[pl.BlockSpec((1, 1, MOE_TOPK * tm), lambda i: (i, 0, 0), memory_space=pltpu.SMEM),
                  pl.BlockSpec((1, 1, MOE_TOPK * tm), lambda i: (jnp.minimum(i + 1, nt - 1), 0, 0),
                               memory_space=pltpu.SMEM),
                  pl.BlockSpec((tm, D_MODEL), lambda i: (blk0 + i, 0)),
                  pl.BlockSpec((tm, LANES), lambda i: (blk0 + i, 0)),
                  pl.BlockSpec((1, D_MODEL), lambda i: (0, 0)),
                  pl.BlockSpec((MOE_TOPK, LANES, LANES), lambda i: (0, 0, 0)),
                  pl.BlockSpec(memory_space=pl.ANY)],
        out_specs=pl.BlockSpec((tm, D_MODEL), lambda i: (i, 0)),
        out_shape=jax.ShapeDtypeStruct((t, D_MODEL), F32),
        scratch_shapes=[pltpu.VMEM((2, MOE_TOPK * tm, D_MODEL), F32), pltpu.SemaphoreType.DMA((2,))],
        compiler_params=_cparams("arbitrary"),
        name="moe_combine",
    )(slots, slots, x1, tg, g.reshape(1, -1), onehot, yb)


def _head_perm():
    p, g, e, d = jnp.meshgrid(jnp.arange(2), jnp.arange(NSA_GROUP), jnp.arange(2), jnp.arange(HEAD_DIM),
                              indexing="ij")
    return (((2 * p + e) * NSA_GROUP + g) * HEAD_DIM + d).reshape(-1)


def _alibi_slopes():
    return 2.0 ** (-8.0 * jnp.arange(1, NSA_HEADS + 1, dtype=F32) / NSA_HEADS)


def _block_ones(width):
    i = jnp.arange(width)
    return (i[:, None] // HEAD_DIM == i[None, :] // HEAD_DIM).astype(BF16)


def _cmp_weights_bd(wk, wv):
    eye = jnp.eye(2, dtype=wk.dtype)
    bd = lambda w: jnp.einsum("hg,bde->bhdge", eye, w).reshape(CMP_BLOCK, LANES, LANES)
    return jnp.stack([bd(wk), bd(wv)]).astype(BF16)


def _pad_cols(a, width):
    return jnp.pad(a, ((0, 0), (0, width - a.shape[1])))


def kernel(x_prompt, x_sample, cache_nsa_kv, state_win_kv, state_rwkv, state_shift, page_table, norm_attn, w_in,
           nsa_w_cmp_k, nsa_w_cmp_v, nsa_out_gain, rwkv_mu, rwkv_w0, rwkv_w2, rwkv_a0, rwkv_a2, rwkv_g2, rwkv_k_k,
           rwkv_k_a, rwkv_r_k, rwkv_ln_w, rwkv_ln_b, w_out, norm_ffn, router_w, router_b, moe_w1, moe_b1, moe_w2,
           moe_b2, norm_final):
    nb_p, seq_p, _ = x_prompt.shape
    nb_s, seq_s, _ = x_sample.shape
    depth = norm_attn.shape[0]
    assert depth == 1 and seq_s == 8 and seq_p % (2 * Q_BLOCK) == 0
    tp, ts = nb_p * seq_p, nb_s * seq_s
    n_phys = cache_nsa_kv.shape[1]
    w_len = state_win_kv.shape[2]
    l = 0
    xp, xs = x_prompt.reshape(tp, D_MODEL), x_sample.reshape(ts, D_MODEL)

    perm = _head_perm()
    w = w_in[l]
    c0, c1, c2 = NSA_WIDTH, NSA_WIDTH + 4 * KV_WIDTH, NSA_WIDTH + 6 * KV_WIDTH
    c3 = c2 + 3 * NSA_HEADS
    q, kvn, winn, gl, z = _in_proj(xp, xs, norm_attn[l], [
        (w[:, :c0] * (HEAD_DIM ** -0.5))[:, perm].astype(BF16), w[:, c0:c1].astype(BF16), w[:, c1:c2].astype(BF16),
        _pad_cols(w[:, c2:c3], LANES).astype(BF16), _pad_cols(w[:, c3:], Z_PAD).astype(BF16)])

    w_bd = _cmp_weights_bd(nsa_w_cmp_k[l], nsa_w_cmp_v[l])
    cache_rows = cache_nsa_kv[l].reshape(n_phys * PAGE_SIZE, 4 * KV_WIDTH)
    kvc_new = _compress(kvn, w_bd, tp // CMP_BLOCK, 256)
    blocks_phys = n_phys * (PAGE_SIZE // CMP_BLOCK)
    kvc_phys = _compress(cache_rows, w_bd, blocks_phys, 512)
    o_p = _nsa_prompt(q, gl, kvn, winn, kvc_new, jnp.stack([_slope_cols(2, Q_BLOCK, p) for p in range(2)]),
                      _gain_cols(nsa_out_gain[l]), _pair01_t(seq_p // CMP_BLOCK), nb_p, seq_p)
    n_cmp_pad = -(-(page_table.shape[1] * PAGE_SIZE // CMP_BLOCK) // LANES) * LANES
    o_s, new_win_s = _nsa_sample(
        page_table, q, gl, kvn, winn, state_win_kv[l].reshape(nb_s, w_len, 2 * KV_WIDTH), cache_rows,
        kvc_phys.reshape(n_phys, 1, (PAGE_SIZE // CMP_BLOCK) * 2 * KV_WIDTH),
        _sample_consts(nsa_out_gain[l]), _pair01_t(n_cmp_pad), tp)
    o_nsa = jnp.concatenate([o_p, o_s.astype(BF16)], axis=0)

    w2p, a2p, g2p, pat = _rwkv_consts(rwkv_w2[l], rwkv_a2[l], rwkv_g2[l])
    mu = jnp.pad(rwkv_mu[l], (0, Z_PAD - SHIFT_WIDTH))
    vecs = (mu, rwkv_w0[l], rwkv_a0[l], rwkv_k_k[l], rwkv_k_a[l], rwkv_r_k[l], rwkv_ln_w[l], rwkv_ln_b[l])
    mats = (w2p, a2p, g2p, _block_ones(LANES), pat, _block_ones(2 * LANES))
    o_rw_p, st_p = _rwkv_mixer(z, 0, nb_p, seq_p, nb_p, 128, None, None, vecs, mats)
    o_rw_s, st_s = _rwkv_mixer(z, tp, nb_s, seq_s, 4, seq_s, _pad_cols(state_shift[l], Z_PAD),
                               _state_to_tiles(state_rwkv[l]), vecs, mats)
    o_rwkv = jnp.concatenate([o_rw_p.reshape(tp, RWKV_WIDTH), o_rw_s.reshape(ts, RWKV_WIDTH)], axis=0).astype(BF16)

    wo = w_out[l]
    rw = _pad_cols(router_w[l], LANES)
    rw_hi = rw.astype(BF16)
    rw_lo = (rw - rw_hi.astype(F32)).astype(BF16)
    rb = jnp.concatenate([router_b[l].astype(F32), jnp.full((LANES - N_EXPERTS,), -1e30, F32)]).reshape(1, LANES)
    x1, h2, top_e, top_g = _outproj_router(xp, xs, o_nsa, o_rwkv, wo[:NSA_WIDTH][perm].astype(BF16),
                                           wo[NSA_WIDTH:].astype(BF16), norm_ffn[l].reshape(1, -1), rw_hi, rw_lo, rb)
    slot_of, blk_e, n_used, n_slots = _route(top_e[:, :MOE_TOPK], MOE_BM)
    xs_sorted = _moe_scatter(slot_of, h2, n_slots)
    act = _moe_up(blk_e, n_used, xs_sorted, moe_w1[l], moe_b1[l].reshape(N_EXPERTS, 1, 2 * D_FF), MOE_BM)
    yb = _moe_down(blk_e, n_used, act, moe_w2[l], moe_b2[l].reshape(N_EXPERTS, 1, D_MODEL), MOE_BM)
    y_p = _moe_combine(slot_of, x1, top_g, norm_final, yb, 0, tp)
    y_s = _moe_combine(slot_of, x1, top_g, norm_final, yb, tp, ts)

    hd = (NSA_KV_HEADS, HEAD_DIM)
    kv_p = kvn[:tp].reshape(1, nb_p, seq_p, 4, *hd)
    kv_s = kvn[tp:].reshape(1, nb_s, seq_s, 4, *hd)
    win_keep = min(WINDOW, seq_p)
    win_p = winn[:tp].reshape(nb_p, seq_p, 2, *hd)[None, :, seq_p - win_keep:]
    win_s = new_win_s.reshape(1, nb_s, w_len, 2, *hd)
    sh_p = z[seq_p - 1:tp:seq_p, :SHIFT_WIDTH][None]
    sh_s = z[tp + seq_s - 1::seq_s, :SHIFT_WIDTH][None]
    return (y_p.reshape(nb_p, seq_p, D_MODEL), y_s.reshape(nb_s, seq_s, D_MODEL), kv_p, kv_s, win_p, win_s,
            _tiles_to_state(st_p)[None], _tiles_to_state(st_s)[None], sh_p, sh_s)
```

```python
import functools

import jax
import jax.numpy as jnp
from jax import lax
from jax.experimental import pallas as pl
from jax.experimental.pallas import tpu as pltpu

F32 = jnp.float32
BF16 = jnp.bfloat16

D_MODEL = 2048
HEAD_DIM = 64
NSA_HEADS = 16
NSA_KV_HEADS = 4
NSA_GROUP = 4
NSA_WIDTH = 1024
KV_WIDTH = 256
CMP_BLOCK = 32
SEL_BLOCK = 64
SEL_TOPK = 16
WINDOW = 512
Q_BLOCK = 128
PAGE_SIZE = 128
RWKV_HEADS = 16
RWKV_WIDTH = 1024
DECAY_LORA = 64
ICLR_LORA = 64
GATE_LORA = 160
SHIFT_WIDTH = 3 * RWKV_WIDTH + DECAY_LORA + ICLR_LORA + GATE_LORA
Z_PAD = 3456
N_EXPERTS = 32
MOE_TOPK = 4
D_FF = 2048
SWIGLU_LIMIT = 7.0
SWIGLU_ALPHA = 1.702
NORM_EPS = 1e-5
GN_EPS = 64e-5
LANES = 128
VMEM_LIMIT = 56 * 1024 * 1024
NEG = -1e30


def _cparams(*sem):
    return pltpu.CompilerParams(dimension_semantics=sem, vmem_limit_bytes=VMEM_LIMIT)


def _dot(a, b):
    return jnp.dot(a, b, preferred_element_type=F32)


def _split3(x):
    hi = x.astype(BF16)
    r1 = x - hi.astype(F32)
    mid = r1.astype(BF16)
    lo = (r1 - mid.astype(F32)).astype(BF16)
    return hi, mid, lo


def _dot_exact01(x, m01):
    hi, mid, lo = _split3(x)
    return _dot(hi, m01) + _dot(mid, m01) + _dot(lo, m01)


def _rows_of(xp_ref, xs_ref, n_p):
    return jnp.where(pl.program_id(0) < n_p, xp_ref[...], xs_ref[...])


def _row_specs(tm, n_p, width):
    return [pl.BlockSpec((tm, width), lambda i: (jnp.minimum(i, n_p - 1), 0)),
            pl.BlockSpec((tm, width), lambda i: (jnp.maximum(i - n_p, 0), 0))]


def _proj_kernel(xp_ref, xs_ref, g_ref, *refs, n_p):
    n = len(refs) // 2
    x = _rows_of(xp_ref, xs_ref, n_p)
    ms = jnp.mean(x * x, axis=-1, keepdims=True)
    h = (x * lax.rsqrt(ms + NORM_EPS) * g_ref[...]).astype(BF16)
    for w_ref, o_ref in zip(refs[:n], refs[n:]):
        o_ref[...] = _dot(h, w_ref[...])


def _in_proj(xp, xs, g, weights, tm=256):
    t = xp.shape[0] + xs.shape[0]
    n_p = xp.shape[0] // tm
    once = functools.partial(pl.BlockSpec, pipeline_mode=pl.Buffered(1))
    return pl.pallas_call(
        functools.partial(_proj_kernel, n_p=n_p),
        grid=(t // tm,),
        in_specs=_row_specs(tm, n_p, D_MODEL) + [once((1, D_MODEL), lambda i: (0, 0))]
        + [once(w.shape, lambda i: (0, 0)) for w in weights],
        out_specs=[pl.BlockSpec((tm, w.shape[1]), lambda i: (i, 0)) for w in weights],
        out_shape=[jax.ShapeDtypeStruct((t, w.shape[1]), F32) for w in weights],
        compiler_params=_cparams("parallel"),
        name="in_proj",
    )(xp, xs, g.reshape(1, -1), *weights)


def _compress_kernel(x_ref, w_ref, o_ref):
    nb = o_ref.shape[0]
    acc = jnp.zeros((nb, LANES), F32)
    for b in range(CMP_BLOCK):
        a = x_ref[pl.ds(b, nb, stride=CMP_BLOCK), :].astype(BF16)
        acc = acc + _dot(a, w_ref[0, b])
    o_ref[...] = acc


def _compress(rows, w_bd, n_blocks, nb, comp_rows=None):
    slabs = 2 * KV_WIDTH // LANES
    if comp_rows is None:
        rows_map = lambda i, s: (i, s)
    else:
        comp_blocks = comp_rows // (nb * CMP_BLOCK)
        rows_map = lambda i, s: ((s // 2) * comp_blocks + i, s % 2)
    return pl.pallas_call(
        _compress_kernel,
        grid=(n_blocks // nb, slabs),
        in_specs=[
            pl.BlockSpec((nb * CMP_BLOCK, LANES), rows_map),
            pl.BlockSpec((1, CMP_BLOCK, LANES, LANES), lambda i, s: (s // 2, 0, 0, 0)),
        ],
        out_specs=pl.BlockSpec((nb, LANES), lambda i, s: (i, s)),
        out_shape=jax.ShapeDtypeStruct((n_blocks, 2 * KV_WIDTH), F32),
        compiler_params=_cparams("parallel", "arbitrary"),
        name="nsa_compress",
    )(rows, w_bd)


def _dot01_left(m01, x):
    hi, mid, lo = _split3(x)
    return _dot(m01, hi) + _dot(m01, mid) + _dot(m01, lo)


def _softmax0(s, mask):
    s = jnp.where(mask, s, -jnp.inf)
    m = jnp.max(s, axis=0, keepdims=True)
    m = jnp.where(jnp.isfinite(m), m, 0.0)
    e = jnp.exp(s - m)
    return e / jnp.maximum(jnp.sum(e, axis=0, keepdims=True), 1e-30)


def _select_blocks_t(imp, cur, n_rank):
    j = lax.broadcasted_iota(jnp.int32, (imp.shape[0], 1), 0)
    forced = (j == cur) | (j == 0)
    key = jnp.where(forced, jnp.inf, jnp.where(j < cur, imp, -jnp.inf))
    rank = jnp.zeros(key.shape, F32)
    for i in range(n_rank):
        ri = key[i:i + 1, :]
        ahead = (ri > key) | ((ri == key) & (j > i))
        rank = rank + jnp.where(ahead, 1.0, 0.0)
    return jnp.where(rank < SEL_TOPK, 1.0, 0.0)


def _dyn_row(x, r):
    rows = lax.broadcasted_iota(jnp.int32, (x.shape[0], 1), 0)
    return jnp.sum(jnp.where(rows == r, x, 0.0), axis=0, keepdims=True)


def _nsa_prompt_kernel(q_ref, gl_ref, ks_ref, vs_ref, kw_ref, vw_ref, kc_ref, vc_ref, slope_ref, gain_ref,
                       pair_ref, o_ref, m_ref, l_ref, acc_ref, b0_ref, *, seq):
    ne, tq = 2, Q_BLOCK
    ncol = NSA_GROUP * ne * tq
    cet = ne * tq
    n_cmp = seq // CMP_BLOCK
    tk = 2 * Q_BLOCK
    i = pl.program_id(1)
    p = pl.program_id(2)
    p0 = i * tq

    qt = q_ref[...].T
    sub = lax.broadcasted_iota(jnp.int32, (ne * HEAD_DIM, 1), 0)
    cols = []
    for g in range(NSA_GROUP):
        qg = qt[g * LANES:(g + 1) * LANES]
        for e in range(ne):
            cols.append(jnp.where((sub >= e * HEAD_DIM) & (sub < (e + 1) * HEAD_DIM), qg, 0.0))
    qbd = jnp.concatenate(cols, axis=1).astype(BF16)

    lane = lax.broadcasted_iota(jnp.int32, (1, ncol), 1)
    tloc = lane & (tq - 1)
    tpos = p0 + tloc
    slope = slope_ref[0]

    s = _dot(kc_ref[...].astype(BF16), qbd)
    cmp_end = (lax.broadcasted_iota(jnp.int32, (n_cmp, 1), 0) + 1) * CMP_BLOCK - 1
    dist = (tpos - cmp_end).astype(F32)
    pc = _softmax0(s - slope * dist, dist >= 0)
    o_c = _dot(vc_ref[...].T.astype(BF16), pc.astype(BF16))

    psum = pc[:, 0:cet]
    for g in range(1, NSA_GROUP):
        psum = psum + pc[:, g * cet:(g + 1) * cet]
    n_sel = seq // SEL_BLOCK
    imp = _dot01_left(pair_ref[...], psum)
    sel = _select_blocks_t(imp, jnp.right_shift(tpos[:, 0:cet], 6), n_sel).astype(BF16)

    ksub = lax.broadcasted_iota(jnp.int32, (tk, 1), 0)
    b0_ref[...] = slope * (tloc - ksub).astype(F32)
    m_ref[...] = jnp.full(m_ref.shape, NEG, F32)
    l_ref[...] = jnp.zeros(l_ref.shape, F32)
    acc_ref[...] = jnp.zeros(acc_ref.shape, F32)
    jrow = lax.broadcasted_iota(jnp.int32, (1, n_sel), 1)

    def tile(k0, causal):
        kt = ks_ref[pl.ds(k0, tk), :].astype(BF16)
        vt = vs_ref[pl.ds(k0, tk), :].T.astype(BF16)
        st = _dot(kt, qbd)
        expand = jnp.where(jnp.right_shift(k0 + ksub, 6) == jrow, 1.0, 0.0).astype(BF16)
        chosen = _dot(expand, sel) > 0.5
        off = (p0 - k0).astype(F32)
        if causal:
            chosen = chosen & ((tloc[:, 0:cet] - ksub + (p0 - k0)) >= 0)
        cb = jnp.where(chosen, 0.0, -jnp.inf)
        cb = jnp.concatenate([cb] * NSA_GROUP, axis=1)
        s2 = st - b0_ref[...] - slope * off + cb
        m_old = m_ref[...]
        m_new = jnp.maximum(m_old, jnp.max(s2, axis=0, keepdims=True))
        alpha = jnp.exp(m_old - m_new)
        pt = jnp.exp(s2 - m_new)
        l_ref[...] = alpha * l_ref[...] + jnp.sum(pt, axis=0, keepdims=True)
        acc_ref[...] = alpha * acc_ref[...] + _dot(vt, pt.astype(BF16))
        m_ref[...] = m_new

    n_past = lax.shift_right_logical(i, 1)

    def pair(t, carry):
        tile(pl.multiple_of(2 * t * tk, tk), False)
        tile(pl.multiple_of((2 * t + 1) * tk, tk), False)
        return carry

    def single(t, carry):
        tile(pl.multiple_of((n_past - 1) * tk, tk), False)
        return carry

    lax.fori_loop(0, lax.shift_right_logical(n_past, 1), pair, 0)
    lax.fori_loop(0, n_past & 1, single, 0)
    tile(pl.multiple_of(n_past * tk, tk), True)
    o_s = acc_ref[...] / jnp.maximum(l_ref[...], 1e-30)

    start = pl.multiple_of(jnp.maximum(p0 - WINDOW, 0), Q_BLOCK)
    wk = WINDOW + tq
    sw = _dot(kw_ref[pl.ds(start, wk), :].astype(BF16), qbd)
    kpos = start + lax.broadcasted_iota(jnp.int32, (wk, 1), 0)
    dw = tpos - kpos
    pw = _softmax0(sw - slope * dw.astype(F32), (dw >= 0) & (dw <= WINDOW))
    o_w = _dot(vw_ref[pl.ds(start, wk), :].T.astype(BF16), pw.astype(BF16))

    gate = jax.nn.sigmoid(gl_ref[...].T)
    for g in range(NSA_GROUP):
        halves = []
        for e in range(ne):
            c0 = (g * ne + e) * tq
            r0 = e * HEAD_DIM
            x = None
            for br, o in enumerate((o_c, o_s, o_w)):
                hrow = ((2 * p + e) * NSA_GROUP + g) * 3 + br
                gt = _dyn_row(gate, hrow)
                term = gt * o[r0:r0 + HEAD_DIM, c0:c0 + tq]
                x = term if x is None else x + term
            ms = jnp.mean(x * x, axis=0, keepdims=True)
            halves.append(x * lax.rsqrt(ms + NORM_EPS) * gain_ref[0, g * ne + e])
        o_ref[:, g * LANES:(g + 1) * LANES] = jnp.concatenate(halves, axis=0).T.astype(o_ref.dtype)


def _nsa_prompt(q, gl, kvn, winn, kvc, slopes, gain, pair01, n, seq):
    nq = seq // Q_BLOCK
    n_cmp = seq // CMP_BLOCK
    n_sel = seq // SEL_BLOCK
    ncol = NSA_GROUP * 2 * Q_BLOCK
    kern = functools.partial(_nsa_prompt_kernel, seq=seq)
    return pl.pallas_call(
        kern,
        grid=(n, nq, 2),
        in_specs=[
            pl.BlockSpec((Q_BLOCK, 512), lambda b, i, p: (b * nq + i, p)),
            pl.BlockSpec((Q_BLOCK, LANES), lambda b, i, p: (b * nq + i, 0)),
            pl.BlockSpec((seq, LANES), lambda b, i, p: (b, 4 + p)),
            pl.BlockSpec((seq, LANES), lambda b, i, p: (b, 6 + p)),
            pl.BlockSpec((seq, LANES), lambda b, i, p: (b, p)),
            pl.BlockSpec((seq, LANES), lambda b, i, p: (b, 2 + p)),
            pl.BlockSpec((n_cmp, LANES), lambda b, i, p: (b, p)),
            pl.BlockSpec((n_cmp, LANES), lambda b, i, p: (b, 2 + p)),
            pl.BlockSpec((1, 1, ncol), lambda b, i, p: (p, 0, 0)),
            pl.BlockSpec((1, 8, HEAD_DIM, LANES), lambda b, i, p: (p, 0, 0, 0)),
            pl.BlockSpec((n_sel, n_cmp), lambda b, i, p: (0, 0)),
        ],
        out_specs=pl.BlockSpec((Q_BLOCK, 512), lambda b, i, p: (b * nq + i, p)),
        out_shape=jax.ShapeDtypeStruct((n * seq, NSA_WIDTH), BF16),
        scratch_shapes=[pltpu.VMEM((1, ncol), F32), pltpu.VMEM((1, ncol), F32), pltpu.VMEM((LANES, ncol), F32),
                        pltpu.VMEM((2 * Q_BLOCK, ncol), F32)],
        compiler_params=_cparams("parallel", "parallel", "parallel"),
        name="nsa_prompt",
    )(q, gl, kvn, kvn, winn, winn, kvc, kvc, slopes, gain, pair01)


def _slope_cols(ne, tq, pair):
    g, e, t = jnp.meshgrid(jnp.arange(NSA_GROUP), jnp.arange(ne), jnp.arange(tq), indexing="ij")
    head = (pair * ne + e) * NSA_GROUP + g
    return _alibi_slopes()[head].reshape(1, -1)


def _gain_cols(gain):
    gh = gain.reshape(2, 2, NSA_GROUP, HEAD_DIM).transpose(0, 2, 1, 3).reshape(2, 8, HEAD_DIM)
    return jnp.broadcast_to(gh[..., None], (2, 8, HEAD_DIM, LANES))


def _pair01_t(n_cmp):
    return (jnp.arange(n_cmp // 2)[:, None] == jnp.arange(n_cmp)[None, :] // 2).astype(BF16)


def _pad_rows(x, rows):
    return jnp.concatenate([x, jnp.zeros((rows - x.shape[0], x.shape[1]), x.dtype)], axis=0)


def _nsa_sample_kernel(pt_ref, q_ref, gl_ref, kvn_ref, winn_ref, wst_ref, *rest, past_len):
    del pt_ref
    n_pages = past_len // PAGE_SIZE
    pk_refs = rest[:n_pages]
    pv_refs = rest[n_pages:2 * n_pages]
    kc_refs = rest[2 * n_pages:3 * n_pages]
    (slope_ref, gain_ref, valid_ref, pair_ref, sumg_ref, tile_ref, gsel_ref,
     o_ref, nw_ref, kc_s, vc_s) = rest[3 * n_pages:]
    ne, tq = NSA_KV_HEADS, 8
    width = ne * HEAD_DIM
    ncol = NSA_GROUP * ne * tq
    blocks_per_page = PAGE_SIZE // CMP_BLOCK

    qf = q_ref[...]
    lane_w = lax.broadcasted_iota(jnp.int32, (1, width), 1)
    parts = []
    for g in range(NSA_GROUP):
        slab = jnp.concatenate([qf[:, g * LANES:(g + 1) * LANES],
                                qf[:, 4 * LANES + g * LANES:4 * LANES + (g + 1) * LANES]], axis=1)
        for e in range(ne):
            parts.append(jnp.where((lane_w >= e * HEAD_DIM) & (lane_w < (e + 1) * HEAD_DIM), slab, 0.0))
    qbd = jnp.concatenate(parts, axis=0).T.astype(BF16)

    lane = lax.broadcasted_iota(jnp.int32, (1, ncol), 1)
    tloc = lane & (tq - 1)
    tpos = past_len + tloc
    slope = slope_ref[...]

    kc_s[...] = jnp.zeros_like(kc_s)
    vc_s[...] = jnp.zeros_like(vc_s)
    for j in range(n_pages):
        for b in range(blocks_per_page):
            c = j * blocks_per_page + b
            kc_s[c:c + 1, :] = kc_refs[j][0, :, b * 2 * KV_WIDTH:b * 2 * KV_WIDTH + KV_WIDTH]
            vc_s[c:c + 1, :] = kc_refs[j][0, :, b * 2 * KV_WIDTH + KV_WIDTH:(b + 1) * 2 * KV_WIDTH]
    ncp = kc_s.shape[0]
    s = _dot(kc_s[...].astype(BF16), qbd)
    cmp_end = (lax.broadcasted_iota(jnp.int32, (ncp, 1), 0) + 1) * CMP_BLOCK - 1
    dist = (tpos - cmp_end).astype(F32)
    pc = _softmax0(s - slope * dist, dist >= 0)
    o_c = _dot(vc_s[...].T.astype(BF16), pc.astype(BF16))

    psum = _dot_exact01(pc, sumg_ref[...])
    imp = _dot01_left(pair_ref[...], psum)
    n_sel_past = past_len // SEL_BLOCK
    sel = _select_blocks_t(imp, jnp.right_shift(tpos, 6), n_sel_past + 1).astype(BF16)
    sel = _dot(sel, tile_ref[...]).astype(BF16)

    ksub = lax.broadcasted_iota(jnp.int32, (PAGE_SIZE, 1), 0)
    jrow = lax.broadcasted_iota(jnp.int32, (1, sel.shape[0]), 1)
    b0 = slope * (tloc - ksub).astype(F32)

    def scores(kt, k0, causal):
        st = _dot(kt.astype(BF16), qbd)
        expand = jnp.where(jnp.right_shift(k0 + ksub, 6) == jrow, 1.0, 0.0).astype(BF16)
        chosen = _dot(expand, sel) > 0.5
        if causal:
            chosen = chosen & ((tloc - ksub + (past_len - k0)) >= 0)
        return st - b0 - slope * float(past_len - k0) + jnp.where(chosen, 0.0, -jnp.inf)

    knew = _pad_rows(kvn_ref[:, 2 * KV_WIDTH:3 * KV_WIDTH], PAGE_SIZE)
    vnew = _pad_rows(kvn_ref[:, 3 * KV_WIDTH:4 * KV_WIDTH], PAGE_SIZE)
    s_all = [scores(pk_refs[j][...], j * PAGE_SIZE, False) for j in range(n_pages)]
    s_all.append(scores(knew, past_len, True))
    m = s_all[0]
    for sj in s_all[1:]:
        m = jnp.maximum(m, sj)
    m = jnp.maximum(jnp.max(m, axis=0, keepdims=True), NEG)
    l = jnp.zeros((1, ncol), F32)
    acc = jnp.zeros((width, ncol), F32)
    for j, sj in enumerate(s_all):
        pt = jnp.exp(sj - m)
        vt = pv_refs[j][...] if j < n_pages else vnew
        l = l + jnp.sum(pt, axis=0, keepdims=True)
        acc = acc + _dot(vt.T.astype(BF16), pt.astype(BF16))
    o_s = acc / jnp.maximum(l, 1e-30)

    kw = jnp.concatenate([wst_ref[0, :, 0:KV_WIDTH], _pad_rows(winn_ref[:, 0:KV_WIDTH], PAGE_SIZE)], axis=0)
    vw = jnp.concatenate([wst_ref[0, :, KV_WIDTH:], _pad_rows(winn_ref[:, KV_WIDTH:], PAGE_SIZE)], axis=0)
    start = past_len - WINDOW
    sw = _dot(kw.astype(BF16), qbd)
    kpos = start + lax.broadcasted_iota(jnp.int32, (kw.shape[0], 1), 0)
    dw = tpos - kpos
    pw = _softmax0(sw - slope * dw.astype(F32), (dw >= 0) & (dw <= WINDOW))
    o_w = _dot(vw.T.astype(BF16), pw.astype(BF16))

    sg = jax.nn.sigmoid(gl_ref[...])
    trow = lax.broadcasted_iota(jnp.int32, (tq, 1), 0)
    x = None
    for br, o in enumerate((o_c, o_s, o_w)):
        spread = _dot_exact01(sg, gsel_ref[br])
        grow = jnp.sum(jnp.where(trow == tloc, spread, 0.0), axis=0, keepdims=True)
        x = grow * o if x is None else x + grow * o
    x = jnp.where(valid_ref[...] > 0.5, x, 0.0)
    ms = jnp.sum(x * x, axis=0, keepdims=True) * (1.0 / HEAD_DIM)
    y = (x * lax.rsqrt(ms + NORM_EPS) * gain_ref[...]).T
    for g in range(NSA_GROUP):
        og = y[g * ne * tq:g * ne * tq + tq]
        for e in range(1, ne):
            og = og + y[(g * ne + e) * tq:(g * ne + e + 1) * tq]
        o_ref[:, g * LANES:(g + 1) * LANES] = og[:, 0:LANES]
        o_ref[:, (NSA_GROUP + g) * LANES:(NSA_GROUP + g + 1) * LANES] = og[:, LANES:]

    w_len = wst_ref.shape[1]
    nw_ref[0, 0:w_len - tq, :] = wst_ref[0, tq:w_len, :]
    nw_ref[0, w_len - tq:w_len, :] = winn_ref[...]


def _sample_consts(gain):
    ne, tq = NSA_KV_HEADS, 8
    c = jnp.arange(NSA_GROUP * ne * tq)
    cg, ck = c // (ne * tq), (c // tq) % ne
    chead = ck * NSA_GROUP + cg
    r = jnp.arange(ne * HEAD_DIM)
    rk = r // HEAD_DIM
    slope = _alibi_slopes()[chead].reshape(1, -1)
    valid = (rk[:, None] == ck[None, :]).astype(F32)
    gain_t = gain.reshape(ne, NSA_GROUP, HEAD_DIM).transpose(0, 2, 1).reshape(ne * HEAD_DIM, NSA_GROUP)
    gain_t = jnp.repeat(gain_t, ne * tq, axis=1) * valid
    lanes = jnp.arange(LANES)
    sumg = ((lanes[None, :] < ne * tq) & (c[:, None] % (ne * tq) == lanes[None, :])).astype(BF16)
    tile = ((lanes[:, None] < ne * tq) & (lanes[:, None] == c[None, :] % (ne * tq))).astype(BF16)
    gsel = jnp.stack([(lanes[:, None] == (chead * 3 + br)[None, :]).astype(BF16) for br in range(3)])
    return slope, gain_t, valid, sumg, tile, gsel


def _nsa_sample(page_table, q, gl, kvn, winn, win_state, cache_rows, kvc_pages, consts, pair01, row0):
    slope, gain_t, valid, sumg, tile, gsel = consts
    n_seq, n_pages = page_table.shape
    past_len = n_pages * PAGE_SIZE
    w_len = win_state.shape[1]
    blk0 = row0 // 8
    tok = lambda s, pt: (blk0 + s, 0)
    full = lambda a: pl.BlockSpec(a.shape, lambda s, pt: (0,) * a.ndim)
    in_specs = [
        pl.BlockSpec((8, NSA_WIDTH), tok),
        pl.BlockSpec((8, LANES), tok),
        pl.BlockSpec((8, 4 * KV_WIDTH), tok),
        pl.BlockSpec((8, 2 * KV_WIDTH), tok),
        pl.BlockSpec((1, w_len, 2 * KV_WIDTH), lambda s, pt: (s, 0, 0)),
    ]
    n_phys = kvc_pages.shape[0]
    for comp in (2, 3):
        in_specs += [pl.BlockSpec((PAGE_SIZE, KV_WIDTH),
                                  functools.partial(lambda s, pt, j, c: (c * n_phys + pt[s, j], 0), j=j, c=comp))
                     for j in range(n_pages)]
    in_specs += [pl.BlockSpec((1, 1, 8 * KV_WIDTH), functools.partial(lambda s, pt, j: (pt[s, j], 0, 0), j=j))
                 for j in range(n_pages)]
    in_specs += [full(slope), full(gain_t), full(valid), full(pair01), full(sumg), full(tile), full(gsel)]
    ncp = -(-(past_len // CMP_BLOCK) // LANES) * LANES
    grid_spec = pltpu.PrefetchScalarGridSpec(
        num_scalar_prefetch=1,
        grid=(n_seq,),
        in_specs=in_specs,
        out_specs=[pl.BlockSpec((8, NSA_WIDTH), lambda s, pt: (s, 0)),
                   pl.BlockSpec((1, w_len, 2 * KV_WIDTH), lambda s, pt: (s, 0, 0))],
        scratch_shapes=[pltpu.VMEM((ncp, KV_WIDTH), F32), pltpu.VMEM((ncp, KV_WIDTH), F32)],
    )
    return pl.pallas_call(
        functools.partial(_nsa_sample_kernel, past_len=past_len),
        grid_spec=grid_spec,
        out_shape=[jax.ShapeDtypeStruct((n_seq * 8, NSA_WIDTH), F32),
                   jax.ShapeDtypeStruct((n_seq, w_len, 2 * KV_WIDTH), F32)],
        compiler_params=_cparams("parallel"),
        name="nsa_sample",
    )(page_table, q, gl, kvn, winn, win_state, *([cache_rows] * (2 * n_pages)), *([kvc_pages] * n_pages),
      slope, gain_t, valid, pair01, sumg, tile, gsel)


def _head_sums(x, bo01):
    return jnp.concatenate([_dot_exact01(x[:, s * LANES:(s + 1) * LANES], bo01)
                            for s in range(x.shape[1] // LANES)], axis=1)


def _rwkv_project(z, prev_row, mu, w0, a0, k_k, k_a, r_k, w2p, a2p, g2p, bo):
    row = lax.broadcasted_iota(jnp.int32, (z.shape[0], 1), 0)
    prev = jnp.where(row == 0, prev_row, pltpu.roll(z, 1, 0))
    zm = z + (prev - z) * mu
    w = RWKV_WIDTH
    r, k, v = zm[:, 0:w], zm[:, w:2 * w], zm[:, 2 * w:3 * w]
    wa = zm[:, 3 * w:3 * w + LANES]
    gd = zm[:, 3 * w + LANES:]
    neg = -(w0 + _dot(jnp.tanh(wa).astype(BF16), w2p))
    softplus = jnp.maximum(neg, 0.0) + jnp.log1p(jnp.exp(-jnp.abs(neg)))
    decay = jnp.exp(-jnp.exp(-softplus - 0.5))
    a = jax.nn.sigmoid(a0 + _dot(wa.astype(BF16), a2p))
    g = _dot(jax.nn.sigmoid(gd).astype(BF16), g2p)
    kk = k * k_k
    kk = kk / jnp.maximum(jnp.sqrt(_head_sums(kk * kk, bo)), 1e-12)
    k2 = k * (1.0 + (a - 1.0) * k_a)
    bonus = _head_sums(r * k2 * r_k, bo) * v
    return (r, decay, k2, v, kk, -(kk * a)), g, bonus


def _rwkv_kernel(*refs, nbatch, tc, has_init):
    z_refs = refs[:nbatch]
    pos = nbatch
    first_ref = s0_ref = None
    if has_init:
        first_ref, s0_ref = refs[pos], refs[pos + 1]
        pos += 2
    (mu_ref, w0_ref, a0_ref, kk_ref, ka_ref, rk_ref, lnw_ref, lnb_ref, w2_ref, a2_ref, g2_ref, bo128_ref, pat_ref,
     bo_ref) = refs[pos:pos + 14]
    pos += 14
    o_ref, s_out, st, ops_s, y_s, zlast = refs[pos:pos + 6]
    c = pl.program_id(1)
    nq = RWKV_HEADS // 4
    tw = 4 * HEAD_DIM

    @pl.when(c == 0)
    def _():
        if has_init:
            st[...] = s0_ref[...]
            zlast[...] = first_ref[0]
        else:
            st[...] = jnp.zeros_like(st)
            zlast[...] = jnp.zeros_like(zlast)

    gates, bonuses = [], []
    for b in range(nbatch):
        z = z_refs[b][...]
        ops, g, bonus = _rwkv_project(z, zlast[b:b + 1, :], mu_ref[...], w0_ref[...], a0_ref[...], kk_ref[...],
                                      ka_ref[...], rk_ref[...], w2_ref[...], a2_ref[...], g2_ref[...], bo128_ref[...])
        zlast[b:b + 1, :] = z[tc - 1:tc, :]
        for k, op in enumerate(ops):
            ops_s[k, b] = op
        gates.append(g)
        bonuses.append(bonus)

    pat = pat_ref[...]
    bo = bo_ref[...]

    def step(t, carry):
        tiles = [(b, q) for b in range(nbatch) for q in range(nq)]
        rowv = lambda k, b, q: ops_s[k, b, pl.ds(t, 1), q * tw:(q + 1) * tw]
        lhs = []
        for b, q in tiles:
            lhs.append((st[b, q] * rowv(4, b, q)).astype(BF16))
            lhs.append((pat * rowv(3, b, q)).astype(BF16))
        red = _dot(jnp.concatenate(lhs, axis=0), bo)
        outs = []
        for n, (b, q) in enumerate(tiles):
            sa = red[n * 2 * HEAD_DIM:n * 2 * HEAD_DIM + HEAD_DIM]
            vb = red[n * 2 * HEAD_DIM + HEAD_DIM:(n + 1) * 2 * HEAD_DIM]
            s2 = st[b, q] * rowv(1, b, q) + sa * rowv(5, b, q) + vb * rowv(2, b, q)
            st[b, q] = s2
            outs.append((s2 * rowv(0, b, q)).astype(BF16))
        yb = _dot(jnp.concatenate(outs, axis=0), bo)
        for n, (b, q) in enumerate(tiles):
            yrow = jnp.sum(yb[n * HEAD_DIM:(n + 1) * HEAD_DIM] * pat, axis=0, keepdims=True)
            y_s[b, pl.ds(t, 1), q * tw:(q + 1) * tw] = yrow
        return carry

    lax.fori_loop(0, tc, step, 0, unroll=4)

    for b in range(nbatch):
        y = y_s[b]
        d = y - _head_sums(y, bo128_ref[...]) * (1.0 / HEAD_DIM)
        var = _head_sums(d * d, bo128_ref[...]) * (1.0 / HEAD_DIM)
        o = (d * lax.rsqrt(var + GN_EPS) * lnw_ref[...] + lnb_ref[...] + bonuses[b]) * gates[b]
        o_ref[b] = o.astype(o_ref.dtype)

    @pl.when(c == pl.num_programs(1) - 1)
    def _():
        s_out[...] = st[...]


def _rwkv_mixer(z, row0, nseq, seq_rows, nbatch, tc, first, s0, vecs, mats):
    nchunk = seq_rows // tc
    nq = RWKV_HEADS // 4
    blk0 = row0 // tc
    rows = lambda b: pl.BlockSpec((tc, Z_PAD), lambda gi, c: (blk0 + (gi * nbatch + b) * nchunk + c, 0))
    st_spec = pl.BlockSpec((nbatch, nq, HEAD_DIM, 4 * HEAD_DIM), lambda gi, c: (gi, 0, 0, 0))
    const = lambda a: pl.BlockSpec(a.shape, lambda gi, c: (0,) * a.ndim)
    in_specs = [rows(b) for b in range(nbatch)]
    args = [z] * nbatch
    if s0 is not None:
        in_specs += [pl.BlockSpec((1, nbatch, Z_PAD), lambda gi, c: (gi, 0, 0)), st_spec]
        args += [first.reshape(nseq // nbatch, nbatch, Z_PAD), s0]
    consts = [v.reshape(1, -1) for v in vecs] + list(mats)
    in_specs += [const(a) for a in consts]
    args += consts
    return pl.pallas_call(
        functools.partial(_rwkv_kernel, nbatch=nbatch, tc=tc, has_init=s0 is not None),
        grid=(nseq // nbatch, nchunk),
        in_specs=in_specs,
        out_specs=[pl.BlockSpec((nbatch, tc, RWKV_WIDTH), lambda gi, c: (gi, c, 0)), st_spec],
        out_shape=[jax.ShapeDtypeStruct((nseq, seq_rows, RWKV_WIDTH), F32),
                   jax.ShapeDtypeStruct((nseq, nq, HEAD_DIM, 4 * HEAD_DIM), F32)],
        scratch_shapes=[pltpu.VMEM((nbatch, nq, HEAD_DIM, 4 * HEAD_DIM), F32),
                        pltpu.VMEM((6, nbatch, tc, RWKV_WIDTH), F32), pltpu.VMEM((nbatch, tc, RWKV_WIDTH), F32),
                        pltpu.VMEM((nbatch, Z_PAD), F32)],
        compiler_params=_cparams("parallel", "arbitrary"),
        name="rwkv_mixer",
    )(*args)


def _rwkv_consts(w2, a2, g2):
    w2p = jnp.concatenate([w2, jnp.zeros((LANES - DECAY_LORA, RWKV_WIDTH), w2.dtype)], axis=0)
    a2p = jnp.concatenate([jnp.zeros((DECAY_LORA, RWKV_WIDTH), a2.dtype), a2], axis=0)
    g2p = jnp.concatenate([g2, jnp.zeros((2 * LANES - GATE_LORA, RWKV_WIDTH), g2.dtype)], axis=0)
    v = jnp.arange(HEAD_DIM)
    pat = (v[:, None] == (jnp.arange(4 * HEAD_DIM)[None, :] % HEAD_DIM)).astype(F32)
    return w2p.astype(BF16), a2p.astype(BF16), g2p.astype(BF16), pat


def _state_to_tiles(s):
    n = s.shape[0]
    return s.reshape(n, 4, 4, HEAD_DIM, HEAD_DIM).transpose(0, 1, 3, 2, 4).reshape(n, 4, HEAD_DIM, 4 * HEAD_DIM)


def _tiles_to_state(t):
    n = t.shape[0]
    return t.reshape(n, 4, HEAD_DIM, 4, HEAD_DIM).transpose(0, 1, 3, 2, 4).reshape(n, RWKV_HEADS, HEAD_DIM, HEAD_DIM)


def _outproj_kernel(xp_ref, xs_ref, on_ref, or_ref, wa_ref, wb_ref, g_ref, rwh_ref, rwl_ref, rb_ref,
                    x1_ref, h2_ref, te_ref, tg_ref, *, n_p):
    x1 = _rows_of(xp_ref, xs_ref, n_p) + _dot(on_ref[...], wa_ref[...]) + _dot(or_ref[...], wb_ref[...])
    x1_ref[...] = x1
    ms = jnp.mean(x1 * x1, axis=-1, keepdims=True)
    hf = x1 * lax.rsqrt(ms + NORM_EPS) * g_ref[...]
    h2_ref[...] = hf
    hh = hf.astype(BF16)
    hl = (hf - hh.astype(F32)).astype(BF16)
    logits = _dot(hh, rwh_ref[...]) + _dot(hl, rwh_ref[...]) + _dot(hh, rwl_ref[...]) + rb_ref[...]
    lane = lax.broadcasted_iota(jnp.int32, logits.shape, 1).astype(F32)
    vals, idxs = [], []
    for _ in range(MOE_TOPK):
        m = jnp.max(logits, axis=-1, keepdims=True)
        idx = jnp.min(jnp.where(logits == m, lane, float(LANES)), axis=-1, keepdims=True)
        vals.append(m)
        idxs.append(idx)
        logits = jnp.where(lane == idx, -jnp.inf, logits)
    es = [jnp.exp(v - vals[0]) for v in vals]
    denom = es[0] + es[1] + es[2] + es[3]
    te = jnp.zeros(logits.shape, F32)
    tg = jnp.zeros(logits.shape, F32)
    for k in range(MOE_TOPK):
        te = jnp.where(lane == float(k), idxs[k], te)
        tg = jnp.where(lane == float(k), es[k] / denom, tg)
    te_ref[...] = te.astype(jnp.int32)
    tg_ref[...] = tg


def _outproj_router(xp, xs, o_nsa, o_rwkv, wa, wb, g, rw_hi, rw_lo, rb, tm=256):
    t = xp.shape[0] + xs.shape[0]
    n_p = xp.shape[0] // tm
    tok = lambda w: pl.BlockSpec((tm, w), lambda i: (i, 0))
    full = lambda a: pl.BlockSpec(a.shape, lambda i: (0,) * a.ndim)
    return pl.pallas_call(
        functools.partial(_outproj_kernel, n_p=n_p),
        grid=(t // tm,),
        in_specs=_row_specs(tm, n_p, D_MODEL) + [tok(NSA_WIDTH), tok(RWKV_WIDTH), full(wa), full(wb), full(g),
                                                 full(rw_hi), full(rw_lo), full(rb)],
        out_specs=[tok(D_MODEL), tok(D_MODEL), tok(LANES), tok(LANES)],
        out_shape=[jax.ShapeDtypeStruct((t, D_MODEL), F32), jax.ShapeDtypeStruct((t, D_MODEL), F32),
                   jax.ShapeDtypeStruct((t, LANES), jnp.int32), jax.ShapeDtypeStruct((t, LANES), F32)],
        compiler_params=_cparams("parallel"),
        name="outproj_router",
    )(xp, xs, o_nsa, o_rwkv, wa, wb, g, rw_hi, rw_lo, rb)


MOE_BM = 256


def _route(top_e, bm):
    n_tok = top_e.shape[0]
    flat_e = top_e.reshape(-1)
    n_assign = flat_e.shape[0]
    onehot = (flat_e[:, None] == jnp.arange(N_EXPERTS, dtype=jnp.int32)[None, :]).astype(jnp.int32)
    cum = jnp.cumsum(onehot, axis=0)
    counts = cum[-1]
    pos = jnp.take_along_axis(cum, flat_e[:, None], axis=1)[:, 0] - 1
    padded = (counts + bm - 1) // bm * bm
    pad_end = jnp.cumsum(padded)
    dest = (pad_end - padded)[flat_e] + pos
    n_blocks = -(-n_assign // bm) + N_EXPERTS
    n_used = (pad_end[-1] // bm).astype(jnp.int32)
    blk = jnp.minimum(jnp.arange(n_blocks, dtype=jnp.int32), n_used - 1)
    blk_e = jnp.minimum(jnp.sum((pad_end[None, :] <= (blk * bm)[:, None]).astype(jnp.int32), axis=1), N_EXPERTS - 1)
    return dest.reshape(n_tok, MOE_TOPK).astype(jnp.int32), blk_e, n_used.reshape(1), n_blocks * bm


def _row_copy(src_ref, dst_ref, sem, src_row, dst_row):
    return pltpu.make_async_copy(src_ref.at[pl.ds(src_row, 1)], dst_ref.at[pl.ds(dst_row, 1)], sem)


def _gather_issue(idx_ref, src_ref, buf, sem):
    n = buf.shape[0]

    def issue(r2, c):
        for par in range(2):
            r = 2 * r2 + par
            _row_copy(src_ref, buf, sem, idx_ref[0, 0, r], r).start(priority=par)
        return c

    lax.fori_loop(0, n // 2, issue, 0, unroll=4)


def _gather_wait(src_ref, buf, sem):
    pltpu.make_async_copy(src_ref.at[pl.ds(0, buf.shape[0])], buf, sem).wait()


def _gather_pipelined(cur_ref, nxt_ref, src_ref, buf, sem, b, nsteps):
    slot = b & 1

    @pl.when(b == 0)
    def _():
        _gather_issue(cur_ref, src_ref, buf.at[0], sem.at[0])

    @pl.when(b + 1 < nsteps)
    def _():
        _gather_issue(nxt_ref, src_ref, buf.at[1 - slot], sem.at[1 - slot])

    _gather_wait(src_ref, buf.at[slot], sem.at[slot])
    return slot


def _moe_scatter_kernel(dst_ref, h_ref, init_ref, o_ref, sem):
    del init_ref
    n = dst_ref.shape[2]
    tm = h_ref.shape[0]

    def issue(r2, c):
        for par in range(2):
            r = 2 * r2 + par
            pltpu.make_async_copy(h_ref.at[pl.ds(r & (tm - 1), 1)], o_ref.at[pl.ds(dst_ref[0, 0, r], 1)],
                                  sem).start(priority=par)
        return c

    lax.fori_loop(0, n // 2, issue, 0, unroll=4)
    pltpu.make_async_copy(o_ref.at[pl.ds(0, n)], o_ref.at[pl.ds(0, n)], sem).wait()


def _moe_scatter(slot_of, h2, n_slots, tm=256):
    t = h2.shape[0]
    nt = t // tm
    dst = slot_of.reshape(nt, tm, MOE_TOPK).transpose(0, 2, 1).reshape(nt, 1, MOE_TOPK * tm)
    init = jnp.zeros((n_slots, D_MODEL), F32)
    return pl.pallas_call(
        _moe_scatter_kernel,
        grid=(nt,),
        in_specs=[pl.BlockSpec((1, 1, MOE_TOPK * tm), lambda i: (i, 0, 0), memory_space=pltpu.SMEM),
                  pl.BlockSpec((tm, D_MODEL), lambda i: (i, 0)),
                  pl.BlockSpec(memory_space=pl.ANY)],
        out_specs=pl.BlockSpec(memory_space=pl.ANY),
        out_shape=jax.ShapeDtypeStruct((n_slots, D_MODEL), F32),
        scratch_shapes=[pltpu.SemaphoreType.DMA(())],
        input_output_aliases={2: 0},
        compiler_params=_cparams("arbitrary"),
        name="moe_scatter",
    )(dst, h2, init)


MOE_FT = 1024
MOE_NF = D_FF // MOE_FT


def _expert_changed(be_ref, b):
    prev = be_ref[jnp.maximum(b - 1, 0)]
    return (b == 0) | (be_ref[b] != prev)


def _moe_up_kernel(be_ref, nu_ref, x_ref, wg_ref, wl_ref, bg_ref, bl_ref, o_ref, wg_s, wl_s):
    b = pl.program_id(1)

    @pl.when(b < nu_ref[0])
    def _():
        @pl.when(_expert_changed(be_ref, b))
        def _():
            wg_s[...] = wg_ref[0].astype(BF16)
            wl_s[...] = wl_ref[0].astype(BF16)

        x = x_ref[...].astype(BF16)
        glu = jnp.minimum(_dot(x, wg_s[...]) + bg_ref[0], SWIGLU_LIMIT)
        lin = jnp.clip(_dot(x, wl_s[...]) + bl_ref[0], -SWIGLU_LIMIT, SWIGLU_LIMIT)
        o_ref[...] = (glu * jax.nn.sigmoid(SWIGLU_ALPHA * glu) * (lin + 1.0)).astype(o_ref.dtype)

    @pl.when(b >= nu_ref[0])
    def _():
        o_ref[...] = jnp.zeros_like(o_ref)


def _moe_up(blk_e, n_used, xs, w1, b1, bm):
    n_slots = xs.shape[0]
    nblk = n_slots // bm
    live = lambda b, nu: jnp.minimum(b, nu[0] - 1)
    grid_spec = pltpu.PrefetchScalarGridSpec(
        num_scalar_prefetch=2,
        grid=(MOE_NF, nblk),
        in_specs=[
            pl.BlockSpec((bm, D_MODEL), lambda f, b, be, nu: (live(b, nu), 0)),
            pl.BlockSpec((1, D_MODEL, MOE_FT), lambda f, b, be, nu: (be[b], 0, f)),
            pl.BlockSpec((1, D_MODEL, MOE_FT), lambda f, b, be, nu: (be[b], 0, MOE_NF + f)),
            pl.BlockSpec((1, 1, MOE_FT), lambda f, b, be, nu: (be[b], 0, f)),
            pl.BlockSpec((1, 1, MOE_FT), lambda f, b, be, nu: (be[b], 0, MOE_NF + f)),
        ],
        out_specs=pl.BlockSpec((bm, MOE_FT), lambda f, b, be, nu: (b, f)),
        scratch_shapes=[pltpu.VMEM((D_MODEL, MOE_FT), BF16), pltpu.VMEM((D_MODEL, MOE_FT), BF16)],
    )
    return pl.pallas_call(
        _moe_up_kernel,
        grid_spec=grid_spec,
        out_shape=jax.ShapeDtypeStruct((n_slots, D_FF), BF16),
        compiler_params=_cparams("arbitrary", "arbitrary"),
        name="moe_up",
    )(blk_e, n_used, xs, w1, w1, b1, b1)


def _moe_down_kernel(be_ref, nu_ref, a_ref, w_ref, b_ref, o_ref, w_s):
    b = pl.program_id(0)

    @pl.when(b < nu_ref[0])
    def _():
        @pl.when(_expert_changed(be_ref, b))
        def _():
            w_s[...] = w_ref[0].astype(BF16)

        o_ref[...] = _dot(a_ref[...], w_s[...]) + b_ref[0]

    @pl.when(b >= nu_ref[0])
    def _():
        o_ref[...] = jnp.zeros_like(o_ref)


def _moe_down(blk_e, n_used, act, w2, b2, bm):
    n_slots = act.shape[0]
    nblk = n_slots // bm
    live = lambda b, nu: jnp.minimum(b, nu[0] - 1)
    grid_spec = pltpu.PrefetchScalarGridSpec(
        num_scalar_prefetch=2,
        grid=(nblk,),
        in_specs=[
            pl.BlockSpec((bm, D_FF), lambda b, be, nu: (live(b, nu), 0)),
            pl.BlockSpec((1, D_FF, D_MODEL), lambda b, be, nu: (be[b], 0, 0)),
            pl.BlockSpec((1, 1, D_MODEL), lambda b, be, nu: (be[b], 0, 0)),
        ],
        out_specs=pl.BlockSpec((bm, D_MODEL), lambda b, be, nu: (b, 0)),
        scratch_shapes=[pltpu.VMEM((D_FF, D_MODEL), BF16)],
    )
    return pl.pallas_call(
        _moe_down_kernel,
        grid_spec=grid_spec,
        out_shape=jax.ShapeDtypeStruct((n_slots, D_MODEL), F32),
        compiler_params=_cparams("arbitrary"),
        name="moe_down",
    )(blk_e, n_used, act, w2, b2)


def _combine_kernel(cur_ref, nxt_ref, x_ref, tg_ref, g_ref, oh_ref, yb_ref, o_ref, buf, sem, *, nsteps):
    tm = x_ref.shape[0]
    slot = _gather_pipelined(cur_ref, nxt_ref, yb_ref, buf, sem, pl.program_id(0), nsteps)
    x = x_ref[...]
    gates = tg_ref[...]
    for k in range(MOE_TOPK):
        gk = _dot_exact01(gates, oh_ref[k])
        x = x + jnp.concatenate([gk] * (D_MODEL // LANES), axis=1) * buf[slot, k * tm:(k + 1) * tm]
    ms = jnp.mean(x * x, axis=-1, keepdims=True)
    o_ref[...] = x * lax.rsqrt(ms + NORM_EPS) * g_ref[...]


def _moe_combine(slot_of, x1, tg, g, yb, row0, t, tm=128):
    nt = t // tm
    blk0 = row0 // tm
    slots = slot_of[row0:row0 + t].reshape(nt, tm, MOE_TOPK).transpose(0, 2, 1).reshape(nt, 1, MOE_TOPK * tm)
    onehot = (jnp.arange(LANES)[None, :, None] == jnp.arange(MOE_TOPK)[:, None, None]).astype(BF16)
    onehot = jnp.broadcast_to(onehot, (MOE_TOPK, LANES, LANES))
    return pl.pallas_call(
        functools.partial(_combine_kernel, nsteps=nt),
        grid=(nt,),
        in_specs=[pl.BlockSpec((1, 1, MOE_TOPK * tm), lambda i: (i, 0, 0), memory_space=pltpu.SMEM),
                  pl.BlockSpec((1, 1, MOE_TOPK * tm), lambda i: (jnp.minimum(i + 1, nt - 1), 0, 0),
                               memory_space=pltpu.SMEM),
                  pl.BlockSpec((tm, D_MODEL), lambda i: (blk0 + i, 0)),
                  pl.BlockSpec((tm, LANES), lambda i: (blk0 + i, 0)),
                  pl.BlockSpec((1, D_MODEL), lambda i: (0, 0)),
                  pl.BlockSpec((MOE_TOPK, LANES, LANES), lambda i: (0, 0, 0)),
                  pl.BlockSpec(memory_space=pl.ANY)],
        out_specs=pl.BlockSpec((tm, D_MODEL), lambda i: (i, 0)),
        out_shape=jax.ShapeDtypeStruct((t, D_MODEL), F32),
        scratch_shapes=[pltpu.VMEM((2, MOE_TOPK * tm, D_MODEL), F32), pltpu.SemaphoreType.DMA((2,))],
        compiler_params=_cparams("arbitrary"),
        name="moe_combine",
    )(slots, slots, x1, tg, g.reshape(1, -1), onehot, yb)


def _head_perm():
    p, g, e, d = jnp.meshgrid(jnp.arange(2), jnp.arange(NSA_GROUP), jnp.arange(2), jnp.arange(HEAD_DIM),
                              indexing="ij")
    return (((2 * p + e) * NSA_GROUP + g) * HEAD_DIM + d).reshape(-1)


def _alibi_slopes():
    return 2.0 ** (-8.0 * jnp.arange(1, NSA_HEADS + 1, dtype=F32) / NSA_HEADS)


def _block_ones(width):
    i = jnp.arange(width)
    return (i[:, None] // HEAD_DIM == i[None, :] // HEAD_DIM).astype(BF16)


def _cmp_weights_bd(wk, wv):
    eye = jnp.eye(2, dtype=wk.dtype)
    bd = lambda w: jnp.einsum("hg,bde->bhdge", eye, w).reshape(CMP_BLOCK, LANES, LANES)
    return jnp.stack([bd(wk), bd(wv)]).astype(BF16)


def _pad_cols(a, width):
    return jnp.pad(a, ((0, 0), (0, width - a.shape[1])))


def kernel(x_prompt, x_sample, cache_nsa_kv, state_win_kv, state_rwkv, state_shift, page_table, norm_attn, w_in,
           nsa_w_cmp_k, nsa_w_cmp_v, nsa_out_gain, rwkv_mu, rwkv_w0, rwkv_w2, rwkv_a0, rwkv_a2, rwkv_g2, rwkv_k_k,
           rwkv_k_a, rwkv_r_k, rwkv_ln_w, rwkv_ln_b, w_out, norm_ffn, router_w, router_b, moe_w1, moe_b1, moe_w2,
           moe_b2, norm_final):
    nb_p, seq_p, _ = x_prompt.shape
    nb_s, seq_s, _ = x_sample.shape
    depth = norm_attn.shape[0]
    assert depth == 1 and seq_s == 8 and seq_p % (2 * Q_BLOCK) == 0
    tp, ts = nb_p * seq_p, nb_s * seq_s
    n_phys = cache_nsa_kv.shape[1]
    w_len = state_win_kv.shape[2]
    l = 0
    xp, xs = x_prompt.reshape(tp, D_MODEL), x_sample.reshape(ts, D_MODEL)

    perm = _head_perm()
    w = w_in[l]
    c0, c1, c2 = NSA_WIDTH, NSA_WIDTH + 4 * KV_WIDTH, NSA_WIDTH + 6 * KV_WIDTH
    c3 = c2 + 3 * NSA_HEADS
    q, kvn, winn, gl, z = _in_proj(xp, xs, norm_attn[l], [
        (w[:, :c0] * (HEAD_DIM ** -0.5))[:, perm].astype(BF16), w[:, c0:c1].astype(BF16), w[:, c1:c2].astype(BF16),
        _pad_cols(w[:, c2:c3], LANES).astype(BF16), _pad_cols(w[:, c3:], Z_PAD).astype(BF16)])

    w_bd = _cmp_weights_bd(nsa_w_cmp_k[l], nsa_w_cmp_v[l])
    cache_rows = jnp.transpose(cache_nsa_kv[l], (2, 0, 1, 3, 4)).reshape(4 * n_phys * PAGE_SIZE, KV_WIDTH)
    kvc_new = _compress(kvn, w_bd, tp // CMP_BLOCK, 256)
    blocks_phys = n_phys * (PAGE_SIZE // CMP_BLOCK)
    kvc_phys = _compress(cache_rows, w_bd, blocks_phys, 512, n_phys * PAGE_SIZE)
    o_p = _nsa_prompt(q, gl, kvn, winn, kvc_new, jnp.stack([_slope_cols(2, Q_BLOCK, p) for p in range(2)]),
                      _gain_cols(nsa_out_gain[l]), _pair01_t(seq_p // CMP_BLOCK), nb_p, seq_p)
    n_cmp_pad = -(-(page_table.shape[1] * PAGE_SIZE // CMP_BLOCK) // LANES) * LANES
    o_s, new_win_s = _nsa_sample(
        page_table, q, gl, kvn, winn, state_win_kv[l].reshape(nb_s, w_len, 2 * KV_WIDTH), cache_rows,
        kvc_phys.reshape(n_phys, 1, (PAGE_SIZE // CMP_BLOCK) * 2 * KV_WIDTH),
        _sample_consts(nsa_out_gain[l]), _pair01_t(n_cmp_pad), tp)
    o_nsa = jnp.concatenate([o_p, o_s.astype(BF16)], axis=0)

    w2p, a2p, g2p, pat = _rwkv_consts(rwkv_w2[l], rwkv_a2[l], rwkv_g2[l])
    mu = jnp.pad(rwkv_mu[l], (0, Z_PAD - SHIFT_WIDTH))
    vecs = (mu, rwkv_w0[l], rwkv_a0[l], rwkv_k_k[l], rwkv_k_a[l], rwkv_r_k[l], rwkv_ln_w[l], rwkv_ln_b[l])
    mats = (w2p, a2p, g2p, _block_ones(LANES), pat, _block_ones(2 * LANES))
    o_rw_p, st_p = _rwkv_mixer(z, 0, nb_p, seq_p, nb_p, 128, None, None, vecs, mats)
    o_rw_s, st_s = _rwkv_mixer(z, tp, nb_s, seq_s, 8, seq_s, _pad_cols(state_shift[l], Z_PAD),
                               _state_to_tiles(state_rwkv[l]), vecs, mats)
    o_rwkv = jnp.concatenate([o_rw_p.reshape(tp, RWKV_WIDTH), o_rw_s.reshape(ts, RWKV_WIDTH)], axis=0).astype(BF16)

    wo = w_out[l]
    rw = _pad_cols(router_w[l], LANES)
    rw_hi = rw.astype(BF16)
    rw_lo = (rw - rw_hi.astype(F32)).astype(BF16)
    rb = jnp.concatenate([router_b[l].astype(F32), jnp.full((LANES - N_EXPERTS,), -1e30, F32)]).reshape(1, LANES)
    x1, h2, top_e, top_g = _outproj_router(xp, xs, o_nsa, o_rwkv, wo[:NSA_WIDTH][perm].astype(BF16),
                                           wo[NSA_WIDTH:].astype(BF16), norm_ffn[l].reshape(1, -1), rw_hi, rw_lo, rb)
    slot_of, blk_e, n_used, n_slots = _route(top_e[:, :MOE_TOPK], MOE_BM)
    xs_sorted = _moe_scatter(slot_of, h2, n_slots)
    act = _moe_up(blk_e, n_used, xs_sorted, moe_w1[l], moe_b1[l].reshape(N_EXPERTS, 1, 2 * D_FF), MOE_BM)
    yb = _moe_down(blk_e, n_used, act, moe_w2[l], moe_b2[l].reshape(N_EXPERTS, 1, D_MODEL), MOE_BM)
    y_p = _moe_combine(slot_of, x1, top_g, norm_final, yb, 0, tp)
    y_s = _moe_combine(slot_of, x1, top_g, norm_final, yb, tp, ts)

    hd = (NSA_KV_HEADS, HEAD_DIM)
    kv_p = kvn[:tp].reshape(1, nb_p, seq_p, 4, *hd)
    kv_s = kvn[tp:].reshape(1, nb_s, seq_s, 4, *hd)
    win_keep = min(WINDOW, seq_p)
    win_p = winn[:tp].reshape(nb_p, seq_p, 2, *hd)[None, :, seq_p - win_keep:]
    win_s = new_win_s.reshape(1, nb_s, w_len, 2, *hd)
    sh_p = z[seq_p - 1:tp:seq_p, :SHIFT_WIDTH][None]
    sh_s = z[tp + seq_s - 1::seq_s, :SHIFT_WIDTH][None]
    return (y_p.reshape(nb_p, seq_p, D_MODEL), y_s.reshape(nb_s, seq_s, D_MODEL), kv_p, kv_s, win_p, win_s,
            _tiles_to_state(st_p)[None], _tiles_to_state(st_s)[None], sh_p, sh_s)
```
